```python
import math
import jax, jax.numpy as jnp
from jax import lax
import numpy as np

D_MODEL = 1024
BATCH = 2
SEQ = 8192
DEPTH = 4

CHUNK = 64
N_HEADS = 16
HEAD_DIM = 64
IDX_HEADS = 8
IDX_DIM = 64
TOPK_MAX = 256
A_QBLOCK = 64
B_QBLOCK = 128
ROPE_THETA = 10000.0
D_FF = 2816
N_EXPERTS = 8
TOP_K = 2
RMS_EPS = 1e-6

kernel_name = "yoco_dsa_fox_moe_trunk"


def rms_norm(x, g):
    xf = x.astype(jnp.float32)
    y = xf * lax.rsqrt(jnp.mean(xf * xf, axis=-1, keepdims=True) + RMS_EPS)
    return (y * g.astype(jnp.float32)).astype(x.dtype)


def rope_tables(seq_len, dim):
    pos = jnp.arange(seq_len, dtype=jnp.float32)
    inv_freq = 1.0 / (ROPE_THETA ** (jnp.arange(0, dim, 2, dtype=jnp.float32) / dim))
    ang = pos[:, None] * inv_freq[None, :]
    return jnp.cos(ang), jnp.sin(ang)


def apply_rope(x, cos, sin):
    half = x.shape[-1] // 2
    c = cos[None, :, None, :].astype(x.dtype)
    s = sin[None, :, None, :].astype(x.dtype)
    x1, x2 = x[..., :half], x[..., half:]
    return jnp.concatenate([x1 * c - x2 * s, x2 * c + x1 * s], axis=-1)


def to_blocks(a, qb):
    b, s = a.shape[0], a.shape[1]
    return a.reshape(b, s // qb, qb, *a.shape[2:]).swapaxes(0, 1)


def from_blocks(a):
    a = a.swapaxes(0, 1)
    return a.reshape(a.shape[0], a.shape[1] * a.shape[2], *a.shape[3:])


def swiglu(x, w_gate_up, w_down):
    gu = x @ w_gate_up
    g, u = jnp.split(gu, 2, axis=-1)
    return (jax.nn.silu(g) * u) @ w_down


def moe_swiglu(x, w_router, w_gate_up, w_down):
    logits = (x @ w_router).astype(jnp.float32)
    top_vals, top_idx = lax.top_k(logits, TOP_K)
    gates = jax.nn.softmax(top_vals, axis=-1)
    combine = jnp.sum(jax.nn.one_hot(top_idx, N_EXPERTS, dtype=jnp.float32)
                      * gates[..., None], axis=-2).astype(x.dtype)
    out = jnp.zeros_like(x)
    for e in range(N_EXPERTS):
        out = out + combine[..., e:e + 1] * swiglu(x, w_gate_up[e], w_down[e])
    return out


def mixer_a(hn, w_in, w_out, cos, sin):
    B, S, _ = hn.shape
    hd = N_HEADS * HEAD_DIM
    splits = [hd, 2 * hd, 3 * hd, 3 * hd + IDX_HEADS * IDX_DIM,
              3 * hd + IDX_HEADS * IDX_DIM + IDX_DIM]
    p = hn @ w_in
    q, k, v, qi, ki, wi = jnp.split(p, splits, axis=-1)
    q = apply_rope(q.reshape(B, S, N_HEADS, HEAD_DIM), cos, sin)
    k = apply_rope(k.reshape(B, S, N_HEADS, HEAD_DIM), cos, sin)
    v = v.reshape(B, S, N_HEADS, HEAD_DIM)
    qi = apply_rope(qi.reshape(B, S, IDX_HEADS, IDX_DIM), cos, sin)
    ki = apply_rope(ki[:, :, None, :], cos, sin)[:, :, 0, :]
    wi = wi * (IDX_HEADS ** -0.5)
    k_sel = min(TOPK_MAX, S // 4)
    scale = HEAD_DIM ** -0.5
    key_pos = jnp.arange(S)

    def block(args):
        qb, qib, wib, b0 = args
        t = b0 * A_QBLOCK + jnp.arange(A_QBLOCK)
        limit = (t // CHUNK + 1) * CHUNK
        visible = key_pos[None, :] < limit[:, None]
        sc = jnp.einsum('bqhd,bsd->bqhs', qib, ki).astype(jnp.float32)
        idx_score = jnp.einsum('bqhs,bqh->bqs', jax.nn.relu(sc), wib.astype(jnp.float32))
        idx_score = jnp.where(visible[None], idx_score, -jnp.inf)
        _, sel = lax.top_k(idx_score, k_sel)
        valid = sel < limit[None, :, None]
        kg = jax.vmap(lambda a, i: a[i])(k, sel)
        vg = jax.vmap(lambda a, i: a[i])(v, sel)
        logits = jnp.einsum('bqhd,bqkhd->bqhk', qb, kg).astype(jnp.float32) * scale
        logits = jnp.where(valid[:, :, None, :], logits, -jnp.inf)
        probs = jax.nn.softmax(logits, axis=-1).astype(qb.dtype)
        return jnp.einsum('bqhk,bqkhd->bqhd', probs, vg)

    nb = S // A_QBLOCK
    o = lax.map(block, (to_blocks(q, A_QBLOCK), to_blocks(qi, A_QBLOCK),
                        to_blocks(wi, A_QBLOCK), jnp.arange(nb)))
    o = from_blocks(o).reshape(B, S, N_HEADS * HEAD_DIM)
    return o @ w_out


def shared_kv(h, g, w_kv, b_f):
    B, S, _ = h.shape
    hd = N_HEADS * HEAD_DIM
    hn = rms_norm(h, g)
    p = hn @ w_kv
    k, v, f_logit = jnp.split(p, [hd, 2 * hd], axis=-1)
    k = k.reshape(B, S, N_HEADS, HEAD_DIM)
    v = v.reshape(B, S, N_HEADS, HEAD_DIM)
    log_f = jax.nn.log_sigmoid(f_logit.astype(jnp.float32) + b_f.astype(jnp.float32))
    cum_log_f = jnp.cumsum(log_f, axis=1)
    return k, v, cum_log_f


def mixer_b(hn, w_q, w_out, k, v, cum_log_f):
    B, S, _ = hn.shape
    q = (hn @ w_q).reshape(B, S, N_HEADS, HEAD_DIM)
    scale = HEAD_DIM ** -0.5
    f_keys = cum_log_f.transpose(0, 2, 1)
    key_pos = jnp.arange(S)

    def block(args):
        qb, fq, b0 = args
        t = b0 * B_QBLOCK + jnp.arange(B_QBLOCK)
        causal = key_pos[None, :] <= t[:, None]
        logits = jnp.einsum('bqhd,bshd->bhqs', qb, k).astype(jnp.float32) * scale
        logits = logits + fq.transpose(0, 2, 1)[..., None] - f_keys[:, :, None, :]
        logits = jnp.where(causal[None, None], logits, -jnp.inf)
        probs = jax.nn.softmax(logits, axis=-1).astype(qb.dtype)
        return jnp.einsum('bhqs,bshd->bqhd', probs, v)

    nb = S // B_QBLOCK
    o = lax.map(block, (to_blocks(q, B_QBLOCK), to_blocks(cum_log_f, B_QBLOCK),
                        jnp.arange(nb)))
    o = from_blocks(o).reshape(B, S, N_HEADS * HEAD_DIM)
    return o @ w_out


def setup_inputs(seed: int = 0) -> dict:
    key = jax.random.key(seed)
    ks = jax.random.split(key, 20)
    n_a = DEPTH // 2
    n_b = DEPTH - n_a
    n_dense = (DEPTH + 1) // 2
    n_moe = DEPTH // 2
    hd = N_HEADS * HEAD_DIM
    a_in_width = 3 * hd + IDX_HEADS * IDX_DIM + IDX_DIM + IDX_HEADS
    out_scale = (2.0 * DEPTH) ** -0.5

    def w(k, shape, fan_in, extra=1.0):
        return jax.random.normal(k, shape, jnp.float32) * (fan_in ** -0.5) * extra

    def gain(k, shape):
        return 1.0 + 0.02 * jax.random.normal(k, shape, jnp.float32)

    return {
        "x": jax.random.normal(ks[0], (BATCH, SEQ, D_MODEL), jnp.float32),
        "a_norm": gain(ks[1], (n_a, D_MODEL)),
        "a_w_in": w(ks[2], (n_a, D_MODEL, a_in_width), D_MODEL),
        "a_w_out": w(ks[3], (n_a, hd, D_MODEL), hd, out_scale),
        "kv_norm": gain(ks[4], (D_MODEL,)),
        "w_kv": w(ks[5], (D_MODEL, 2 * hd + N_HEADS), D_MODEL),
        "b_f": 3.0 + 0.5 * jax.random.normal(ks[6], (N_HEADS,), jnp.float32),
        "b_norm": gain(ks[7], (n_b, D_MODEL)),
        "b_w_q": w(ks[8], (n_b, D_MODEL, hd), D_MODEL),
        "b_w_out": w(ks[9], (n_b, hd, D_MODEL), hd, out_scale),
        "ffn_norm": gain(ks[10], (DEPTH, D_MODEL)),
        "dense_w_gate_up": w(ks[11], (n_dense, D_MODEL, 2 * D_FF), D_MODEL),
        "dense_w_down": w(ks[12], (n_dense, D_FF, D_MODEL), D_FF, out_scale),
        "moe_router": w(ks[13], (n_moe, D_MODEL, N_EXPERTS), D_MODEL),
        "moe_w_gate_up": w(ks[14], (n_moe, N_EXPERTS, D_MODEL, 2 * D_FF), D_MODEL),
        "moe_w_down": w(ks[15], (n_moe, N_EXPERTS, D_FF, D_MODEL), D_FF, out_scale),
        "final_norm": gain(ks[16], (D_MODEL,)),
    }


def reference(x, a_norm, a_w_in, a_w_out, kv_norm, w_kv, b_f, b_norm, b_w_q,
              b_w_out, ffn_norm, dense_w_gate_up, dense_w_down, moe_router,
              moe_w_gate_up, moe_w_down, final_norm):
    S = x.shape[1]
    n_a = DEPTH // 2
    cos, sin = rope_tables(S, HEAD_DIM)
    h = x
    k_sh = v_sh = f_sh = None
    for i in range(DEPTH):
        if i < n_a:
            h = h + mixer_a(rms_norm(h, a_norm[i]), a_w_in[i], a_w_out[i], cos, sin)
        else:
            if i == n_a:
                k_sh, v_sh, f_sh = shared_kv(h, kv_norm, w_kv, b_f)
            j = i - n_a
            h = h + mixer_b(rms_norm(h, b_norm[j]), b_w_q[j], b_w_out[j], k_sh, v_sh, f_sh)
        hn = rms_norm(h, ffn_norm[i])
        if i % 2 == 0:
            h = h + swiglu(hn, dense_w_gate_up[i // 2], dense_w_down[i // 2])
        else:
            h = h + moe_swiglu(hn, moe_router[i // 2], moe_w_gate_up[i // 2], moe_w_down[i // 2])
    return rms_norm(h, final_norm)
```

```python
import functools

import numpy as np
import jax
import jax.numpy as jnp
from jax import lax
from jax.experimental import pallas as pl
from jax.experimental.pallas import tpu as pltpu

N_HEADS = 16
HEAD_DIM = 64
IDX_HEADS = 8
IDX_DIM = 64
TOPK_MAX = 256
CHUNK = 64
ROPE_THETA = 10000.0
N_EXPERTS = 8
RMS_EPS = 1e-6

LANES = 128
HD = N_HEADS * HEAD_DIM
N_PAIRS = N_HEADS // 2
NEG = -1e30
VMEM_LIMIT = 52 * 1024 * 1024

F32 = jnp.float32
BF16 = jnp.bfloat16


def _cparams(n_axes):
    return pltpu.CompilerParams(dimension_semantics=("arbitrary",) * n_axes,
                                vmem_limit_bytes=VMEM_LIMIT)


def _rms(x, g):
    return x * lax.rsqrt(jnp.mean(x * x, axis=-1, keepdims=True) + RMS_EPS) * g


def _half_mask(shape):
    return lax.broadcasted_iota(jnp.int32, shape, len(shape) - 1) < HEAD_DIM


def _nt_dot(a, b):
    return lax.dot_general(a, b, (((1,), (1,)), ((), ())), preferred_element_type=F32)


def _rope_chunk(p, cos, sin_signed):
    fwd = pltpu.roll(p, LANES - 32, 1)
    bwd = pltpu.roll(p, 32, 1)
    lane = lax.broadcasted_iota(jnp.int32, p.shape, 1)
    partner = jnp.where((lane % HEAD_DIM) < HEAD_DIM // 2, fwd, bwd)
    return p * cos + partner * sin_signed


def _a_proj_kernel(x_ref, g_ref, w_ref, cos_ref, sin_ref,
                   q_ref, k_ref, v_ref, qi_ref, ki_ref, wi_ref):
    xn = _rms(x_ref[...], g_ref[...]).astype(BF16)
    cos = cos_ref[...]
    sin = sin_ref[...]
    seg = 512

    def proj(c0, width):
        return jnp.dot(xn, w_ref[:, c0:c0 + width], preferred_element_type=F32)

    def rope_store(p, out_ref, o0, scale):
        for j in range(p.shape[1] // LANES):
            r = _rope_chunk(p[:, j * LANES:(j + 1) * LANES], cos, sin)
            if scale != 1.0:
                r = r * scale
            out_ref[:, o0 + j * LANES:o0 + (j + 1) * LANES] = r.astype(out_ref.dtype)

    for s in range(HD // seg):
        rope_store(proj(s * seg, seg), q_ref, s * seg, HEAD_DIM ** -0.5)
    for s in range(HD // seg):
        rope_store(proj(HD + s * seg, seg), k_ref, s * seg, 1.0)
    for s in range(HD // seg):
        v_ref[:, s * seg:(s + 1) * seg] = proj(2 * HD + s * seg, seg).astype(BF16)
    rope_store(proj(3 * HD, IDX_HEADS * IDX_DIM), qi_ref, 0, 1.0)
    tail = proj(3 * HD + IDX_HEADS * IDX_DIM, 2 * LANES)
    rope_store(tail[:, :LANES], ki_ref, 0, 1.0)
    wi_ref[...] = tail[:, LANES:] * (IDX_HEADS ** -0.5)


def _a_proj(h, g, w, cos_t, sin_t, seq, tm=512):
    t, d = h.shape
    n_w = w.shape[1]
    tiles_per_seq = seq // tm
    row = lambda i: (i, 0)
    tab = lambda i: (i % tiles_per_seq, 0)
    const = lambda i: (0, 0)
    return pl.pallas_call(
        _a_proj_kernel,
        grid=(t // tm,),
        in_specs=[pl.BlockSpec((tm, d), row),
                  pl.BlockSpec((1, d), const),
                  pl.BlockSpec((d, n_w), const),
                  pl.BlockSpec((tm, LANES), tab),
                  pl.BlockSpec((tm, LANES), tab)],
        out_specs=[pl.BlockSpec((tm, HD), row),
                   pl.BlockSpec((tm, HD), row),
                   pl.BlockSpec((tm, HD), row),
                   pl.BlockSpec((tm, IDX_HEADS * IDX_DIM), row),
                   pl.BlockSpec((tm, LANES), row),
                   pl.BlockSpec((tm, LANES), row)],
        out_shape=[jax.ShapeDtypeStruct((t, HD), BF16),
                   jax.ShapeDtypeStruct((t, HD), BF16),
                   jax.ShapeDtypeStruct((t, HD), BF16),
                   jax.ShapeDtypeStruct((t, IDX_HEADS * IDX_DIM), BF16),
                   jax.ShapeDtypeStruct((t, LANES), BF16),
                   jax.ShapeDtypeStruct((t, LANES), F32)],
        compiler_params=_cparams(1),
        name="a_proj",
    )(h, g, w, cos_t, sin_t)


def _kv_proj_kernel(x_ref, g_ref, w_ref, bf_ref, k_ref, v_ref, cum_ref, carry_ref, *, tiles_per_seq):
    i = pl.program_id(0)
    tm = x_ref.shape[0]

    @pl.when(i % tiles_per_seq == 0)
    def _():
        carry_ref[...] = jnp.zeros_like(carry_ref)

    xn = _rms(x_ref[...], g_ref[...]).astype(BF16)
    seg = 512
    for s in range(HD // seg):
        k_ref[:, s * seg:(s + 1) * seg] = jnp.dot(
            xn, w_ref[:, s * seg:(s + 1) * seg], preferred_element_type=F32).astype(BF16)
    for s in range(HD // seg):
        v_ref[:, s * seg:(s + 1) * seg] = jnp.dot(
            xn, w_ref[:, HD + s * seg:HD + (s + 1) * seg], preferred_element_type=F32).astype(BF16)
    f_logit = jnp.dot(xn, w_ref[:, 2 * HD:2 * HD + LANES], preferred_element_type=F32)
    log_f = jax.nn.log_sigmoid(f_logit + bf_ref[...])
    r_i = lax.broadcasted_iota(jnp.int32, (tm, tm), 0)
    c_i = lax.broadcasted_iota(jnp.int32, (tm, tm), 1)
    tri = jnp.where(c_i <= r_i, 1.0, 0.0).astype(BF16)
    x1 = log_f.astype(BF16)
    rem = log_f - x1.astype(F32)
    x2 = rem.astype(BF16)
    x3 = (rem - x2.astype(F32)).astype(BF16)
    cum = (jnp.dot(tri, x1, preferred_element_type=F32)
           + jnp.dot(tri, x2, preferred_element_type=F32)
           + jnp.dot(tri, x3, preferred_element_type=F32)) + carry_ref[...]
    cum_ref[...] = cum
    carry_ref[...] = cum[tm - 1:tm, :]


def _kv_proj(h, g, w, b_f, seq, tm=512):
    t, d = h.shape
    row = lambda i: (i, 0)
    const = lambda i: (0, 0)
    return pl.pallas_call(
        functools.partial(_kv_proj_kernel, tiles_per_seq=seq // tm),
        grid=(t // tm,),
        in_specs=[pl.BlockSpec((tm, d), row),
                  pl.BlockSpec((1, d), const),
                  pl.BlockSpec((d, w.shape[1]), const),
                  pl.BlockSpec((1, LANES), const)],
        out_specs=[pl.BlockSpec((tm, HD), row),
                   pl.BlockSpec((tm, HD), row),
                   pl.BlockSpec((tm, LANES), row)],
        out_shape=[jax.ShapeDtypeStruct((t, HD), BF16),
                   jax.ShapeDtypeStruct((t, HD), BF16),
                   jax.ShapeDtypeStruct((t, LANES), F32)],
        scratch_shapes=[pltpu.VMEM((1, LANES), F32)],
        compiler_params=_cparams(1),
        name="kv_proj",
    )(h, g, w, b_f)


def _q_proj_kernel(x_ref, g_ref, w_ref, q_ref):
    xn = _rms(x_ref[...], g_ref[...]).astype(BF16)
    seg = 512
    for s in range(HD // seg):
        p = jnp.dot(xn, w_ref[:, s * seg:(s + 1) * seg], preferred_element_type=F32)
        q_ref[:, s * seg:(s + 1) * seg] = (p * (HEAD_DIM ** -0.5)).astype(BF16)


def _q_proj(h, g, w, tm=512):
    t, d = h.shape
    row = lambda i: (i, 0)
    const = lambda i: (0, 0)
    return pl.pallas_call(
        _q_proj_kernel,
        grid=(t // tm,),
        in_specs=[pl.BlockSpec((tm, d), row),
                  pl.BlockSpec((1, d), const),
                  pl.BlockSpec((d, HD), const)],
        out_specs=pl.BlockSpec((tm, HD), row),
        out_shape=jax.ShapeDtypeStruct((t, HD), BF16),
        compiler_params=_cparams(1),
        name="q_proj",
    )(h, g, w)


def _out_proj_kernel(o_ref, w_ref, h_ref, out_ref):
    out_ref[...] = h_ref[...] + jnp.dot(o_ref[...], w_ref[...], preferred_element_type=F32)


def _out_proj(o, w, h, tm=512):
    t, d = h.shape
    row = lambda i: (i, 0)
    const = lambda i: (0, 0)
    return pl.pallas_call(
        _out_proj_kernel,
        grid=(t // tm,),
        in_specs=[pl.BlockSpec((tm, HD), row),
                  pl.BlockSpec((HD, d), const),
                  pl.BlockSpec((tm, d), row)],
        out_specs=pl.BlockSpec((tm, d), row),
        out_shape=jax.ShapeDtypeStruct((t, d), F32),
        compiler_params=_cparams(1),
        name="out_proj",
    )(o, w, h)


def _attn_block(q_ref, k_ref, v_ref, m_scr, l_scr, acc_scr, bias_fn):
    for j in range(N_PAIRS):
        cols = slice(j * LANES, (j + 1) * LANES)
        qp = q_ref[:, cols]
        kp = k_ref[:, cols]
        vp = v_ref[:, cols]
        first = _half_mask(qp.shape)
        zero = jnp.zeros_like(qp)
        upd = []
        for half in range(2):
            h = 2 * j + half
            qh = jnp.where(first, qp, zero) if half == 0 else jnp.where(first, zero, qp)
            s = _nt_dot(qh, kp) + bias_fn(h)
            m_prev = m_scr[h]
            m_new = jnp.maximum(m_prev, jnp.max(s, axis=1, keepdims=True))
            alpha = jnp.exp(m_prev - m_new)
            p = jnp.exp(s - m_new)
            l_scr[h] = alpha * l_scr[h] + jnp.sum(p, axis=1, keepdims=True)
            m_scr[h] = m_new
            pv = jnp.dot(p.astype(BF16), vp, preferred_element_type=F32)
            upd.append((alpha, pv))
        acc = acc_scr[:, cols]
        first_f = _half_mask(acc.shape)
        acc_scr[:, cols] = jnp.where(first_f,
                                     upd[0][0] * acc + upd[0][1],
                                     upd[1][0] * acc + upd[1][1])


def _attn_init(m_scr, l_scr, acc_scr):
    m_scr[...] = jnp.full(m_scr.shape, NEG, F32)
    l_scr[...] = jnp.zeros_like(l_scr)
    acc_scr[...] = jnp.zeros_like(acc_scr)


def _attn_finish(o_ref, l_scr, acc_scr):
    for j in range(N_PAIRS):
        cols = slice(j * LANES, (j + 1) * LANES)
        acc = acc_scr[:, cols]
        denom = jnp.where(_half_mask(acc.shape), l_scr[2 * j], l_scr[2 * j + 1])
        o_ref[:, cols] = (acc / denom).astype(o_ref.dtype)


def _ordered_bits_to_f32(u):
    key = u ^ jnp.int32(-2 ** 31)
    bits = jnp.where(key >= 0, key, key ^ jnp.int32(0x7FFFFFFF))
    return lax.bitcast_convert_type(bits, F32)


def _dsa_kernel(b_s, qt_s, ph_s, kb_s, nkb_s,
                qi_ref, wi_ref, ki_ref, q_ref, k_ref, v_ref, o_ref,
                score_scr, m_scr, l_scr, acc_scr, *, k_sel):
    step = pl.program_id(0)
    qt = qt_s[step]
    phase = ph_s[step]
    kb = kb_s[step]
    nkb = nkb_s[step]
    tq = q_ref.shape[0]
    tk = k_ref.shape[0]

    row = lax.broadcasted_iota(jnp.int32, (tq, 1), 0) + qt * tq
    limit = (row // CHUNK + 1) * CHUNK

    @pl.when(phase == 0)
    def _index():
        kk = ki_ref[...]
        w = wi_ref[...]
        acc = jnp.zeros((tq, tk), F32)
        for j in range(IDX_HEADS // 2):
            qp = qi_ref[:, j * LANES:(j + 1) * LANES]
            first = _half_mask(qp.shape)
            zero = jnp.zeros_like(qp)
            for half in range(2):
                h = 2 * j + half
                qh = jnp.where(first, qp, zero) if half == 0 else jnp.where(first, zero, qp)
                sc = _nt_dot(qh, kk)
                acc = acc + jnp.maximum(sc, 0.0) * w[:, h:h + 1]
        key = lax.broadcasted_iota(jnp.int32, (1, tk), 1) + kb * tk
        score_scr[kb] = jnp.where(key < limit, acc, -jnp.inf)

    @pl.when(jnp.logical_and(phase == 0, kb == nkb - 1))
    def _select():
        def count(pred):
            def body(j, c):
                return c + jnp.sum(jnp.where(pred(score_scr[j]), 1.0, 0.0), axis=1, keepdims=True)
            return lax.fori_loop(0, nkb, body, jnp.zeros((tq, 1), F32))

        def bit_body(i, r):
            cand_u = r | jnp.left_shift(jnp.int32(1), 31 - i)
            cand = _ordered_bits_to_f32(cand_u)
            cnt = count(lambda blk: blk >= cand)
            return jnp.where(cnt >= k_sel, cand_u, r)

        r = lax.fori_loop(0, 32, bit_body, jnp.zeros((tq, 1), jnp.int32))
        thr = jnp.where(limit <= k_sel, -jnp.inf, _ordered_bits_to_f32(r))
        need = k_sel - count(lambda blk: blk > thr)

        sub = 256
        r_i = lax.broadcasted_iota(jnp.int32, (sub, sub), 0)
        c_i = lax.broadcasted_iota(jnp.int32, (sub, sub), 1)
        tri = jnp.where(r_i <= c_i, 1.0, 0.0).astype(BF16)

        def sel_body(j, carry):
            for c in range(tk // sub):
                cols = slice(c * sub, (c + 1) * sub)
                blk = score_scr[j, :, cols]
                key = lax.broadcasted_iota(jnp.int32, (1, sub), 1) + (j * tk + c * sub)
                eq = blk == thr
                rank = carry + jnp.dot(jnp.where(eq, 1.0, 0.0).astype(BF16), tri,
                                       preferred_element_type=F32)
                bias = jnp.where(eq, jnp.where(rank <= need, 0.0, NEG),
                                 jnp.where(blk > thr, 0.0, NEG))
                score_scr[j, :, cols] = jnp.where(key < limit, bias, NEG)
                carry = rank[:, sub - 1:sub]
            return carry

        lax.fori_loop(0, nkb, sel_body, jnp.zeros((tq, 1), F32))

    @pl.when(jnp.logical_and(phase == 1, kb == 0))
    def _init():
        _attn_init(m_scr, l_scr, acc_scr)

    @pl.when(phase == 1)
    def _attend():
        bias = score_scr[kb]
        _attn_block(q_ref, k_ref, v_ref, m_scr, l_scr, acc_scr, lambda h: bias)

    @pl.when(jnp.logical_and(phase == 1, kb == nkb - 1))
    def _finish():
        _attn_finish(o_ref, l_scr, acc_scr)


def _dsa_schedule(batch, seq, tq, tk):
    rows = []
    for b in range(batch):
        for qt in range(seq // tq):
            nkb = -(-((qt + 1) * tq) // tk)
            for phase in range(2):
                for kb in range(nkb):
                    rows.append((b, qt, phase, kb, nkb))
    return [jnp.asarray(c, jnp.int32) for c in np.asarray(rows, np.int32).T]


def _dsa_attention(qi, wi, ki, q, k, v, batch, seq, tq=256, tk=1024):
    t = q.shape[0]
    qt_per_b = seq // tq
    kb_per_b = seq // tk
    sched = _dsa_schedule(batch, seq, tq, tk)
    k_sel = min(TOPK_MAX, seq // 4)

    def q_row(i, b, qt, ph, kb, nkb):
        return (b[i] * qt_per_b + qt[i], 0)

    def ki_row(i, b, qt, ph, kb, nkb):
        return (b[i] * kb_per_b + jnp.where(ph[i] == 0, kb[i], nkb[i] - 1), 0)

    def kv_row(i, b, qt, ph, kb, nkb):
        return (b[i] * kb_per_b + jnp.where(ph[i] == 0, 0, kb[i]), 0)

    grid_spec = pltpu.PrefetchScalarGridSpec(
        num_scalar_prefetch=5,
        grid=(int(sched[0].shape[0]),),
        in_specs=[pl.BlockSpec((tq, IDX_HEADS * IDX_DIM), q_row),
                  pl.BlockSpec((tq, LANES), q_row),
                  pl.BlockSpec((tk, LANES), ki_row),
                  pl.BlockSpec((tq, HD), q_row),
                  pl.BlockSpec((tk, HD), kv_row),
                  pl.BlockSpec((tk, HD), kv_row)],
        out_specs=pl.BlockSpec((tq, HD), q_row),
        scratch_shapes=[pltpu.VMEM((kb_per_b, tq, tk), F32),
                        pltpu.VMEM((N_HEADS, tq, 1), F32),
                        pltpu.VMEM((N_HEADS, tq, 1), F32),
                        pltpu.VMEM((tq, HD), F32)],
    )
    return pl.pallas_call(
        functools.partial(_dsa_kernel, k_sel=k_sel),
        grid_spec=grid_spec,
        out_shape=jax.ShapeDtypeStruct((t, HD), BF16),
        compiler_params=_cparams(1),
        name="dsa_attention",
    )(*sched, qi, wi, ki, q, k, v)


def _fox_kernel(b_s, qt_s, kb_s, nkb_s, q_ref, fq_ref, k_ref, v_ref, fk_ref, o_ref,
                m_scr, l_scr, acc_scr):
    step = pl.program_id(0)
    qt = qt_s[step]
    kb = kb_s[step]
    nkb = nkb_s[step]
    tq = q_ref.shape[0]
    tk = k_ref.shape[0]

    @pl.when(kb == 0)
    def _init():
        _attn_init(m_scr, l_scr, acc_scr)

    row = lax.broadcasted_iota(jnp.int32, (tq, 1), 0) + qt * tq
    key = lax.broadcasted_iota(jnp.int32, (1, tk), 1) + kb * tk
    causal = jnp.where(key <= row, 0.0, NEG)
    fq = fq_ref[...]
    fk = fk_ref[0]

    def bias(h):
        return (fq[:, h:h + 1] - fk[h:h + 1, :]) + causal

    _attn_block(q_ref, k_ref, v_ref, m_scr, l_scr, acc_scr, bias)

    @pl.when(kb == nkb - 1)
    def _finish():
        _attn_finish(o_ref, l_scr, acc_scr)


def _fox_schedule(batch, seq, tq, tk):
    rows = []
    for b in range(batch):
        for qt in range(seq // tq):
            nkb = -(-((qt + 1) * tq) // tk)
            for kb in range(nkb):
                rows.append((b, qt, kb, nkb))
    return [jnp.asarray(c, jnp.int32) for c in np.asarray(rows, np.int32).T]


def _fox_attention(q, cum_f, cum_f_t, k, v, batch, seq, tq=256, tk=1024):
    t = q.shape[0]
    qt_per_b = seq // tq
    kb_per_b = seq // tk
    sched = _fox_schedule(batch, seq, tq, tk)

    def q_row(i, b, qt, kb, nkb):
        return (b[i] * qt_per_b + qt[i], 0)

    def kv_row(i, b, qt, kb, nkb):
        return (b[i] * kb_per_b + kb[i], 0)

    def fk_row(i, b, qt, kb, nkb):
        return (b[i], 0, kb[i])

    grid_spec = pltpu.PrefetchScalarGridSpec(
        num_scalar_prefetch=4,
        grid=(int(sched[0].shape[0]),),
        in_specs=[pl.BlockSpec((tq, HD), q_row),
                  pl.BlockSpec((tq, LANES), q_row),
                  pl.BlockSpec((tk, HD), kv_row),
                  pl.BlockSpec((tk, HD), kv_row),
                  pl.BlockSpec((1, N_HEADS, tk), fk_row)],
        out_specs=pl.BlockSpec((tq, HD), q_row),
        scratch_shapes=[pltpu.VMEM((N_HEADS, tq, 1), F32),
                        pltpu.VMEM((N_HEADS, tq, 1), F32),
                        pltpu.VMEM((tq, HD), F32)],
    )
    return pl.pallas_call(
        _fox_kernel,
        grid_spec=grid_spec,
        out_shape=jax.ShapeDtypeStruct((t, HD), BF16),
        compiler_params=_cparams(1),
        name="fox_attention",
    )(*sched, q, cum_f, k, v, cum_f_t)


def _swiglu_partial(xn, wgu_ref, wd_ref, row_scale):
    tf = wd_ref.shape[-2]
    wgu = wgu_ref[(0,) * (len(wgu_ref.shape) - 2)]
    wd = wd_ref[(0,) * (len(wd_ref.shape) - 2)]
    gu = jnp.dot(xn, wgu, preferred_element_type=F32)
    g = gu[:, :tf]
    u = gu[:, tf:]
    a = g * jax.nn.sigmoid(g) * u
    if row_scale is not None:
        a = a * row_scale
    return jnp.dot(a.astype(BF16), wd, preferred_element_type=F32)


def _ffn_kernel(x_ref, g_ref, wgu_ref, wd_ref, o_ref, xn_scr, acc_scr):
    f = pl.program_id(1)

    @pl.when(f == 0)
    def _():
        xn_scr[...] = _rms(x_ref[...], g_ref[...]).astype(BF16)
        acc_scr[...] = jnp.zeros_like(acc_scr)

    acc_scr[...] += _swiglu_partial(xn_scr[...], wgu_ref, wd_ref, None)

    @pl.when(f == pl.num_programs(1) - 1)
    def _():
        o_ref[...] = x_ref[...] + acc_scr[...]


def _ffn(h, g, wgu, wd, tm=512):
    t, d = h.shape
    n_f, _, two_tf = wgu.shape
    tf = two_tf // 2
    return pl.pallas_call(
        _ffn_kernel,
        grid=(t // tm, n_f),
        in_specs=[pl.BlockSpec((tm, d), lambda i, f: (i, 0)),
                  pl.BlockSpec((1, d), lambda i, f: (0, 0)),
                  pl.BlockSpec((1, d, two_tf), lambda i, f: (f, 0, 0)),
                  pl.BlockSpec((tf, d), lambda i, f: (f, 0))],
        out_specs=pl.BlockSpec((tm, d), lambda i, f: (i, 0)),
        out_shape=jax.ShapeDtypeStruct((t, d), F32),
        scratch_shapes=[pltpu.VMEM((tm, d), BF16), pltpu.VMEM((tm, d), F32)],
        compiler_params=_cparams(2),
        name="ffn_dense",
    )(h, g, wgu, wd)


def _moe_kernel(x_ref, g_ref, wr_ref, wgu_ref, wd_ref, gf_ref, o_ref,
                xn_scr, comb_scr, acc_scr, *, final_norm):
    e = pl.program_id(1)
    f = pl.program_id(2)
    tm = x_ref.shape[0]

    @pl.when(jnp.logical_and(e == 0, f == 0))
    def _route():
        xn = _rms(x_ref[...], g_ref[...])
        xn_scr[...] = xn.astype(BF16)
        acc_scr[...] = jnp.zeros_like(acc_scr)
        logits = jnp.dot(xn, wr_ref[...], preferred_element_type=F32,
                         precision=lax.Precision.HIGHEST)
        lane = lax.broadcasted_iota(jnp.int32, (tm, LANES), 1)
        lg = jnp.where(lane < N_EXPERTS, logits, -jnp.inf)
        m1 = jnp.max(lg, axis=1, keepdims=True)
        i1 = jnp.min(jnp.where(lg == m1, lane, LANES), axis=1, keepdims=True)
        lg2 = jnp.where(lane == i1, -jnp.inf, lg)
        m2 = jnp.max(lg2, axis=1, keepdims=True)
        i2 = jnp.min(jnp.where(lg2 == m2, lane, LANES), axis=1, keepdims=True)
        e2 = jnp.exp(m2 - m1)
        den = 1.0 + e2
        comb_scr[...] = (jnp.where(lane == i1, 1.0 / den, 0.0)
                         + jnp.where(lane == i2, e2 / den, 0.0))

    lane = lax.broadcasted_iota(jnp.int32, (tm, LANES), 1)
    gate = jnp.sum(jnp.where(lane == e, comb_scr[...], 0.0), axis=1, keepdims=True)
    acc_scr[...] += _swiglu_partial(xn_scr[...], wgu_ref, wd_ref, gate)

    @pl.when(jnp.logical_and(e == pl.num_programs(1) - 1, f == pl.num_programs(2) - 1))
    def _():
        y = x_ref[...] + acc_scr[...]
        if final_norm:
            y = _rms(y, gf_ref[...])
        o_ref[...] = y


def _moe(h, g, w_router, wgu, wd, g_final, final_norm, tm=512):
    t, d = h.shape
    n_e, n_f, _, two_tf = wgu.shape
    tf = two_tf // 2
    return pl.pallas_call(
        functools.partial(_moe_kernel, final_norm=final_norm),
        grid=(t // tm, n_e, n_f),
        in_specs=[pl.BlockSpec((tm, d), lambda i, e, f: (i, 0)),
                  pl.BlockSpec((1, d), lambda i, e, f: (0, 0)),
                  pl.BlockSpec((d, LANES), lambda i, e, f: (0, 0)),
                  pl.BlockSpec((1, 1, d, two_tf), lambda i, e, f: (e, f, 0, 0)),
                  pl.BlockSpec((1, tf, d), lambda i, e, f: (e, f, 0)),
                  pl.BlockSpec((1, d), lambda i, e, f: (0, 0))],
        out_specs=pl.BlockSpec((tm, d), lambda i, e, f: (i, 0)),
        out_shape=jax.ShapeDtypeStruct((t, d), F32),
        scratch_shapes=[pltpu.VMEM((tm, d), BF16), pltpu.VMEM((tm, LANES), F32),
                        pltpu.VMEM((tm, d), F32)],
        compiler_params=_cparams(3),
        name="moe",
    )(h, g, w_router, wgu, wd, g_final)


def _split_gate_up(w, n_f):
    d, two_f = w.shape[-2:]
    tf = two_f // 2 // n_f
    g = w[..., :two_f // 2].reshape(*w.shape[:-1], n_f, tf)
    u = w[..., two_f // 2:].reshape(*w.shape[:-1], n_f, tf)
    gu = jnp.concatenate([g, u], axis=-1)
    return jnp.moveaxis(gu, -2, -3).astype(BF16)


def _rope_tables(seq):
    pos = jnp.arange(seq, dtype=F32)
    inv_freq = 1.0 / (ROPE_THETA ** (jnp.arange(0, HEAD_DIM, 2, dtype=F32) / HEAD_DIM))
    ang = pos[:, None] * inv_freq[None, :]
    cos, sin = jnp.cos(ang), jnp.sin(ang)
    cos_t = jnp.tile(cos, (1, LANES // (HEAD_DIM // 2)))
    sin_t = jnp.tile(jnp.concatenate([-sin, sin], axis=1), (1, LANES // HEAD_DIM))
    return cos_t, sin_t


def kernel(x, a_norm, a_w_in, a_w_out, kv_norm, w_kv, b_f, b_norm, b_w_q, b_w_out, ffn_norm,
           dense_w_gate_up, dense_w_down, moe_router, moe_w_gate_up, moe_w_down, final_norm):
    batch, seq, d = x.shape
    depth = ffn_norm.shape[0]
    n_a = a_norm.shape[0]
    n_f = 2
    cos_t, sin_t = _rope_tables(seq)
    idx_w = IDX_HEADS * IDX_DIM

    h = x.reshape(batch * seq, d)
    k_sh = v_sh = f_sh = f_sh_t = None
    for i in range(depth):
        if i < n_a:
            w = a_w_in[i]
            ki_w = w[:, 3 * HD + idx_w:3 * HD + idx_w + IDX_DIM]
            wi_w = w[:, 3 * HD + idx_w + IDX_DIM:]
            w_p = jnp.concatenate(
                [w[:, :3 * HD + idx_w], ki_w, ki_w, wi_w,
                 jnp.zeros((d, LANES - IDX_HEADS), w.dtype)], axis=1).astype(BF16)
            q, k, v, qi, ki, wi = _a_proj(h, a_norm[i][None, :], w_p, cos_t, sin_t, seq)
            o = _dsa_attention(qi, wi, ki, q, k, v, batch, seq)
            h = _out_proj(o, a_w_out[i].astype(BF16), h)
        else:
            j = i - n_a
            if k_sh is None:
                w_p = jnp.concatenate(
                    [w_kv, jnp.zeros((d, LANES - N_HEADS), w_kv.dtype)], axis=1).astype(BF16)
                b_p = jnp.concatenate([b_f, jnp.zeros((LANES - N_HEADS,), b_f.dtype)])[None, :]
                k_sh, v_sh, f_sh = _kv_proj(h, kv_norm[None, :], w_p, b_p, seq)
                f_sh_t = f_sh[:, :N_HEADS].reshape(batch, seq, N_HEADS).transpose(0, 2, 1)
            q = _q_proj(h, b_norm[j][None, :], b_w_q[j].astype(BF16))
            o = _fox_attention(q, f_sh, f_sh_t, k_sh, v_sh, batch, seq)
            h = _out_proj(o, b_w_out[j].astype(BF16), h)
        if i % 2 == 0:
            wd = dense_w_down[i // 2].astype(BF16)
            h = _ffn(h, ffn_norm[i][None, :], _split_gate_up(dense_w_gate_up[i // 2], n_f), wd)
        else:
            e = i // 2
            wr = jnp.concatenate(
                [moe_router[e], jnp.zeros((d, LANES - N_EXPERTS), moe_router.dtype)], axis=1)
            wd = moe_w_down[e].astype(BF16).reshape(N_EXPERTS, -1, d)
            last = i == depth - 1
            h = _moe(h, ffn_norm[i][None, :], wr, _split_gate_up(moe_w_gate_up[e], n_f), wd,
                     final_norm[None, :], last)
    if depth % 2 == 1:
        raise NotImplementedError("final norm is fused into the last expert mixer")
    return h.reshape(batch, seq, d)
```

```python
import functools
import math

import numpy as np
import jax
import jax.numpy as jnp
from jax import lax
from jax.experimental import pallas as pl
from jax.experimental.pallas import tpu as pltpu

N_HEADS = 16
HEAD_DIM = 64
IDX_HEADS = 8
IDX_DIM = 64
TOPK_MAX = 256
CHUNK = 64
ROPE_THETA = 10000.0
N_EXPERTS = 8
RMS_EPS = 1e-6

LANES = 128
HD = N_HEADS * HEAD_DIM
N_PAIRS = N_HEADS // 2
NEG = -1e30
LOG2E = math.log2(math.e)
VMEM_LIMIT = 52 * 1024 * 1024

F32 = jnp.float32
BF16 = jnp.bfloat16


def _cparams(n_axes):
    return pltpu.CompilerParams(dimension_semantics=("arbitrary",) * n_axes,
                                vmem_limit_bytes=VMEM_LIMIT)


def _rms(x, g):
    return x * lax.rsqrt(jnp.mean(x * x, axis=-1, keepdims=True) + RMS_EPS) * g


def _half_mask(shape):
    return lax.broadcasted_iota(jnp.int32, shape, len(shape) - 1) < HEAD_DIM


def _nt_dot(a, b):
    return lax.dot_general(a, b, (((1,), (1,)), ((), ())), preferred_element_type=F32)


def _store_values_with_ones(vx_ref, p, c0):
    ones = jnp.ones((p.shape[0], LANES), BF16)
    for j in range(p.shape[1] // LANES):
        pair = (c0 // LANES) + j
        vx_ref[:, 2 * pair * LANES:(2 * pair + 1) * LANES] = p[:, j * LANES:(j + 1) * LANES].astype(BF16)
        vx_ref[:, (2 * pair + 1) * LANES:(2 * pair + 2) * LANES] = ones


def _rope_chunk(p, cos, sin_signed):
    fwd = pltpu.roll(p, LANES - 32, 1)
    bwd = pltpu.roll(p, 32, 1)
    lane = lax.broadcasted_iota(jnp.int32, p.shape, 1)
    partner = jnp.where((lane % HEAD_DIM) < HEAD_DIM // 2, fwd, bwd)
    return p * cos + partner * sin_signed


def _a_proj_kernel(x_ref, g_ref, w_ref, cos_ref, sin_ref,
                   q_ref, k_ref, vx_ref, qi_ref, ki_ref, wi_ref):
    xn = _rms(x_ref[...], g_ref[...]).astype(BF16)
    cos = cos_ref[...]
    sin = sin_ref[...]
    seg = 512

    def proj(c0, width):
        return jnp.dot(xn, w_ref[:, c0:c0 + width], preferred_element_type=F32)

    def rope_store(p, out_ref, o0, scale):
        for j in range(p.shape[1] // LANES):
            r = _rope_chunk(p[:, j * LANES:(j + 1) * LANES], cos, sin)
            if scale != 1.0:
                r = r * scale
            out_ref[:, o0 + j * LANES:o0 + (j + 1) * LANES] = r.astype(out_ref.dtype)

    for s in range(HD // seg):
        rope_store(proj(s * seg, seg), q_ref, s * seg, HEAD_DIM ** -0.5 * LOG2E)
    for s in range(HD // seg):
        rope_store(proj(HD + s * seg, seg), k_ref, s * seg, 1.0)
    for s in range(HD // seg):
        _store_values_with_ones(vx_ref, proj(2 * HD + s * seg, seg), s * seg)
    rope_store(proj(3 * HD, IDX_HEADS * IDX_DIM), qi_ref, 0, 1.0)
    tail = proj(3 * HD + IDX_HEADS * IDX_DIM, 2 * LANES)
    rope_store(tail[:, :LANES], ki_ref, 0, 1.0)
    wi_ref[...] = tail[:, LANES:] * (IDX_HEADS ** -0.5)


def _a_proj(h, g, w, cos_t, sin_t, seq, tm=512):
    t, d = h.shape
    n_w = w.shape[1]
    tiles_per_seq = seq // tm
    row = lambda i: (i, 0)
    tab = lambda i: (i % tiles_per_seq, 0)
    const = lambda i: (0, 0)
    return pl.pallas_call(
        _a_proj_kernel,
        grid=(t // tm,),
        in_specs=[pl.BlockSpec((tm, d), row),
                  pl.BlockSpec((1, d), const),
                  pl.BlockSpec((d, n_w), const),
                  pl.BlockSpec((tm, LANES), tab),
                  pl.BlockSpec((tm, LANES), tab)],
        out_specs=[pl.BlockSpec((tm, HD), row),
                   pl.BlockSpec((tm, HD), row),
                   pl.BlockSpec((tm, 2 * HD), row),
                   pl.BlockSpec((tm, IDX_HEADS * IDX_DIM), row),
                   pl.BlockSpec((tm, LANES), row),
                   pl.BlockSpec((tm, LANES), row)],
        out_shape=[jax.ShapeDtypeStruct((t, HD), BF16),
                   jax.ShapeDtypeStruct((t, HD), BF16),
                   jax.ShapeDtypeStruct((t, 2 * HD), BF16),
                   jax.ShapeDtypeStruct((t, IDX_HEADS * IDX_DIM), BF16),
                   jax.ShapeDtypeStruct((t, LANES), BF16),
                   jax.ShapeDtypeStruct((t, LANES), F32)],
        compiler_params=_cparams(1),
        name="a_proj",
    )(h, g, w, cos_t, sin_t)


def _kv_proj_kernel(x_ref, g_ref, w_ref, bf_ref, k_ref, vx_ref, cum_ref, carry_ref, *, tiles_per_seq):
    i = pl.program_id(0)
    tm = x_ref.shape[0]

    @pl.when(i % tiles_per_seq == 0)
    def _():
        carry_ref[...] = jnp.zeros_like(carry_ref)

    xn = _rms(x_ref[...], g_ref[...]).astype(BF16)
    seg = 512
    for s in range(HD // seg):
        k_ref[:, s * seg:(s + 1) * seg] = jnp.dot(
            xn, w_ref[:, s * seg:(s + 1) * seg], preferred_element_type=F32).astype(BF16)
    for s in range(HD // seg):
        _store_values_with_ones(
            vx_ref, jnp.dot(xn, w_ref[:, HD + s * seg:HD + (s + 1) * seg], preferred_element_type=F32),
            s * seg)
    f_logit = jnp.dot(xn, w_ref[:, 2 * HD:2 * HD + LANES], preferred_element_type=F32)
    log_f = jax.nn.log_sigmoid(f_logit + bf_ref[...])
    r_i = lax.broadcasted_iota(jnp.int32, (tm, tm), 0)
    c_i = lax.broadcasted_iota(jnp.int32, (tm, tm), 1)
    tri = jnp.where(c_i <= r_i, 1.0, 0.0).astype(BF16)
    x1 = log_f.astype(BF16)
    rem = log_f - x1.astype(F32)
    x2 = rem.astype(BF16)
    x3 = (rem - x2.astype(F32)).astype(BF16)
    cum = (jnp.dot(tri, x1, preferred_element_type=F32)
           + jnp.dot(tri, x2, preferred_element_type=F32)
           + jnp.dot(tri, x3, preferred_element_type=F32)) + carry_ref[...]
    cum_ref[...] = cum * LOG2E
    carry_ref[...] = cum[tm - 1:tm, :]


def _kv_proj(h, g, w, b_f, seq, tm=512):
    t, d = h.shape
    row = lambda i: (i, 0)
    const = lambda i: (0, 0)
    return pl.pallas_call(
        functools.partial(_kv_proj_kernel, tiles_per_seq=seq // tm),
        grid=(t // tm,),
        in_specs=[pl.BlockSpec((tm, d), row),
                  pl.BlockSpec((1, d), const),
                  pl.BlockSpec((d, w.shape[1]), const),
                  pl.BlockSpec((1, LANES), const)],
        out_specs=[pl.BlockSpec((tm, HD), row),
                   pl.BlockSpec((tm, 2 * HD), row),
                   pl.BlockSpec((tm, LANES), row)],
        out_shape=[jax.ShapeDtypeStruct((t, HD), BF16),
                   jax.ShapeDtypeStruct((t, 2 * HD), BF16),
                   jax.ShapeDtypeStruct((t, LANES), F32)],
        scratch_shapes=[pltpu.VMEM((1, LANES), F32)],
        compiler_params=_cparams(1),
        name="kv_proj",
    )(h, g, w, b_f)


def _q_proj_kernel(x_ref, g_ref, w_ref, q_ref):
    xn = _rms(x_ref[...], g_ref[...]).astype(BF16)
    seg = 512
    for s in range(HD // seg):
        p = jnp.dot(xn, w_ref[:, s * seg:(s + 1) * seg], preferred_element_type=F32)
        q_ref[:, s * seg:(s + 1) * seg] = (p * (HEAD_DIM ** -0.5 * LOG2E)).astype(BF16)


def _q_proj(h, g, w, tm=512):
    t, d = h.shape
    row = lambda i: (i, 0)
    const = lambda i: (0, 0)
    return pl.pallas_call(
        _q_proj_kernel,
        grid=(t // tm,),
        in_specs=[pl.BlockSpec((tm, d), row),
                  pl.BlockSpec((1, d), const),
                  pl.BlockSpec((d, HD), const)],
        out_specs=pl.BlockSpec((tm, HD), row),
        out_shape=jax.ShapeDtypeStruct((t, HD), BF16),
        compiler_params=_cparams(1),
        name="q_proj",
    )(h, g, w)


def _out_proj_kernel(o_ref, w_ref, h_ref, out_ref):
    out_ref[...] = h_ref[...] + jnp.dot(o_ref[...], w_ref[...], preferred_element_type=F32)


def _out_proj(o, w, h, tm=512):
    t, d = h.shape
    row = lambda i: (i, 0)
    const = lambda i: (0, 0)
    return pl.pallas_call(
        _out_proj_kernel,
        grid=(t // tm,),
        in_specs=[pl.BlockSpec((tm, HD), row),
                  pl.BlockSpec((HD, d), const),
                  pl.BlockSpec((tm, d), row)],
        out_specs=pl.BlockSpec((tm, d), row),
        out_shape=jax.ShapeDtypeStruct((t, d), F32),
        compiler_params=_cparams(1),
        name="out_proj",
    )(o, w, h)


def _stack_pair_queries(q_ref, q2_scr):
    tq = q_ref.shape[0]
    for j in range(N_PAIRS):
        qp = q_ref[:, j * LANES:(j + 1) * LANES]
        first = _half_mask(qp.shape)
        zero = jnp.zeros_like(qp)
        q2_scr[j, 0:tq, :] = jnp.where(first, qp, zero)
        q2_scr[j, tq:2 * tq, :] = jnp.where(first, zero, qp)


def _attn_init(m_scr, l_scr, acc_scr):
    m_scr[...] = jnp.full(m_scr.shape, NEG, F32)
    l_scr[...] = jnp.zeros_like(l_scr)
    acc_scr[...] = jnp.zeros_like(acc_scr)


def _attn_block(q2_scr, k_ref, vx_ref, m_scr, l_scr, acc_scr, s_scr, p_scr, bias_fn):
    tq = s_scr.shape[0] // 2
    for j in range(N_PAIRS):
        kp = k_ref[:, j * LANES:(j + 1) * LANES]
        sv = _nt_dot(q2_scr[j], kp)
        for half in range(2):
            rows = slice(half * tq, (half + 1) * tq)
            sh = bias_fn(sv[rows], 2 * j + half)
            s_scr[rows, :] = sh
            m_prev = m_scr[j, rows, :]
            m_new = jnp.maximum(m_prev, jnp.max(sh, axis=1, keepdims=True))
            m_scr[j, rows, :] = m_new
            alpha = jnp.exp2(m_prev - m_new)
            p_scr[rows, :] = jnp.exp2(s_scr[rows, :] - m_new).astype(BF16)
            acc_scr[j, rows, :] = alpha * acc_scr[j, rows, :]
            l_scr[j, rows, :] = alpha * l_scr[j, rows, :]
        pv = jnp.dot(p_scr[...], vx_ref[:, 2 * j * LANES:2 * (j + 1) * LANES],
                     preferred_element_type=F32)
        acc_scr[j] += pv[:, :LANES]
        l_scr[j] += pv[:, LANES:]


def _attn_finish(o_ref, l_scr, acc_scr):
    tq = o_ref.shape[0]
    for j in range(N_PAIRS):
        o2 = acc_scr[j] / l_scr[j]
        o_ref[:, j * LANES:(j + 1) * LANES] = jnp.where(
            _half_mask((tq, LANES)), o2[0:tq], o2[tq:2 * tq]).astype(o_ref.dtype)


def _attn_scratch(tq, tk):
    return [pltpu.VMEM((N_PAIRS, 2 * tq, LANES), BF16),
            pltpu.VMEM((N_PAIRS, 2 * tq, 1), F32),
            pltpu.VMEM((N_PAIRS, 2 * tq, LANES), F32),
            pltpu.VMEM((N_PAIRS, 2 * tq, LANES), F32),
            pltpu.VMEM((2 * tq, tk), F32),
            pltpu.VMEM((2 * tq, tk), BF16)]


def _ordered_bits_to_f32(u):
    key = u ^ jnp.int32(-2 ** 31)
    bits = jnp.where(key >= 0, key, key ^ jnp.int32(0x7FFFFFFF))
    return lax.bitcast_convert_type(bits, F32)


def _dsa_kernel(b_s, qt_s, ph_s, kb_s, nkb_s,
                qi_ref, wi_ref, ki_ref, q_ref, k_ref, vx_ref, o_ref,
                score_scr, q2_scr, m_scr, l_scr, acc_scr, s_scr, p_scr, *, k_sel):
    step = pl.program_id(0)
    qt = qt_s[step]
    phase = ph_s[step]
    kb = kb_s[step]
    nkb = nkb_s[step]
    tq = q_ref.shape[0]
    tk = k_ref.shape[0]

    row = lax.broadcasted_iota(jnp.int32, (tq, 1), 0) + qt * tq
    limit = (row // CHUNK + 1) * CHUNK

    @pl.when(phase == 0)
    def _index():
        kk = ki_ref[...]
        w = wi_ref[...]
        acc = jnp.zeros((tq, tk), F32)
        for j in range(IDX_HEADS // 2):
            qp = qi_ref[:, j * LANES:(j + 1) * LANES]
            first = _half_mask(qp.shape)
            zero = jnp.zeros_like(qp)
            for half in range(2):
                h = 2 * j + half
                qh = jnp.where(first, qp, zero) if half == 0 else jnp.where(first, zero, qp)
                sc = _nt_dot(qh, kk)
                acc = acc + jnp.maximum(sc, 0.0) * w[:, h:h + 1]
        key = lax.broadcasted_iota(jnp.int32, (1, tk), 1) + kb * tk
        score_scr[kb] = jnp.where(key < limit, acc, -jnp.inf)

    @pl.when(jnp.logical_and(phase == 0, kb == nkb - 1))
    def _select():
        def count(pred):
            def body(j, c):
                return c + jnp.sum(jnp.where(pred(score_scr[j]), 1.0, 0.0), axis=1, keepdims=True)
            return lax.fori_loop(0, nkb, body, jnp.zeros((tq, 1), F32))

        def bit_body(i, r):
            cand_u = r | jnp.left_shift(jnp.int32(1), 31 - i)
            cand = _ordered_bits_to_f32(cand_u)
            cnt = count(lambda blk: blk >= cand)
            return jnp.where(cnt >= k_sel, cand_u, r)

        r = lax.fori_loop(0, 32, bit_body, jnp.zeros((tq, 1), jnp.int32))
        thr = jnp.where(limit <= k_sel, -jnp.inf, _ordered_bits_to_f32(r))
        need = k_sel - count(lambda blk: blk > thr)

        sub = 256
        r_i = lax.broadcasted_iota(jnp.int32, (sub, sub), 0)
        c_i = lax.broadcasted_iota(jnp.int32, (sub, sub), 1)
        tri = jnp.where(r_i <= c_i, 1.0, 0.0).astype(BF16)

        def sel_body(j, carry):
            for c in range(tk // sub):
                cols = slice(c * sub, (c + 1) * sub)
                blk = score_scr[j, :, cols]
                key = lax.broadcasted_iota(jnp.int32, (1, sub), 1) + (j * tk + c * sub)
                eq = blk == thr
                rank = carry + jnp.dot(jnp.where(eq, 1.0, 0.0).astype(BF16), tri,
                                       preferred_element_type=F32)
                bias = jnp.where(eq, jnp.where(rank <= need, 0.0, NEG),
                                 jnp.where(blk > thr, 0.0, NEG))
                score_scr[j, :, cols] = jnp.where(key < limit, bias, NEG)
                carry = rank[:, sub - 1:sub]
            return carry

        lax.fori_loop(0, nkb, sel_body, jnp.zeros((tq, 1), F32))

    @pl.when(jnp.logical_and(phase == 1, kb == 0))
    def _init():
        _stack_pair_queries(q_ref, q2_scr)
        _attn_init(m_scr, l_scr, acc_scr)

    @pl.when(phase == 1)
    def _attend():
        _attn_block(q2_scr, k_ref, vx_ref, m_scr, l_scr, acc_scr, s_scr, p_scr,
                    lambda sv, h: sv + score_scr[kb])

    @pl.when(jnp.logical_and(phase == 1, kb == nkb - 1))
    def _finish():
        _attn_finish(o_ref, l_scr, acc_scr)


def _dsa_schedule(batch, seq, tq, tk):
    rows = []
    for b in range(batch):
        for qt in range(seq // tq):
            nkb = -(-((qt + 1) * tq) // tk)
            for phase in range(2):
                for kb in range(nkb):
                    rows.append((b, qt, phase, kb, nkb))
    return [jnp.asarray(c, jnp.int32) for c in np.asarray(rows, np.int32).T]


def _dsa_attention(qi, wi, ki, q, k, vx, batch, seq, tq=256, tk=1024):
    t = q.shape[0]
    qt_per_b = seq // tq
    kb_per_b = seq // tk
    sched = _dsa_schedule(batch, seq, tq, tk)
    k_sel = min(TOPK_MAX, seq // 4)

    def q_row(i, b, qt, ph, kb, nkb):
        return (b[i] * qt_per_b + qt[i], 0)

    def ki_row(i, b, qt, ph, kb, nkb):
        return (b[i] * kb_per_b + jnp.where(ph[i] == 0, kb[i], nkb[i] - 1), 0)

    def kv_row(i, b, qt, ph, kb, nkb):
        return (b[i] * kb_per_b + jnp.where(ph[i] == 0, 0, kb[i]), 0)

    grid_spec = pltpu.PrefetchScalarGridSpec(
        num_scalar_prefetch=5,
        grid=(int(sched[0].shape[0]),),
        in_specs=[pl.BlockSpec((tq, IDX_HEADS * IDX_DIM), q_row),
                  pl.BlockSpec((tq, LANES), q_row),
                  pl.BlockSpec((tk, LANES), ki_row),
                  pl.BlockSpec((tq, HD), q_row),
                  pl.BlockSpec((tk, HD), kv_row),
                  pl.BlockSpec((tk, 2 * HD), kv_row)],
        out_specs=pl.BlockSpec((tq, HD), q_row),
        scratch_shapes=[pltpu.VMEM((kb_per_b, tq, tk), F32)] + _attn_scratch(tq, tk),
    )
    return pl.pallas_call(
        functools.partial(_dsa_kernel, k_sel=k_sel),
        grid_spec=grid_spec,
        out_shape=jax.ShapeDtypeStruct((t, HD), BF16),
        compiler_params=_cparams(1),
        name="dsa_attention",
    )(*sched, qi, wi, ki, q, k, vx)


def _fox_kernel(b_s, qt_s, kb_s, nkb_s, q_ref, k_ref, vx_ref, fk_ref, o_ref,
                q2_scr, m_scr, l_scr, acc_scr, s_scr, p_scr):
    step = pl.program_id(0)
    qt = qt_s[step]
    kb = kb_s[step]
    nkb = nkb_s[step]
    tq = q_ref.shape[0]
    tk = k_ref.shape[0]

    @pl.when(kb == 0)
    def _init():
        _stack_pair_queries(q_ref, q2_scr)
        _attn_init(m_scr, l_scr, acc_scr)

    fk = fk_ref[0]
    diagonal = (kb + 1) * tk > qt * tq

    @pl.when(jnp.logical_not(diagonal))
    def _full():
        _attn_block(q2_scr, k_ref, vx_ref, m_scr, l_scr, acc_scr, s_scr, p_scr,
                    lambda sv, h: sv - fk[h:h + 1, :])

    @pl.when(diagonal)
    def _masked():
        row = lax.broadcasted_iota(jnp.int32, (tq, 1), 0) + qt * tq
        key = lax.broadcasted_iota(jnp.int32, (1, tk), 1) + kb * tk
        causal = jnp.where(key <= row, 0.0, NEG)
        _attn_block(q2_scr, k_ref, vx_ref, m_scr, l_scr, acc_scr, s_scr, p_scr,
                    lambda sv, h: sv + (causal - fk[h:h + 1, :]))

    @pl.when(kb == nkb - 1)
    def _finish():
        _attn_finish(o_ref, l_scr, acc_scr)


def _fox_schedule(batch, seq, tq, tk):
    rows = []
    for b in range(batch):
        for qt in range(seq // tq):
            nkb = -(-((qt + 1) * tq) // tk)
            for kb in range(nkb):
                rows.append((b, qt, kb, nkb))
    return [jnp.asarray(c, jnp.int32) for c in np.asarray(rows, np.int32).T]


def _fox_attention(q, fk_t, k, vx, batch, seq, tq=256, tk=1024):
    t = q.shape[0]
    qt_per_b = seq // tq
    kb_per_b = seq // tk
    sched = _fox_schedule(batch, seq, tq, tk)

    def q_row(i, b, qt, kb, nkb):
        return (b[i] * qt_per_b + qt[i], 0)

    def kv_row(i, b, qt, kb, nkb):
        return (b[i] * kb_per_b + kb[i], 0)

    def fk_row(i, b, qt, kb, nkb):
        return (b[i], 0, kb[i])

    grid_spec = pltpu.PrefetchScalarGridSpec(
        num_scalar_prefetch=4,
        grid=(int(sched[0].shape[0]),),
        in_specs=[pl.BlockSpec((tq, HD), q_row),
                  pl.BlockSpec((tk, HD), kv_row),
                  pl.BlockSpec((tk, 2 * HD), kv_row),
                  pl.BlockSpec((1, N_HEADS, tk), fk_row)],
        out_specs=pl.BlockSpec((tq, HD), q_row),
        scratch_shapes=_attn_scratch(tq, tk),
    )
    return pl.pallas_call(
        _fox_kernel,
        grid_spec=grid_spec,
        out_shape=jax.ShapeDtypeStruct((t, HD), BF16),
        compiler_params=_cparams(1),
        name="fox_attention",
    )(*sched, q, k, vx, fk_t)


def _swiglu_partial(xn, wgu_ref, wd_ref, row_scale):
    tf = wd_ref.shape[-2]
    wgu = wgu_ref[(0,) * (len(wgu_ref.shape) - 2)]
    wd = wd_ref[(0,) * (len(wd_ref.shape) - 2)]
    gu = jnp.dot(xn, wgu, preferred_element_type=F32)
    g = gu[:, :tf]
    u = gu[:, tf:]
    a = g * jax.nn.sigmoid(g) * u
    if row_scale is not None:
        a = a * row_scale
    return jnp.dot(a.astype(BF16), wd, preferred_element_type=F32)


def _ffn_kernel(x_ref, g_ref, wgu_ref, wd_ref, o_ref, xn_scr, acc_scr):
    f = pl.program_id(1)

    @pl.when(f == 0)
    def _():
        xn_scr[...] = _rms(x_ref[...], g_ref[...]).astype(BF16)
        acc_scr[...] = jnp.zeros_like(acc_scr)

    acc_scr[...] += _swiglu_partial(xn_scr[...], wgu_ref, wd_ref, None)

    @pl.when(f == pl.num_programs(1) - 1)
    def _():
        o_ref[...] = x_ref[...] + acc_scr[...]


def _ffn(h, g, wgu, wd, tm=512):
    t, d = h.shape
    n_f, _, two_tf = wgu.shape
    tf = two_tf // 2
    return pl.pallas_call(
        _ffn_kernel,
        grid=(t // tm, n_f),
        in_specs=[pl.BlockSpec((tm, d), lambda i, f: (i, 0)),
                  pl.BlockSpec((1, d), lambda i, f: (0, 0)),
                  pl.BlockSpec((1, d, two_tf), lambda i, f: (f, 0, 0)),
                  pl.BlockSpec((tf, d), lambda i, f: (f, 0))],
        out_specs=pl.BlockSpec((tm, d), lambda i, f: (i, 0)),
        out_shape=jax.ShapeDtypeStruct((t, d), F32),
        scratch_shapes=[pltpu.VMEM((tm, d), BF16), pltpu.VMEM((tm, d), F32)],
        compiler_params=_cparams(2),
        name="ffn_dense",
    )(h, g, wgu, wd)


def _moe_kernel(x_ref, g_ref, wr_ref, wgu_ref, wd_ref, gf_ref, o_ref,
                xn_scr, comb_scr, acc_scr, *, final_norm):
    e = pl.program_id(1)
    f = pl.program_id(2)
    tm = x_ref.shape[0]

    @pl.when(jnp.logical_and(e == 0, f == 0))
    def _route():
        xn = _rms(x_ref[...], g_ref[...])
        xn_scr[...] = xn.astype(BF16)
        acc_scr[...] = jnp.zeros_like(acc_scr)
        logits = jnp.dot(xn, wr_ref[...], preferred_element_type=F32,
                         precision=lax.Precision.HIGHEST)
        lane = lax.broadcasted_iota(jnp.int32, (tm, LANES), 1)
        lg = jnp.where(lane < N_EXPERTS, logits, -jnp.inf)
        m1 = jnp.max(lg, axis=1, keepdims=True)
        i1 = jnp.min(jnp.where(lg == m1, lane, LANES), axis=1, keepdims=True)
        lg2 = jnp.where(lane == i1, -jnp.inf, lg)
        m2 = jnp.max(lg2, axis=1, keepdims=True)
        i2 = jnp.min(jnp.where(lg2 == m2, lane, LANES), axis=1, keepdims=True)
        e2 = jnp.exp(m2 - m1)
        den = 1.0 + e2
        comb_scr[...] = (jnp.where(lane == i1, 1.0 / den, 0.0)
                         + jnp.where(lane == i2, e2 / den, 0.0))

    lane = lax.broadcasted_iota(jnp.int32, (tm, LANES), 1)
    gate = jnp.sum(jnp.where(lane == e, comb_scr[...], 0.0), axis=1, keepdims=True)
    acc_scr[...] += _swiglu_partial(xn_scr[...], wgu_ref, wd_ref, gate)

    @pl.when(jnp.logical_and(e == pl.num_programs(1) - 1, f == pl.num_programs(2) - 1))
    def _():
        y = x_ref[...] + acc_scr[...]
        if final_norm:
            y = _rms(y, gf_ref[...])
        o_ref[...] = y


def _moe(h, g, w_router, wgu, wd, g_final, final_norm, tm=512):
    t, d = h.shape
    n_e, n_f, _, two_tf = wgu.shape
    tf = two_tf // 2
    return pl.pallas_call(
        functools.partial(_moe_kernel, final_norm=final_norm),
        grid=(t // tm, n_e, n_f),
        in_specs=[pl.BlockSpec((tm, d), lambda i, e, f: (i, 0)),
                  pl.BlockSpec((1, d), lambda i, e, f: (0, 0)),
                  pl.BlockSpec((d, LANES), lambda i, e, f: (0, 0)),
                  pl.BlockSpec((1, 1, d, two_tf), lambda i, e, f: (e, f, 0, 0)),
                  pl.BlockSpec((1, tf, d), lambda i, e, f: (e, f, 0)),
                  pl.BlockSpec((1, d), lambda i, e, f: (0, 0))],
        out_specs=pl.BlockSpec((tm, d), lambda i, e, f: (i, 0)),
        out_shape=jax.ShapeDtypeStruct((t, d), F32),
        scratch_shapes=[pltpu.VMEM((tm, d), BF16), pltpu.VMEM((tm, LANES), F32),
                        pltpu.VMEM((tm, d), F32)],
        compiler_params=_cparams(3),
        name="moe",
    )(h, g, w_router, wgu, wd, g_final)


def _split_gate_up(w, n_f):
    d, two_f = w.shape[-2:]
    tf = two_f // 2 // n_f
    g = w[..., :two_f // 2].reshape(*w.shape[:-1], n_f, tf)
    u = w[..., two_f // 2:].reshape(*w.shape[:-1], n_f, tf)
    gu = jnp.concatenate([g, u], axis=-1)
    return jnp.moveaxis(gu, -2, -3).astype(BF16)


def _rope_tables(seq):
    pos = jnp.arange(seq, dtype=F32)
    inv_freq = 1.0 / (ROPE_THETA ** (jnp.arange(0, HEAD_DIM, 2, dtype=F32) / HEAD_DIM))
    ang = pos[:, None] * inv_freq[None, :]
    cos, sin = jnp.cos(ang), jnp.sin(ang)
    cos_t = jnp.tile(cos, (1, LANES // (HEAD_DIM // 2)))
    sin_t = jnp.tile(jnp.concatenate([-sin, sin], axis=1), (1, LANES // HEAD_DIM))
    return cos_t, sin_t


def kernel(x, a_norm, a_w_in, a_w_out, kv_norm, w_kv, b_f, b_norm, b_w_q, b_w_out, ffn_norm,
           dense_w_gate_up, dense_w_down, moe_router, moe_w_gate_up, moe_w_down, final_norm):
    batch, seq, d = x.shape
    depth = ffn_norm.shape[0]
    n_a = a_norm.shape[0]
    n_f = 2
    cos_t, sin_t = _rope_tables(seq)
    idx_w = IDX_HEADS * IDX_DIM

    h = x.reshape(batch * seq, d)
    k_sh = vx_sh = fk_t = None
    for i in range(depth):
        if i < n_a:
            w = a_w_in[i]
            ki_w = w[:, 3 * HD + idx_w:3 * HD + idx_w + IDX_DIM]
            wi_w = w[:, 3 * HD + idx_w + IDX_DIM:]
            w_p = jnp.concatenate(
                [w[:, :3 * HD + idx_w], ki_w, ki_w, wi_w,
                 jnp.zeros((d, LANES - IDX_HEADS), w.dtype)], axis=1).astype(BF16)
            q, k, vx, qi, ki, wi = _a_proj(h, a_norm[i][None, :], w_p, cos_t, sin_t, seq)
            o = _dsa_attention(qi, wi, ki, q, k, vx, batch, seq)
            h = _out_proj(o, a_w_out[i].astype(BF16), h)
        else:
            j = i - n_a
            if k_sh is None:
                w_p = jnp.concatenate(
                    [w_kv, jnp.zeros((d, LANES - N_HEADS), w_kv.dtype)], axis=1).astype(BF16)
                b_p = jnp.concatenate([b_f, jnp.zeros((LANES - N_HEADS,), b_f.dtype)])[None, :]
                k_sh, vx_sh, f_sh = _kv_proj(h, kv_norm[None, :], w_p, b_p, seq)
                fk_t = f_sh[:, :N_HEADS].reshape(batch, seq, N_HEADS).transpose(0, 2, 1)
            q = _q_proj(h, b_norm[j][None, :], b_w_q[j].astype(BF16))
            o = _fox_attention(q, fk_t, k_sh, vx_sh, batch, seq)
            h = _out_proj(o, b_w_out[j].astype(BF16), h)
        if i % 2 == 0:
            wd = dense_w_down[i // 2].astype(BF16)
            h = _ffn(h, ffn_norm[i][None, :], _split_gate_up(dense_w_gate_up[i // 2], n_f), wd)
        else:
            e = i // 2
            wr = jnp.concatenate(
                [moe_router[e], jnp.zeros((d, LANES - N_EXPERTS), moe_router.dtype)], axis=1)
            wd = moe_w_down[e].astype(BF16).reshape(N_EXPERTS, -1, d)
            last = i == depth - 1
            h = _moe(h, ffn_norm[i][None, :], wr, _split_gate_up(moe_w_gate_up[e], n_f), wd,
                     final_norm[None, :], last)
    if depth % 2 == 1:
        raise NotImplementedError("final norm is fused into the last expert mixer")
    return h.reshape(batch, seq, d)
```

```python
import functools
import math

import numpy as np
import jax
import jax.numpy as jnp
from jax import lax
from jax.experimental import pallas as pl
from jax.experimental.pallas import tpu as pltpu

N_HEADS = 16
HEAD_DIM = 64
IDX_HEADS = 8
IDX_DIM = 64
TOPK_MAX = 256
CHUNK = 64
ROPE_THETA = 10000.0
N_EXPERTS = 8
RMS_EPS = 1e-6

LANES = 128
HD = N_HEADS * HEAD_DIM
N_PAIRS = N_HEADS // 2
NEG = -1e30
LOG2E = math.log2(math.e)
VMEM_LIMIT = 52 * 1024 * 1024

F32 = jnp.float32
BF16 = jnp.bfloat16


def _cparams(n_axes):
    return pltpu.CompilerParams(dimension_semantics=("arbitrary",) * n_axes,
                                vmem_limit_bytes=VMEM_LIMIT)


def _rms(x, g):
    return x * lax.rsqrt(jnp.mean(x * x, axis=-1, keepdims=True) + RMS_EPS) * g


def _half_mask(shape):
    return lax.broadcasted_iota(jnp.int32, shape, len(shape) - 1) < HEAD_DIM


def _nt_dot(a, b):
    return lax.dot_general(a, b, (((1,), (1,)), ((), ())), preferred_element_type=F32)


def _store_values_with_ones(vx_ref, p, c0):
    ones = jnp.ones((p.shape[0], LANES), BF16)
    for j in range(p.shape[1] // LANES):
        pair = (c0 // LANES) + j
        vx_ref[:, 2 * pair * LANES:(2 * pair + 1) * LANES] = p[:, j * LANES:(j + 1) * LANES].astype(BF16)
        vx_ref[:, (2 * pair + 1) * LANES:(2 * pair + 2) * LANES] = ones


def _rope_chunk(p, cos, sin_signed):
    fwd = pltpu.roll(p, LANES - 32, 1)
    bwd = pltpu.roll(p, 32, 1)
    lane = lax.broadcasted_iota(jnp.int32, p.shape, 1)
    partner = jnp.where((lane % HEAD_DIM) < HEAD_DIM // 2, fwd, bwd)
    return p * cos + partner * sin_signed


def _a_proj_kernel(x_ref, g_ref, w_ref, cos_ref, sin_ref,
                   q_ref, k_ref, vx_ref, qi_ref, ki_ref, wi_ref):
    xn = _rms(x_ref[...], g_ref[...]).astype(BF16)
    cos = cos_ref[...]
    sin = sin_ref[...]
    seg = 512

    def proj(c0, width):
        return jnp.dot(xn, w_ref[:, c0:c0 + width], preferred_element_type=F32)

    def rope_store(p, out_ref, o0, scale):
        for j in range(p.shape[1] // LANES):
            r = _rope_chunk(p[:, j * LANES:(j + 1) * LANES], cos, sin)
            if scale != 1.0:
                r = r * scale
            out_ref[:, o0 + j * LANES:o0 + (j + 1) * LANES] = r.astype(out_ref.dtype)

    for s in range(HD // seg):
        rope_store(proj(s * seg, seg), q_ref, s * seg, HEAD_DIM ** -0.5 * LOG2E)
    for s in range(HD // seg):
        rope_store(proj(HD + s * seg, seg), k_ref, s * seg, 1.0)
    for s in range(HD // seg):
        _store_values_with_ones(vx_ref, proj(2 * HD + s * seg, seg), s * seg)
    rope_store(proj(3 * HD, IDX_HEADS * IDX_DIM), qi_ref, 0, 1.0)
    tail = proj(3 * HD + IDX_HEADS * IDX_DIM, 2 * LANES)
    rope_store(tail[:, :LANES], ki_ref, 0, 1.0)
    wi_ref[...] = tail[:, LANES:] * (IDX_HEADS ** -0.5)


def _a_proj(h, g, w, cos_t, sin_t, seq, tm=512):
    t, d = h.shape
    n_w = w.shape[1]
    tiles_per_seq = seq // tm
    row = lambda i: (i, 0)
    tab = lambda i: (i % tiles_per_seq, 0)
    const = lambda i: (0, 0)
    return pl.pallas_call(
        _a_proj_kernel,
        grid=(t // tm,),
        in_specs=[pl.BlockSpec((tm, d), row),
                  pl.BlockSpec((1, d), const),
                  pl.BlockSpec((d, n_w), const),
                  pl.BlockSpec((tm, LANES), tab),
                  pl.BlockSpec((tm, LANES), tab)],
        out_specs=[pl.BlockSpec((tm, HD), row),
                   pl.BlockSpec((tm, HD), row),
                   pl.BlockSpec((tm, 2 * HD), row),
                   pl.BlockSpec((tm, IDX_HEADS * IDX_DIM), row),
                   pl.BlockSpec((tm, LANES), row),
                   pl.BlockSpec((tm, LANES), row)],
        out_shape=[jax.ShapeDtypeStruct((t, HD), BF16),
                   jax.ShapeDtypeStruct((t, HD), BF16),
                   jax.ShapeDtypeStruct((t, 2 * HD), BF16),
                   jax.ShapeDtypeStruct((t, IDX_HEADS * IDX_DIM), BF16),
                   jax.ShapeDtypeStruct((t, LANES), BF16),
                   jax.ShapeDtypeStruct((t, LANES), F32)],
        compiler_params=_cparams(1),
        name="a_proj",
    )(h, g, w, cos_t, sin_t)


def _kv_proj_kernel(x_ref, g_ref, w_ref, bf_ref, k_ref, vx_ref, cum_ref, carry_ref, *, tiles_per_seq):
    i = pl.program_id(0)
    tm = x_ref.shape[0]

    @pl.when(i % tiles_per_seq == 0)
    def _():
        carry_ref[...] = jnp.zeros_like(carry_ref)

    xn = _rms(x_ref[...], g_ref[...]).astype(BF16)
    seg = 512
    for s in range(HD // seg):
        k_ref[:, s * seg:(s + 1) * seg] = jnp.dot(
            xn, w_ref[:, s * seg:(s + 1) * seg], preferred_element_type=F32).astype(BF16)
    for s in range(HD // seg):
        _store_values_with_ones(
            vx_ref, jnp.dot(xn, w_ref[:, HD + s * seg:HD + (s + 1) * seg], preferred_element_type=F32),
            s * seg)
    f_logit = jnp.dot(xn, w_ref[:, 2 * HD:2 * HD + LANES], preferred_element_type=F32)
    log_f = jax.nn.log_sigmoid(f_logit + bf_ref[...])
    r_i = lax.broadcasted_iota(jnp.int32, (tm, tm), 0)
    c_i = lax.broadcasted_iota(jnp.int32, (tm, tm), 1)
    tri = jnp.where(c_i <= r_i, 1.0, 0.0).astype(BF16)
    x1 = log_f.astype(BF16)
    rem = log_f - x1.astype(F32)
    x2 = rem.astype(BF16)
    x3 = (rem - x2.astype(F32)).astype(BF16)
    cum = (jnp.dot(tri, x1, preferred_element_type=F32)
           + jnp.dot(tri, x2, preferred_element_type=F32)
           + jnp.dot(tri, x3, preferred_element_type=F32)) + carry_ref[...]
    cum_ref[...] = cum * LOG2E
    carry_ref[...] = cum[tm - 1:tm, :]


def _kv_proj(h, g, w, b_f, seq, tm=512):
    t, d = h.shape
    row = lambda i: (i, 0)
    const = lambda i: (0, 0)
    return pl.pallas_call(
        functools.partial(_kv_proj_kernel, tiles_per_seq=seq // tm),
        grid=(t // tm,),
        in_specs=[pl.BlockSpec((tm, d), row),
                  pl.BlockSpec((1, d), const),
                  pl.BlockSpec((d, w.shape[1]), const),
                  pl.BlockSpec((1, LANES), const)],
        out_specs=[pl.BlockSpec((tm, HD), row),
                   pl.BlockSpec((tm, 2 * HD), row),
                   pl.BlockSpec((tm, LANES), row)],
        out_shape=[jax.ShapeDtypeStruct((t, HD), BF16),
                   jax.ShapeDtypeStruct((t, 2 * HD), BF16),
                   jax.ShapeDtypeStruct((t, LANES), F32)],
        scratch_shapes=[pltpu.VMEM((1, LANES), F32)],
        compiler_params=_cparams(1),
        name="kv_proj",
    )(h, g, w, b_f)


def _q_proj_kernel(x_ref, g_ref, w_ref, q_ref):
    xn = _rms(x_ref[...], g_ref[...]).astype(BF16)
    seg = 512
    for s in range(HD // seg):
        p = jnp.dot(xn, w_ref[:, s * seg:(s + 1) * seg], preferred_element_type=F32)
        q_ref[:, s * seg:(s + 1) * seg] = (p * (HEAD_DIM ** -0.5 * LOG2E)).astype(BF16)


def _q_proj(h, g, w, tm=512):
    t, d = h.shape
    row = lambda i: (i, 0)
    const = lambda i: (0, 0)
    return pl.pallas_call(
        _q_proj_kernel,
        grid=(t // tm,),
        in_specs=[pl.BlockSpec((tm, d), row),
                  pl.BlockSpec((1, d), const),
                  pl.BlockSpec((d, HD), const)],
        out_specs=pl.BlockSpec((tm, HD), row),
        out_shape=jax.ShapeDtypeStruct((t, HD), BF16),
        compiler_params=_cparams(1),
        name="q_proj",
    )(h, g, w)


def _out_proj_kernel(o_ref, w_ref, h_ref, out_ref):
    out_ref[...] = h_ref[...] + jnp.dot(o_ref[...], w_ref[...], preferred_element_type=F32)


def _out_proj(o, w, h, tm=512):
    t, d = h.shape
    row = lambda i: (i, 0)
    const = lambda i: (0, 0)
    return pl.pallas_call(
        _out_proj_kernel,
        grid=(t // tm,),
        in_specs=[pl.BlockSpec((tm, HD), row),
                  pl.BlockSpec((HD, d), const),
                  pl.BlockSpec((tm, d), row)],
        out_specs=pl.BlockSpec((tm, d), row),
        out_shape=jax.ShapeDtypeStruct((t, d), F32),
        compiler_params=_cparams(1),
        name="out_proj",
    )(o, w, h)


def _stack_pair_queries(q_ref, q2_scr):
    tq = q_ref.shape[0]
    for j in range(N_PAIRS):
        qp = q_ref[:, j * LANES:(j + 1) * LANES]
        first = _half_mask(qp.shape)
        zero = jnp.zeros_like(qp)
        q2_scr[j, 0:tq, :] = jnp.where(first, qp, zero)
        q2_scr[j, tq:2 * tq, :] = jnp.where(first, zero, qp)


def _attn_init(m_scr, l_scr, acc_scr):
    m_scr[...] = jnp.full(m_scr.shape, NEG, F32)
    l_scr[...] = jnp.zeros_like(l_scr)
    acc_scr[...] = jnp.zeros_like(acc_scr)


def _attn_block(q2_scr, k_ref, vx_ref, m_scr, l_scr, acc_scr, s_scr, p_scr, bias_fn,
                row_bias_fn=None):
    tq = s_scr.shape[0] // 2
    for j in range(N_PAIRS):
        kp = k_ref[:, j * LANES:(j + 1) * LANES]
        sv = _nt_dot(q2_scr[j], kp)
        alphas = []
        for half in range(2):
            rows = slice(half * tq, (half + 1) * tq)
            s_scr[rows, :] = bias_fn(sv[rows], 2 * j + half)
            m_prev = m_scr[j, rows, :]
            m_blk = jnp.max(s_scr[rows, :], axis=1, keepdims=True)
            if row_bias_fn is not None:
                m_blk = m_blk + row_bias_fn(2 * j + half)
            m_new = jnp.maximum(m_prev, m_blk)
            m_scr[j, rows, :] = m_new
            alphas.append(jnp.exp2(m_prev - m_new))
            shift = m_new if row_bias_fn is None else m_new - row_bias_fn(2 * j + half)
            p_scr[rows, :] = jnp.exp2(s_scr[rows, :] - shift).astype(BF16)
        pv = jnp.dot(p_scr[...], vx_ref[:, 2 * j * LANES:2 * (j + 1) * LANES],
                     preferred_element_type=F32)
        for half in range(2):
            rows = slice(half * tq, (half + 1) * tq)
            acc_scr[j, rows, :] = alphas[half] * acc_scr[j, rows, :] + pv[rows, :LANES]
            l_scr[j, rows, :] = alphas[half] * l_scr[j, rows, :] + pv[rows, LANES:]


def _attn_finish(o_ref, l_scr, acc_scr):
    tq = o_ref.shape[0]
    for j in range(N_PAIRS):
        o2 = acc_scr[j] / l_scr[j]
        o_ref[:, j * LANES:(j + 1) * LANES] = jnp.where(
            _half_mask((tq, LANES)), o2[0:tq], o2[tq:2 * tq]).astype(o_ref.dtype)


def _attn_scratch(tq, tk):
    return [pltpu.VMEM((N_PAIRS, 2 * tq, LANES), BF16),
            pltpu.VMEM((N_PAIRS, 2 * tq, 1), F32),
            pltpu.VMEM((N_PAIRS, 2 * tq, LANES), F32),
            pltpu.VMEM((N_PAIRS, 2 * tq, LANES), F32),
            pltpu.VMEM((2 * tq, tk), F32),
            pltpu.VMEM((2 * tq, tk), BF16)]


I16_MIN = -2 ** 15


def _ordered_key(x):
    bits = lax.bitcast_convert_type(x, jnp.int32)
    return bits ^ ((bits >> 31) & jnp.int32(0x7FFFFFFF))


def _ordered_key_to_f32(key):
    return lax.bitcast_convert_type(key ^ ((key >> 31) & jnp.int32(0x7FFFFFFF)), F32)


def _dsa_kernel(b_s, qt_s, ph_s, kb_s, nkb_s,
                qi_ref, wi_ref, ki_ref, q_ref, k_ref, vx_ref, o_ref,
                score_scr, hi_scr, lo_scr, q2_scr, m_scr, l_scr, acc_scr, s_scr, p_scr, *, k_sel):
    step = pl.program_id(0)
    qt = qt_s[step]
    phase = ph_s[step]
    kb = kb_s[step]
    nkb = nkb_s[step]
    tq = q_ref.shape[0]
    tk = k_ref.shape[0]

    row = lax.broadcasted_iota(jnp.int32, (tq, 1), 0) + qt * tq
    limit = (row // CHUNK + 1) * CHUNK

    @pl.when(phase == 0)
    def _index():
        kk = ki_ref[...]
        w = wi_ref[...]
        acc = jnp.zeros((tq, tk), F32)
        for j in range(IDX_HEADS // 2):
            qp = qi_ref[:, j * LANES:(j + 1) * LANES]
            first = _half_mask(qp.shape)
            zero = jnp.zeros_like(qp)
            for half in range(2):
                h = 2 * j + half
                qh = jnp.where(first, qp, zero) if half == 0 else jnp.where(first, zero, qp)
                sc = _nt_dot(qh, kk)
                acc = acc + jnp.maximum(sc, 0.0) * w[:, h:h + 1]
        key = lax.broadcasted_iota(jnp.int32, (1, tk), 1) + kb * tk
        sc = jnp.where(key < limit, acc, -jnp.inf)
        score_scr[kb] = sc
        okey = _ordered_key(sc)
        hi_scr[kb] = (okey >> 16).astype(jnp.int16)
        lo_scr[kb] = ((okey & 0xFFFF) + I16_MIN).astype(jnp.int16)

    @pl.when(jnp.logical_and(phase == 0, kb == nkb - 1))
    def _select():
        one16 = jnp.ones((tq, LANES), jnp.int16)
        zero16 = jnp.zeros((tq, LANES), jnp.int16)

        def count16(src_scr, cand, strict):
            c16 = jnp.broadcast_to(cand, (tq, LANES)).astype(jnp.int16)

            def body(j, acc):
                for c in range(tk // LANES):
                    blk = src_scr[j, :, c * LANES:(c + 1) * LANES]
                    hit = (blk > c16) if strict else (blk >= c16)
                    acc = acc + jnp.where(hit, one16, zero16)
                return acc

            acc = lax.fori_loop(0, nkb, body, zero16)
            return jnp.sum(acc.astype(jnp.int32), axis=1, keepdims=True)

        def kth_largest16(src_scr, k_need):
            def bit_body(i, r):
                cand_u = r | jnp.left_shift(jnp.int32(1), 15 - i)
                cnt = count16(src_scr, cand_u + I16_MIN, False)
                return jnp.where(cnt >= k_need, cand_u, r)
            return lax.fori_loop(0, 16, bit_body, jnp.zeros((tq, 1), jnp.int32)) + I16_MIN

        hi_k = kth_largest16(hi_scr, k_sel)
        above = count16(hi_scr, hi_k, True)
        hi_k16 = jnp.broadcast_to(hi_k, (tq, LANES)).astype(jnp.int16)
        parked = jnp.full((tq, LANES), I16_MIN, jnp.int16)

        def park_body(j, carry):
            for c in range(tk // LANES):
                cols = slice(c * LANES, (c + 1) * LANES)
                lo_scr[j, :, cols] = jnp.where(hi_scr[j, :, cols] == hi_k16, lo_scr[j, :, cols], parked)
            return carry

        lax.fori_loop(0, nkb, park_body, 0)
        lo_k = kth_largest16(lo_scr, k_sel - above)
        above = above + count16(lo_scr, lo_k, True)
        thr = _ordered_key_to_f32(hi_k * 65536 + (lo_k - I16_MIN))
        few = limit <= k_sel
        thr = jnp.where(few, -jnp.inf, thr)
        need = (k_sel - jnp.where(few, limit, above)).astype(F32)

        sub = 256
        r_i = lax.broadcasted_iota(jnp.int32, (sub, sub), 0)
        c_i = lax.broadcasted_iota(jnp.int32, (sub, sub), 1)
        tri = jnp.where(r_i <= c_i, 1.0, 0.0).astype(BF16)

        def sel_body(j, carry):
            for c in range(tk // sub):
                cols = slice(c * sub, (c + 1) * sub)
                blk = score_scr[j, :, cols]
                key = lax.broadcasted_iota(jnp.int32, (1, sub), 1) + (j * tk + c * sub)
                eq = blk == thr
                rank = carry + jnp.dot(jnp.where(eq, 1.0, 0.0).astype(BF16), tri,
                                       preferred_element_type=F32)
                bias = jnp.where(eq, jnp.where(rank <= need, 0.0, NEG),
                                 jnp.where(blk > thr, 0.0, NEG))
                score_scr[j, :, cols] = jnp.where(key < limit, bias, NEG)
                carry = rank[:, sub - 1:sub]
            return carry

        lax.fori_loop(0, nkb, sel_body, jnp.zeros((tq, 1), F32))

    @pl.when(jnp.logical_and(phase == 1, kb == 0))
    def _init():
        _stack_pair_queries(q_ref, q2_scr)
        _attn_init(m_scr, l_scr, acc_scr)

    @pl.when(phase == 1)
    def _attend():
        _attn_block(q2_scr, k_ref, vx_ref, m_scr, l_scr, acc_scr, s_scr, p_scr,
                    lambda sv, h: sv + score_scr[kb])

    @pl.when(jnp.logical_and(phase == 1, kb == nkb - 1))
    def _finish():
        _attn_finish(o_ref, l_scr, acc_scr)


def _dsa_schedule(batch, seq, tq, tk):
    rows = []
    for b in range(batch):
        for qt in range(seq // tq):
            nkb = -(-((qt + 1) * tq) // tk)
            for phase in range(2):
                for kb in range(nkb):
                    rows.append((b, qt, phase, kb, nkb))
    return [jnp.asarray(c, jnp.int32) for c in np.asarray(rows, np.int32).T]


def _dsa_attention(qi, wi, ki, q, k, vx, batch, seq, tq=256, tk=1024):
    t = q.shape[0]
    qt_per_b = seq // tq
    kb_per_b = seq // tk
    sched = _dsa_schedule(batch, seq, tq, tk)
    k_sel = min(TOPK_MAX, seq // 4)

    def q_row(i, b, qt, ph, kb, nkb):
        return (b[i] * qt_per_b + qt[i], 0)

    def ki_row(i, b, qt, ph, kb, nkb):
        return (b[i] * kb_per_b + jnp.where(ph[i] == 0, kb[i], nkb[i] - 1), 0)

    def kv_row(i, b, qt, ph, kb, nkb):
        return (b[i] * kb_per_b + jnp.where(ph[i] == 0, 0, kb[i]), 0)

    grid_spec = pltpu.PrefetchScalarGridSpec(
        num_scalar_prefetch=5,
        grid=(int(sched[0].shape[0]),),
        in_specs=[pl.BlockSpec((tq, IDX_HEADS * IDX_DIM), q_row),
                  pl.BlockSpec((tq, LANES), q_row),
                  pl.BlockSpec((tk, LANES), ki_row),
                  pl.BlockSpec((tq, HD), q_row),
                  pl.BlockSpec((tk, HD), kv_row),
                  pl.BlockSpec((tk, 2 * HD), kv_row)],
        out_specs=pl.BlockSpec((tq, HD), q_row),
        scratch_shapes=[pltpu.VMEM((kb_per_b, tq, tk), F32),
                        pltpu.VMEM((kb_per_b, tq, tk), jnp.int16),
                        pltpu.VMEM((kb_per_b, tq, tk), jnp.int16)] + _attn_scratch(tq, tk),
    )
    return pl.pallas_call(
        functools.partial(_dsa_kernel, k_sel=k_sel),
        grid_spec=grid_spec,
        out_shape=jax.ShapeDtypeStruct((t, HD), BF16),
        compiler_params=_cparams(1),
        name="dsa_attention",
    )(*sched, qi, wi, ki, q, k, vx)


def _fox_kernel(b_s, qt_s, kb_s, nkb_s, q_ref, fq_ref, k_ref, vx_ref, fk_ref, o_ref,
                q2_scr, m_scr, l_scr, acc_scr, s_scr, p_scr, fq_scr):
    step = pl.program_id(0)
    qt = qt_s[step]
    kb = kb_s[step]
    nkb = nkb_s[step]
    tq = q_ref.shape[0]
    tk = k_ref.shape[0]

    @pl.when(kb == 0)
    def _init():
        _stack_pair_queries(q_ref, q2_scr)
        _attn_init(m_scr, l_scr, acc_scr)
        fq = fq_ref[...]
        for h in range(N_HEADS):
            fq_scr[h] = fq[:, h:h + 1]

    fk = fk_ref[0]
    row_bias = lambda h: fq_scr[h]
    diagonal = (kb + 1) * tk > qt * tq

    @pl.when(jnp.logical_not(diagonal))
    def _full():
        _attn_block(q2_scr, k_ref, vx_ref, m_scr, l_scr, acc_scr, s_scr, p_scr,
                    lambda sv, h: sv - fk[h:h + 1, :], row_bias)

    @pl.when(diagonal)
    def _masked():
        row = lax.broadcasted_iota(jnp.int32, (tq, 1), 0) + qt * tq
        key = lax.broadcasted_iota(jnp.int32, (1, tk), 1) + kb * tk
        causal = jnp.where(key <= row, 0.0, NEG)
        _attn_block(q2_scr, k_ref, vx_ref, m_scr, l_scr, acc_scr, s_scr, p_scr,
                    lambda sv, h: sv + (causal - fk[h:h + 1, :]), row_bias)

    @pl.when(kb == nkb - 1)
    def _finish():
        _attn_finish(o_ref, l_scr, acc_scr)


def _fox_schedule(batch, seq, tq, tk):
    rows = []
    for b in range(batch):
        for qt in range(seq // tq):
            nkb = -(-((qt + 1) * tq) // tk)
            for kb in range(nkb):
                rows.append((b, qt, kb, nkb))
    return [jnp.asarray(c, jnp.int32) for c in np.asarray(rows, np.int32).T]


def _fox_attention(q, fq, fk_t, k, vx, batch, seq, tq=256, tk=1024):
    t = q.shape[0]
    qt_per_b = seq // tq
    kb_per_b = seq // tk
    sched = _fox_schedule(batch, seq, tq, tk)

    def q_row(i, b, qt, kb, nkb):
        return (b[i] * qt_per_b + qt[i], 0)

    def kv_row(i, b, qt, kb, nkb):
        return (b[i] * kb_per_b + kb[i], 0)

    def fk_row(i, b, qt, kb, nkb):
        return (b[i], 0, kb[i])

    grid_spec = pltpu.PrefetchScalarGridSpec(
        num_scalar_prefetch=4,
        grid=(int(sched[0].shape[0]),),
        in_specs=[pl.BlockSpec((tq, HD), q_row),
                  pl.BlockSpec((tq, LANES), q_row),
                  pl.BlockSpec((tk, HD), kv_row),
                  pl.BlockSpec((tk, 2 * HD), kv_row),
                  pl.BlockSpec((1, N_HEADS, tk), fk_row)],
        out_specs=pl.BlockSpec((tq, HD), q_row),
        scratch_shapes=_attn_scratch(tq, tk) + [pltpu.VMEM((N_HEADS, tq, 1), F32)],
    )
    return pl.pallas_call(
        _fox_kernel,
        grid_spec=grid_spec,
        out_shape=jax.ShapeDtypeStruct((t, HD), BF16),
        compiler_params=_cparams(1),
        name="fox_attention",
    )(*sched, q, fq, k, vx, fk_t)


def _swiglu_partial(xn, wgu_ref, wd_ref, row_scale):
    tf = wd_ref.shape[-2]
    wgu = wgu_ref[(0,) * (len(wgu_ref.shape) - 2)]
    wd = wd_ref[(0,) * (len(wd_ref.shape) - 2)]
    gu = jnp.dot(xn, wgu, preferred_element_type=F32)
    g = gu[:, :tf]
    u = gu[:, tf:]
    a = g * jax.nn.sigmoid(g) * u
    if row_scale is not None:
        a = a * row_scale
    return jnp.dot(a.astype(BF16), wd, preferred_element_type=F32)


def _ffn_kernel(x_ref, g_ref, wgu_ref, wd_ref, o_ref, xn_scr, acc_scr):
    f = pl.program_id(1)

    @pl.when(f == 0)
    def _():
        xn_scr[...] = _rms(x_ref[...], g_ref[...]).astype(BF16)
        acc_scr[...] = jnp.zeros_like(acc_scr)

    acc_scr[...] += _swiglu_partial(xn_scr[...], wgu_ref, wd_ref, None)

    @pl.when(f == pl.num_programs(1) - 1)
    def _():
        o_ref[...] = x_ref[...] + acc_scr[...]


def _ffn(h, g, wgu, wd, tm=512):
    t, d = h.shape
    n_f, _, two_tf = wgu.shape
    tf = two_tf // 2
    return pl.pallas_call(
        _ffn_kernel,
        grid=(t // tm, n_f),
        in_specs=[pl.BlockSpec((tm, d), lambda i, f: (i, 0)),
                  pl.BlockSpec((1, d), lambda i, f: (0, 0)),
                  pl.BlockSpec((1, d, two_tf), lambda i, f: (f, 0, 0)),
                  pl.BlockSpec((tf, d), lambda i, f: (f, 0))],
        out_specs=pl.BlockSpec((tm, d), lambda i, f: (i, 0)),
        out_shape=jax.ShapeDtypeStruct((t, d), F32),
        scratch_shapes=[pltpu.VMEM((tm, d), BF16), pltpu.VMEM((tm, d), F32)],
        compiler_params=_cparams(2),
        name="ffn_dense",
    )(h, g, wgu, wd)


def _moe_kernel(x_ref, g_ref, wr_ref, wgu_ref, wd_ref, gf_ref, o_ref,
                xn_scr, comb_scr, acc_scr, *, final_norm):
    e = pl.program_id(1)
    f = pl.program_id(2)
    tm = x_ref.shape[0]

    @pl.when(jnp.logical_and(e == 0, f == 0))
    def _route():
        xn = _rms(x_ref[...], g_ref[...])
        xn_scr[...] = xn.astype(BF16)
        acc_scr[...] = jnp.zeros_like(acc_scr)
        logits = jnp.dot(xn, wr_ref[...], preferred_element_type=F32,
                         precision=lax.Precision.HIGHEST)
        lane = lax.broadcasted_iota(jnp.int32, (tm, LANES), 1)
        lg = jnp.where(lane < N_EXPERTS, logits, -jnp.inf)
        m1 = jnp.max(lg, axis=1, keepdims=True)
        i1 = jnp.min(jnp.where(lg == m1, lane, LANES), axis=1, keepdims=True)
        lg2 = jnp.where(lane == i1, -jnp.inf, lg)
        m2 = jnp.max(lg2, axis=1, keepdims=True)
        i2 = jnp.min(jnp.where(lg2 == m2, lane, LANES), axis=1, keepdims=True)
        e2 = jnp.exp(m2 - m1)
        den = 1.0 + e2
        comb_scr[...] = (jnp.where(lane == i1, 1.0 / den, 0.0)
                         + jnp.where(lane == i2, e2 / den, 0.0))

    lane = lax.broadcasted_iota(jnp.int32, (tm, LANES), 1)
    gate = jnp.sum(jnp.where(lane == e, comb_scr[...], 0.0), axis=1, keepdims=True)
    acc_scr[...] += _swiglu_partial(xn_scr[...], wgu_ref, wd_ref, gate)

    @pl.when(jnp.logical_and(e == pl.num_programs(1) - 1, f == pl.num_programs(2) - 1))
    def _():
        y = x_ref[...] + acc_scr[...]
        if final_norm:
            y = _rms(y, gf_ref[...])
        o_ref[...] = y


def _moe(h, g, w_router, wgu, wd, g_final, final_norm, tm=512):
    t, d = h.shape
    n_e, n_f, _, two_tf = wgu.shape
    tf = two_tf // 2
    return pl.pallas_call(
        functools.partial(_moe_kernel, final_norm=final_norm),
        grid=(t // tm, n_e, n_f),
        in_specs=[pl.BlockSpec((tm, d), lambda i, e, f: (i, 0)),
                  pl.BlockSpec((1, d), lambda i, e, f: (0, 0)),
                  pl.BlockSpec((d, LANES), lambda i, e, f: (0, 0)),
                  pl.BlockSpec((1, 1, d, two_tf), lambda i, e, f: (e, f, 0, 0)),
                  pl.BlockSpec((1, tf, d), lambda i, e, f: (e, f, 0)),
                  pl.BlockSpec((1, d), lambda i, e, f: (0, 0))],
        out_specs=pl.BlockSpec((tm, d), lambda i, e, f: (i, 0)),
        out_shape=jax.ShapeDtypeStruct((t, d), F32),
        scratch_shapes=[pltpu.VMEM((tm, d), BF16), pltpu.VMEM((tm, LANES), F32),
                        pltpu.VMEM((tm, d), F32)],
        compiler_params=_cparams(3),
        name="moe",
    )(h, g, w_router, wgu, wd, g_final)


def _split_gate_up(w, n_f):
    d, two_f = w.shape[-2:]
    tf = two_f // 2 // n_f
    g = w[..., :two_f // 2].reshape(*w.shape[:-1], n_f, tf)
    u = w[..., two_f // 2:].reshape(*w.shape[:-1], n_f, tf)
    gu = jnp.concatenate([g, u], axis=-1)
    return jnp.moveaxis(gu, -2, -3).astype(BF16)


def _rope_tables(seq):
    pos = jnp.arange(seq, dtype=F32)
    inv_freq = 1.0 / (ROPE_THETA ** (jnp.arange(0, HEAD_DIM, 2, dtype=F32) / HEAD_DIM))
    ang = pos[:, None] * inv_freq[None, :]
    cos, sin = jnp.cos(ang), jnp.sin(ang)
    cos_t = jnp.tile(cos, (1, LANES // (HEAD_DIM // 2)))
    sin_t = jnp.tile(jnp.concatenate([-sin, sin], axis=1), (1, LANES // HEAD_DIM))
    return cos_t, sin_t


def kernel(x, a_norm, a_w_in, a_w_out, kv_norm, w_kv, b_f, b_norm, b_w_q, b_w_out, ffn_norm,
           dense_w_gate_up, dense_w_down, moe_router, moe_w_gate_up, moe_w_down, final_norm):
    batch, seq, d = x.shape
    depth = ffn_norm.shape[0]
    n_a = a_norm.shape[0]
    n_f = 2
    cos_t, sin_t = _rope_tables(seq)
    idx_w = IDX_HEADS * IDX_DIM

    h = x.reshape(batch * seq, d)
    k_sh = vx_sh = f_sh = fk_t = None
    for i in range(depth):
        if i < n_a:
            w = a_w_in[i]
            ki_w = w[:, 3 * HD + idx_w:3 * HD + idx_w + IDX_DIM]
            wi_w = w[:, 3 * HD + idx_w + IDX_DIM:]
            w_p = jnp.concatenate(
                [w[:, :3 * HD + idx_w], ki_w, ki_w, wi_w,
                 jnp.zeros((d, LANES - IDX_HEADS), w.dtype)], axis=1).astype(BF16)
            q, k, vx, qi, ki, wi = _a_proj(h, a_norm[i][None, :], w_p, cos_t, sin_t, seq)
            o = _dsa_attention(qi, wi, ki, q, k, vx, batch, seq)
            h = _out_proj(o, a_w_out[i].astype(BF16), h)
        else:
            j = i - n_a
            if k_sh is None:
                w_p = jnp.concatenate(
                    [w_kv, jnp.zeros((d, LANES - N_HEADS), w_kv.dtype)], axis=1).astype(BF16)
                b_p = jnp.concatenate([b_f, jnp.zeros((LANES - N_HEADS,), b_f.dtype)])[None, :]
                k_sh, vx_sh, f_sh = _kv_proj(h, kv_norm[None, :], w_p, b_p, seq)
                fk_t = f_sh[:, :N_HEADS].reshape(batch, seq, N_HEADS).transpose(0, 2, 1)
            q = _q_proj(h, b_norm[j][None, :], b_w_q[j].astype(BF16))
            o = _fox_attention(q, f_sh, fk_t, k_sh, vx_sh, batch, seq)
            h = _out_proj(o, b_w_out[j].astype(BF16), h)
        if i % 2 == 0:
            wd = dense_w_down[i // 2].astype(BF16)
            h = _ffn(h, ffn_norm[i][None, :], _split_gate_up(dense_w_gate_up[i // 2], n_f), wd)
        else:
            e = i // 2
            wr = jnp.concatenate(
                [moe_router[e], jnp.zeros((d, LANES - N_EXPERTS), moe_router.dtype)], axis=1)
            wd = moe_w_down[e].astype(BF16).reshape(N_EXPERTS, -1, d)
            last = i == depth - 1
            h = _moe(h, ffn_norm[i][None, :], wr, _split_gate_up(moe_w_gate_up[e], n_f), wd,
                     final_norm[None, :], last)
    if depth % 2 == 1:
        raise NotImplementedError("final norm is fused into the last expert mixer")
    return h.reshape(batch, seq, d)
```

```python
import functools
import math

import numpy as np
import jax
import jax.numpy as jnp
from jax import lax
from jax.experimental import pallas as pl
from jax.experimental.pallas import tpu as pltpu

N_HEADS = 16
HEAD_DIM = 64
IDX_HEADS = 8
IDX_DIM = 64
TOPK_MAX = 256
CHUNK = 64
ROPE_THETA = 10000.0
N_EXPERTS = 8
RMS_EPS = 1e-6

LANES = 128
HD = N_HEADS * HEAD_DIM
N_PAIRS = N_HEADS // 2
NEG = -1e30
LOG2E = math.log2(math.e)
VMEM_LIMIT = 52 * 1024 * 1024

F32 = jnp.float32
BF16 = jnp.bfloat16


def _cparams(n_axes):
    return pltpu.CompilerParams(dimension_semantics=("arbitrary",) * n_axes,
                                vmem_limit_bytes=VMEM_LIMIT)


def _rms(x, g):
    return x * lax.rsqrt(jnp.mean(x * x, axis=-1, keepdims=True) + RMS_EPS) * g


def _half_mask(shape):
    return lax.broadcasted_iota(jnp.int32, shape, len(shape) - 1) < HEAD_DIM


def _nt_dot(a, b):
    return lax.dot_general(a, b, (((1,), (1,)), ((), ())), preferred_element_type=F32)


def _store_values_with_ones(vx_ref, p, c0):
    ones = jnp.ones((p.shape[0], LANES), BF16)
    for j in range(p.shape[1] // LANES):
        pair = (c0 // LANES) + j
        vx_ref[:, 2 * pair * LANES:(2 * pair + 1) * LANES] = p[:, j * LANES:(j + 1) * LANES].astype(BF16)
        vx_ref[:, (2 * pair + 1) * LANES:(2 * pair + 2) * LANES] = ones


def _rope_chunk(p, cos, sin_signed):
    fwd = pltpu.roll(p, LANES - 32, 1)
    bwd = pltpu.roll(p, 32, 1)
    lane = lax.broadcasted_iota(jnp.int32, p.shape, 1)
    partner = jnp.where((lane % HEAD_DIM) < HEAD_DIM // 2, fwd, bwd)
    return p * cos + partner * sin_signed


def _a_proj_kernel(x_ref, g_ref, w_ref, cos_ref, sin_ref,
                   q_ref, k_ref, vx_ref, qi_ref, ki_ref, wi_ref):
    xn = _rms(x_ref[...], g_ref[...]).astype(BF16)
    cos = cos_ref[...]
    sin = sin_ref[...]
    seg = 512

    def proj(c0, width):
        return jnp.dot(xn, w_ref[:, c0:c0 + width], preferred_element_type=F32)

    def rope_store(p, out_ref, o0, scale):
        for j in range(p.shape[1] // LANES):
            r = _rope_chunk(p[:, j * LANES:(j + 1) * LANES], cos, sin)
            if scale != 1.0:
                r = r * scale
            out_ref[:, o0 + j * LANES:o0 + (j + 1) * LANES] = r.astype(out_ref.dtype)

    for s in range(HD // seg):
        rope_store(proj(s * seg, seg), q_ref, s * seg, HEAD_DIM ** -0.5 * LOG2E)
    for s in range(HD // seg):
        rope_store(proj(HD + s * seg, seg), k_ref, s * seg, 1.0)
    for s in range(HD // seg):
        _store_values_with_ones(vx_ref, proj(2 * HD + s * seg, seg), s * seg)
    rope_store(proj(3 * HD, IDX_HEADS * IDX_DIM), qi_ref, 0, 1.0)
    tail = proj(3 * HD + IDX_HEADS * IDX_DIM, 2 * LANES)
    rope_store(tail[:, :LANES], ki_ref, 0, 1.0)
    wi_ref[...] = tail[:, LANES:] * (IDX_HEADS ** -0.5)


def _a_proj(h, g, w, cos_t, sin_t, seq, tm=512):
    t, d = h.shape
    n_w = w.shape[1]
    tiles_per_seq = seq // tm
    row = lambda i: (i, 0)
    tab = lambda i: (i % tiles_per_seq, 0)
    const = lambda i: (0, 0)
    return pl.pallas_call(
        _a_proj_kernel,
        grid=(t // tm,),
        in_specs=[pl.BlockSpec((tm, d), row),
                  pl.BlockSpec((1, d), const),
                  pl.BlockSpec((d, n_w), const),
                  pl.BlockSpec((tm, LANES), tab),
                  pl.BlockSpec((tm, LANES), tab)],
        out_specs=[pl.BlockSpec((tm, HD), row),
                   pl.BlockSpec((tm, HD), row),
                   pl.BlockSpec((tm, 2 * HD), row),
                   pl.BlockSpec((tm, IDX_HEADS * IDX_DIM), row),
                   pl.BlockSpec((tm, LANES), row),
                   pl.BlockSpec((tm, LANES), row)],
        out_shape=[jax.ShapeDtypeStruct((t, HD), BF16),
                   jax.ShapeDtypeStruct((t, HD), BF16),
                   jax.ShapeDtypeStruct((t, 2 * HD), BF16),
                   jax.ShapeDtypeStruct((t, IDX_HEADS * IDX_DIM), BF16),
                   jax.ShapeDtypeStruct((t, LANES), BF16),
                   jax.ShapeDtypeStruct((t, LANES), F32)],
        compiler_params=_cparams(1),
        name="a_proj",
    )(h, g, w, cos_t, sin_t)


def _kv_proj_kernel(x_ref, g_ref, w_ref, bf_ref, k_ref, vx_ref, cum_ref, carry_ref, *, tiles_per_seq):
    i = pl.program_id(0)
    tm = x_ref.shape[0]

    @pl.when(i % tiles_per_seq == 0)
    def _():
        carry_ref[...] = jnp.zeros_like(carry_ref)

    xn = _rms(x_ref[...], g_ref[...]).astype(BF16)
    seg = 512
    for s in range(HD // seg):
        k_ref[:, s * seg:(s + 1) * seg] = jnp.dot(
            xn, w_ref[:, s * seg:(s + 1) * seg], preferred_element_type=F32).astype(BF16)
    for s in range(HD // seg):
        _store_values_with_ones(
            vx_ref, jnp.dot(xn, w_ref[:, HD + s * seg:HD + (s + 1) * seg], preferred_element_type=F32),
            s * seg)
    f_logit = jnp.dot(xn, w_ref[:, 2 * HD:2 * HD + LANES], preferred_element_type=F32)
    log_f = jax.nn.log_sigmoid(f_logit + bf_ref[...])
    r_i = lax.broadcasted_iota(jnp.int32, (tm, tm), 0)
    c_i = lax.broadcasted_iota(jnp.int32, (tm, tm), 1)
    tri = jnp.where(c_i <= r_i, 1.0, 0.0).astype(BF16)
    x1 = log_f.astype(BF16)
    rem = log_f - x1.astype(F32)
    x2 = rem.astype(BF16)
    x3 = (rem - x2.astype(F32)).astype(BF16)
    cum = (jnp.dot(tri, x1, preferred_element_type=F32)
           + jnp.dot(tri, x2, preferred_element_type=F32)
           + jnp.dot(tri, x3, preferred_element_type=F32)) + carry_ref[...]
    cum_ref[...] = cum * LOG2E
    carry_ref[...] = cum[tm - 1:tm, :]


def _kv_proj(h, g, w, b_f, seq, tm=512):
    t, d = h.shape
    row = lambda i: (i, 0)
    const = lambda i: (0, 0)
    return pl.pallas_call(
        functools.partial(_kv_proj_kernel, tiles_per_seq=seq // tm),
        grid=(t // tm,),
        in_specs=[pl.BlockSpec((tm, d), row),
                  pl.BlockSpec((1, d), const),
                  pl.BlockSpec((d, w.shape[1]), const),
                  pl.BlockSpec((1, LANES), const)],
        out_specs=[pl.BlockSpec((tm, HD), row),
                   pl.BlockSpec((tm, 2 * HD), row),
                   pl.BlockSpec((tm, LANES), row)],
        out_shape=[jax.ShapeDtypeStruct((t, HD), BF16),
                   jax.ShapeDtypeStruct((t, 2 * HD), BF16),
                   jax.ShapeDtypeStruct((t, LANES), F32)],
        scratch_shapes=[pltpu.VMEM((1, LANES), F32)],
        compiler_params=_cparams(1),
        name="kv_proj",
    )(h, g, w, b_f)


def _q_proj_kernel(x_ref, g_ref, w_ref, q_ref):
    xn = _rms(x_ref[...], g_ref[...]).astype(BF16)
    seg = 512
    for s in range(HD // seg):
        p = jnp.dot(xn, w_ref[:, s * seg:(s + 1) * seg], preferred_element_type=F32)
        q_ref[:, s * seg:(s + 1) * seg] = (p * (HEAD_DIM ** -0.5 * LOG2E)).astype(BF16)


def _q_proj(h, g, w, tm=512):
    t, d = h.shape
    row = lambda i: (i, 0)
    const = lambda i: (0, 0)
    return pl.pallas_call(
        _q_proj_kernel,
        grid=(t // tm,),
        in_specs=[pl.BlockSpec((tm, d), row),
                  pl.BlockSpec((1, d), const),
                  pl.BlockSpec((d, HD), const)],
        out_specs=pl.BlockSpec((tm, HD), row),
        out_shape=jax.ShapeDtypeStruct((t, HD), BF16),
        compiler_params=_cparams(1),
        name="q_proj",
    )(h, g, w)


def _out_proj_kernel(o_ref, w_ref, h_ref, out_ref):
    out_ref[...] = h_ref[...] + jnp.dot(o_ref[...], w_ref[...], preferred_element_type=F32)


def _out_proj(o, w, h, tm=512):
    t, d = h.shape
    row = lambda i: (i, 0)
    const = lambda i: (0, 0)
    return pl.pallas_call(
        _out_proj_kernel,
        grid=(t // tm,),
        in_specs=[pl.BlockSpec((tm, HD), row),
                  pl.BlockSpec((HD, d), const),
                  pl.BlockSpec((tm, d), row)],
        out_specs=pl.BlockSpec((tm, d), row),
        out_shape=jax.ShapeDtypeStruct((t, d), F32),
        compiler_params=_cparams(1),
        name="out_proj",
    )(o, w, h)


def _stack_pair_queries(q_ref, q2_scr):
    tq = q_ref.shape[0]
    for j in range(N_PAIRS):
        qp = q_ref[:, j * LANES:(j + 1) * LANES]
        first = _half_mask(qp.shape)
        zero = jnp.zeros_like(qp)
        q2_scr[j, 0:tq, :] = jnp.where(first, qp, zero)
        q2_scr[j, tq:2 * tq, :] = jnp.where(first, zero, qp)


def _attn_init(m_scr, l_scr, acc_scr):
    m_scr[...] = jnp.full(m_scr.shape, NEG, F32)
    l_scr[...] = jnp.zeros_like(l_scr)
    acc_scr[...] = jnp.zeros_like(acc_scr)


def _attn_block(q2_scr, k_ref, vx_ref, m_scr, l_scr, acc_scr, s_scr, p_scr, bias_fn,
                row_bias_fn=None):
    tq = s_scr.shape[0] // 2
    for j in range(N_PAIRS):
        kp = k_ref[:, j * LANES:(j + 1) * LANES]
        sv = _nt_dot(q2_scr[j], kp)
        alphas = []
        for half in range(2):
            rows = slice(half * tq, (half + 1) * tq)
            s_scr[rows, :] = bias_fn(sv[rows], 2 * j + half)
            m_prev = m_scr[j, rows, :]
            m_blk = jnp.max(s_scr[rows, :], axis=1, keepdims=True)
            if row_bias_fn is not None:
                m_blk = m_blk + row_bias_fn(2 * j + half)
            m_new = jnp.maximum(m_prev, m_blk)
            m_scr[j, rows, :] = m_new
            alphas.append(jnp.exp2(m_prev - m_new))
            shift = m_new if row_bias_fn is None else m_new - row_bias_fn(2 * j + half)
            p_scr[rows, :] = jnp.exp2(s_scr[rows, :] - shift).astype(BF16)
        pv = jnp.dot(p_scr[...], vx_ref[:, 2 * j * LANES:2 * (j + 1) * LANES],
                     preferred_element_type=F32)
        for half in range(2):
            rows = slice(half * tq, (half + 1) * tq)
            acc_scr[j, rows, :] = alphas[half] * acc_scr[j, rows, :] + pv[rows, :LANES]
            l_scr[j, rows, :] = alphas[half] * l_scr[j, rows, :] + pv[rows, LANES:]


def _attn_finish(o_ref, l_scr, acc_scr):
    tq = o_ref.shape[0]
    for j in range(N_PAIRS):
        o2 = acc_scr[j] / l_scr[j]
        o_ref[:, j * LANES:(j + 1) * LANES] = jnp.where(
            _half_mask((tq, LANES)), o2[0:tq], o2[tq:2 * tq]).astype(o_ref.dtype)


def _attn_scratch(tq, tk):
    return [pltpu.VMEM((N_PAIRS, 2 * tq, LANES), BF16),
            pltpu.VMEM((N_PAIRS, 2 * tq, 1), F32),
            pltpu.VMEM((N_PAIRS, 2 * tq, LANES), F32),
            pltpu.VMEM((N_PAIRS, 2 * tq, LANES), F32),
            pltpu.VMEM((2 * tq, tk), F32),
            pltpu.VMEM((2 * tq, tk), BF16)]


I16_MIN = -2 ** 15


def _ordered_key(x):
    bits = lax.bitcast_convert_type(x, jnp.int32)
    return bits ^ ((bits >> 31) & jnp.int32(0x7FFFFFFF))


def _ordered_key_to_f32(key):
    return lax.bitcast_convert_type(key ^ ((key >> 31) & jnp.int32(0x7FFFFFFF)), F32)


def _dsa_kernel(b_s, qt_s, ph_s, kb_s, nkb_s,
                qi_ref, wi_ref, ki_ref, q_ref, k_ref, vx_ref, o_ref,
                score_scr, hi_scr, lo_scr, q2_scr, m_scr, l_scr, acc_scr, s_scr, p_scr, *, k_sel):
    step = pl.program_id(0)
    qt = qt_s[step]
    phase = ph_s[step]
    kb = kb_s[step]
    nkb = nkb_s[step]
    tq = q_ref.shape[0]
    tk = k_ref.shape[0]

    row = lax.broadcasted_iota(jnp.int32, (tq, 1), 0) + qt * tq
    limit = (row // CHUNK + 1) * CHUNK

    @pl.when(phase == 0)
    def _index():
        kk = ki_ref[...]
        w = wi_ref[...]
        acc = jnp.zeros((tq, tk), F32)
        for j in range(IDX_HEADS // 2):
            qp = qi_ref[:, j * LANES:(j + 1) * LANES]
            first = _half_mask(qp.shape)
            zero = jnp.zeros_like(qp)
            for half in range(2):
                h = 2 * j + half
                qh = jnp.where(first, qp, zero) if half == 0 else jnp.where(first, zero, qp)
                sc = _nt_dot(qh, kk)
                acc = acc + jnp.maximum(sc, 0.0) * w[:, h:h + 1]
        key = lax.broadcasted_iota(jnp.int32, (1, tk), 1) + kb * tk
        sc = jnp.where(key < limit, acc, -jnp.inf)
        score_scr[kb, :, 0:tk] = sc
        okey = _ordered_key(sc)
        hi_scr[kb, :, 0:tk] = (okey >> 16).astype(jnp.int16)
        lo_scr[kb, :, 0:tk] = ((okey & 0xFFFF) + I16_MIN).astype(jnp.int16)

    @pl.when(jnp.logical_and(phase == 0, kb == nkb - 1))
    def _select():
        one16 = jnp.ones((tq, LANES), jnp.int16)
        zero16 = jnp.zeros((tq, LANES), jnp.int16)

        def count16(src_scr, cand, strict):
            c16 = jnp.broadcast_to(cand, (tq, LANES)).astype(jnp.int16)

            def body(j, acc):
                blk = src_scr[j, :, 0:tk]
                for c in range(tk // LANES):
                    part = blk[:, c * LANES:(c + 1) * LANES]
                    hit = (part > c16) if strict else (part >= c16)
                    acc = acc + jnp.where(hit, one16, zero16)
                return acc

            acc = lax.fori_loop(0, nkb, body, zero16)
            return jnp.sum(acc.astype(jnp.int32), axis=1, keepdims=True)

        def kth_largest16(src_scr, k_need):
            def bit_body(i, r):
                cand_u = r | jnp.left_shift(jnp.int32(1), 15 - i)
                cnt = count16(src_scr, cand_u + I16_MIN, False)
                return jnp.where(cnt >= k_need, cand_u, r)
            return lax.fori_loop(0, 16, bit_body, jnp.zeros((tq, 1), jnp.int32)) + I16_MIN

        hi_k = kth_largest16(hi_scr, k_sel)
        above = count16(hi_scr, hi_k, True)
        hi_k16 = jnp.broadcast_to(hi_k, (tq, LANES)).astype(jnp.int16)
        parked = jnp.full((tq, LANES), I16_MIN, jnp.int16)

        def park_body(j, carry):
            hi = hi_scr[j, :, 0:tk]
            lo = lo_scr[j, :, 0:tk]
            lo_scr[j, :, 0:tk] = jnp.concatenate(
                [jnp.where(hi[:, c * LANES:(c + 1) * LANES] == hi_k16,
                           lo[:, c * LANES:(c + 1) * LANES], parked)
                 for c in range(tk // LANES)], axis=1)
            return carry

        lax.fori_loop(0, nkb, park_body, 0)
        lo_k = kth_largest16(lo_scr, k_sel - above)
        above = above + count16(lo_scr, lo_k, True)
        thr = _ordered_key_to_f32(hi_k * 65536 + (lo_k - I16_MIN))
        few = limit <= k_sel
        thr = jnp.where(few, -jnp.inf, thr)
        need = (k_sel - jnp.where(few, limit, above)).astype(F32)

        sub = 256
        r_i = lax.broadcasted_iota(jnp.int32, (sub, sub), 0)
        c_i = lax.broadcasted_iota(jnp.int32, (sub, sub), 1)
        tri = jnp.where(r_i <= c_i, 1.0, 0.0).astype(BF16)

        def sel_body(j, carry):
            for c in range(tk // sub):
                cols = slice(c * sub, (c + 1) * sub)
                blk = score_scr[j, :, cols]
                key = lax.broadcasted_iota(jnp.int32, (1, sub), 1) + (j * tk + c * sub)
                eq = blk == thr
                rank = carry + jnp.dot(jnp.where(eq, 1.0, 0.0).astype(BF16), tri,
                                       preferred_element_type=F32)
                bias = jnp.where(eq, jnp.where(rank <= need, 0.0, NEG),
                                 jnp.where(blk > thr, 0.0, NEG))
                score_scr[j, :, cols] = jnp.where(key < limit, bias, NEG)
                carry = rank[:, sub - 1:sub]
            return carry

        lax.fori_loop(0, nkb, sel_body, jnp.zeros((tq, 1), F32))

    @pl.when(jnp.logical_and(phase == 1, kb == 0))
    def _init():
        _stack_pair_queries(q_ref, q2_scr)
        _attn_init(m_scr, l_scr, acc_scr)

    @pl.when(phase == 1)
    def _attend():
        _attn_block(q2_scr, k_ref, vx_ref, m_scr, l_scr, acc_scr, s_scr, p_scr,
                    lambda sv, h: sv + score_scr[kb, :, 0:tk])

    @pl.when(jnp.logical_and(phase == 1, kb == nkb - 1))
    def _finish():
        _attn_finish(o_ref, l_scr, acc_scr)


def _dsa_schedule(batch, seq, tq, tk):
    rows = []
    for b in range(batch):
        for qt in range(seq // tq):
            nkb = -(-((qt + 1) * tq) // tk)
            for phase in range(2):
                for kb in range(nkb):
                    rows.append((b, qt, phase, kb, nkb))
    return [jnp.asarray(c, jnp.int32) for c in np.asarray(rows, np.int32).T]


def _dsa_attention(qi, wi, ki, q, k, vx, batch, seq, tq=256, tk=1024):
    t = q.shape[0]
    qt_per_b = seq // tq
    kb_per_b = seq // tk
    sched = _dsa_schedule(batch, seq, tq, tk)
    k_sel = min(TOPK_MAX, seq // 4)

    def q_row(i, b, qt, ph, kb, nkb):
        return (b[i] * qt_per_b + qt[i], 0)

    def ki_row(i, b, qt, ph, kb, nkb):
        return (b[i] * kb_per_b + jnp.where(ph[i] == 0, kb[i], nkb[i] - 1), 0)

    def kv_row(i, b, qt, ph, kb, nkb):
        return (b[i] * kb_per_b + jnp.where(ph[i] == 0, 0, kb[i]), 0)

    grid_spec = pltpu.PrefetchScalarGridSpec(
        num_scalar_prefetch=5,
        grid=(int(sched[0].shape[0]),),
        in_specs=[pl.BlockSpec((tq, IDX_HEADS * IDX_DIM), q_row),
                  pl.BlockSpec((tq, LANES), q_row),
                  pl.BlockSpec((tk, LANES), ki_row),
                  pl.BlockSpec((tq, HD), q_row),
                  pl.BlockSpec((tk, HD), kv_row),
                  pl.BlockSpec((tk, 2 * HD), kv_row)],
        out_specs=pl.BlockSpec((tq, HD), q_row),
        scratch_shapes=[pltpu.VMEM((kb_per_b, tq, tk + LANES), F32),
                        pltpu.VMEM((kb_per_b, tq, tk + LANES), jnp.int16),
                        pltpu.VMEM((kb_per_b, tq, tk + LANES), jnp.int16)] + _attn_scratch(tq, tk),
    )
    return pl.pallas_call(
        functools.partial(_dsa_kernel, k_sel=k_sel),
        grid_spec=grid_spec,
        out_shape=jax.ShapeDtypeStruct((t, HD), BF16),
        compiler_params=_cparams(1),
        name="dsa_attention",
    )(*sched, qi, wi, ki, q, k, vx)


def _fox_kernel(b_s, qt_s, kb_s, nkb_s, q_ref, fq_ref, k_ref, vx_ref, fk_ref, o_ref,
                q2_scr, m_scr, l_scr, acc_scr, s_scr, p_scr, fq_scr):
    step = pl.program_id(0)
    qt = qt_s[step]
    kb = kb_s[step]
    nkb = nkb_s[step]
    tq = q_ref.shape[0]
    tk = k_ref.shape[0]

    @pl.when(kb == 0)
    def _init():
        _stack_pair_queries(q_ref, q2_scr)
        _attn_init(m_scr, l_scr, acc_scr)
        fq = fq_ref[...]
        for h in range(N_HEADS):
            fq_scr[h] = fq[:, h:h + 1]

    fk = fk_ref[0]
    row_bias = lambda h: fq_scr[h]
    diagonal = (kb + 1) * tk > qt * tq

    @pl.when(jnp.logical_not(diagonal))
    def _full():
        _attn_block(q2_scr, k_ref, vx_ref, m_scr, l_scr, acc_scr, s_scr, p_scr,
                    lambda sv, h: sv - fk[h:h + 1, :], row_bias)

    @pl.when(diagonal)
    def _masked():
        row = lax.broadcasted_iota(jnp.int32, (tq, 1), 0) + qt * tq
        key = lax.broadcasted_iota(jnp.int32, (1, tk), 1) + kb * tk
        causal = jnp.where(key <= row, 0.0, NEG)
        _attn_block(q2_scr, k_ref, vx_ref, m_scr, l_scr, acc_scr, s_scr, p_scr,
                    lambda sv, h: sv + (causal - fk[h:h + 1, :]), row_bias)

    @pl.when(kb == nkb - 1)
    def _finish():
        _attn_finish(o_ref, l_scr, acc_scr)


def _fox_schedule(batch, seq, tq, tk):
    rows = []
    for b in range(batch):
        for qt in range(seq // tq):
            nkb = -(-((qt + 1) * tq) // tk)
            for kb in range(nkb):
                rows.append((b, qt, kb, nkb))
    return [jnp.asarray(c, jnp.int32) for c in np.asarray(rows, np.int32).T]


def _fox_attention(q, fq, fk_t, k, vx, batch, seq, tq=256, tk=1024):
    t = q.shape[0]
    qt_per_b = seq // tq
    kb_per_b = seq // tk
    sched = _fox_schedule(batch, seq, tq, tk)

    def q_row(i, b, qt, kb, nkb):
        return (b[i] * qt_per_b + qt[i], 0)

    def kv_row(i, b, qt, kb, nkb):
        return (b[i] * kb_per_b + kb[i], 0)

    def fk_row(i, b, qt, kb, nkb):
        return (b[i], 0, kb[i])

    grid_spec = pltpu.PrefetchScalarGridSpec(
        num_scalar_prefetch=4,
        grid=(int(sched[0].shape[0]),),
        in_specs=[pl.BlockSpec((tq, HD), q_row),
                  pl.BlockSpec((tq, LANES), q_row),
                  pl.BlockSpec((tk, HD), kv_row),
                  pl.BlockSpec((tk, 2 * HD), kv_row),
                  pl.BlockSpec((1, N_HEADS, tk), fk_row)],
        out_specs=pl.BlockSpec((tq, HD), q_row),
        scratch_shapes=_attn_scratch(tq, tk) + [pltpu.VMEM((N_HEADS, tq, 1), F32)],
    )
    return pl.pallas_call(
        _fox_kernel,
        grid_spec=grid_spec,
        out_shape=jax.ShapeDtypeStruct((t, HD), BF16),
        compiler_params=_cparams(1),
        name="fox_attention",
    )(*sched, q, fq, k, vx, fk_t)


def _swiglu_partial(xn, wgu_ref, wd_ref, row_scale):
    tf = wd_ref.shape[-2]
    wgu = wgu_ref[(0,) * (len(wgu_ref.shape) - 2)]
    wd = wd_ref[(0,) * (len(wd_ref.shape) - 2)]
    gu = jnp.dot(xn, wgu, preferred_element_type=F32)
    g = gu[:, :tf]
    u = gu[:, tf:]
    a = g * jax.nn.sigmoid(g) * u
    if row_scale is not None:
        a = a * row_scale
    return jnp.dot(a.astype(BF16), wd, preferred_element_type=F32)


def _ffn_kernel(x_ref, g_ref, wgu_ref, wd_ref, o_ref, xn_scr, acc_scr):
    f = pl.program_id(1)

    @pl.when(f == 0)
    def _():
        xn_scr[...] = _rms(x_ref[...], g_ref[...]).astype(BF16)
        acc_scr[...] = jnp.zeros_like(acc_scr)

    acc_scr[...] += _swiglu_partial(xn_scr[...], wgu_ref, wd_ref, None)

    @pl.when(f == pl.num_programs(1) - 1)
    def _():
        o_ref[...] = x_ref[...] + acc_scr[...]


def _ffn(h, g, wgu, wd, tm=512):
    t, d = h.shape
    n_f, _, two_tf = wgu.shape
    tf = two_tf // 2
    return pl.pallas_call(
        _ffn_kernel,
        grid=(t // tm, n_f),
        in_specs=[pl.BlockSpec((tm, d), lambda i, f: (i, 0)),
                  pl.BlockSpec((1, d), lambda i, f: (0, 0)),
                  pl.BlockSpec((1, d, two_tf), lambda i, f: (f, 0, 0)),
                  pl.BlockSpec((tf, d), lambda i, f: (f, 0))],
        out_specs=pl.BlockSpec((tm, d), lambda i, f: (i, 0)),
        out_shape=jax.ShapeDtypeStruct((t, d), F32),
        scratch_shapes=[pltpu.VMEM((tm, d), BF16), pltpu.VMEM((tm, d), F32)],
        compiler_params=_cparams(2),
        name="ffn_dense",
    )(h, g, wgu, wd)


def _moe_kernel(x_ref, g_ref, wr_ref, wgu_ref, wd_ref, gf_ref, o_ref,
                xn_scr, comb_scr, acc_scr, *, final_norm):
    e = pl.program_id(1)
    f = pl.program_id(2)
    tm = x_ref.shape[0]

    @pl.when(jnp.logical_and(e == 0, f == 0))
    def _route():
        xn = _rms(x_ref[...], g_ref[...])
        xn_scr[...] = xn.astype(BF16)
        acc_scr[...] = jnp.zeros_like(acc_scr)
        logits = jnp.dot(xn, wr_ref[...], preferred_element_type=F32,
                         precision=lax.Precision.HIGHEST)
        lane = lax.broadcasted_iota(jnp.int32, (tm, LANES), 1)
        lg = jnp.where(lane < N_EXPERTS, logits, -jnp.inf)
        m1 = jnp.max(lg, axis=1, keepdims=True)
        i1 = jnp.min(jnp.where(lg == m1, lane, LANES), axis=1, keepdims=True)
        lg2 = jnp.where(lane == i1, -jnp.inf, lg)
        m2 = jnp.max(lg2, axis=1, keepdims=True)
        i2 = jnp.min(jnp.where(lg2 == m2, lane, LANES), axis=1, keepdims=True)
        e2 = jnp.exp(m2 - m1)
        den = 1.0 + e2
        comb_scr[...] = (jnp.where(lane == i1, 1.0 / den, 0.0)
                         + jnp.where(lane == i2, e2 / den, 0.0))

    lane = lax.broadcasted_iota(jnp.int32, (tm, LANES), 1)
    gate = jnp.sum(jnp.where(lane == e, comb_scr[...], 0.0), axis=1, keepdims=True)
    acc_scr[...] += _swiglu_partial(xn_scr[...], wgu_ref, wd_ref, gate)

    @pl.when(jnp.logical_and(e == pl.num_programs(1) - 1, f == pl.num_programs(2) - 1))
    def _():
        y = x_ref[...] + acc_scr[...]
        if final_norm:
            y = _rms(y, gf_ref[...])
        o_ref[...] = y


def _moe(h, g, w_router, wgu, wd, g_final, final_norm, tm=512):
    t, d = h.shape
    n_e, n_f, _, two_tf = wgu.shape
    tf = two_tf // 2
    return pl.pallas_call(
        functools.partial(_moe_kernel, final_norm=final_norm),
        grid=(t // tm, n_e, n_f),
        in_specs=[pl.BlockSpec((tm, d), lambda i, e, f: (i, 0)),
                  pl.BlockSpec((1, d), lambda i, e, f: (0, 0)),
                  pl.BlockSpec((d, LANES), lambda i, e, f: (0, 0)),
                  pl.BlockSpec((1, 1, d, two_tf), lambda i, e, f: (e, f, 0, 0)),
                  pl.BlockSpec((1, tf, d), lambda i, e, f: (e, f, 0)),
                  pl.BlockSpec((1, d), lambda i, e, f: (0, 0))],
        out_specs=pl.BlockSpec((tm, d), lambda i, e, f: (i, 0)),
        out_shape=jax.ShapeDtypeStruct((t, d), F32),
        scratch_shapes=[pltpu.VMEM((tm, d), BF16), pltpu.VMEM((tm, LANES), F32),
                        pltpu.VMEM((tm, d), F32)],
        compiler_params=_cparams(3),
        name="moe",
    )(h, g, w_router, wgu, wd, g_final)


def _split_gate_up(w, n_f):
    d, two_f = w.shape[-2:]
    tf = two_f // 2 // n_f
    g = w[..., :two_f // 2].reshape(*w.shape[:-1], n_f, tf)
    u = w[..., two_f // 2:].reshape(*w.shape[:-1], n_f, tf)
    gu = jnp.concatenate([g, u], axis=-1)
    return jnp.moveaxis(gu, -2, -3).astype(BF16)


def _rope_tables(seq):
    pos = jnp.arange(seq, dtype=F32)
    inv_freq = 1.0 / (ROPE_THETA ** (jnp.arange(0, HEAD_DIM, 2, dtype=F32) / HEAD_DIM))
    ang = pos[:, None] * inv_freq[None, :]
    cos, sin = jnp.cos(ang), jnp.sin(ang)
    cos_t = jnp.tile(cos, (1, LANES // (HEAD_DIM // 2)))
    sin_t = jnp.tile(jnp.concatenate([-sin, sin], axis=1), (1, LANES // HEAD_DIM))
    return cos_t, sin_t


def kernel(x, a_norm, a_w_in, a_w_out, kv_norm, w_kv, b_f, b_norm, b_w_q, b_w_out, ffn_norm,
           dense_w_gate_up, dense_w_down, moe_router, moe_w_gate_up, moe_w_down, final_norm):
    batch, seq, d = x.shape
    depth = ffn_norm.shape[0]
    n_a = a_norm.shape[0]
    n_f = 2
    cos_t, sin_t = _rope_tables(seq)
    idx_w = IDX_HEADS * IDX_DIM

    h = x.reshape(batch * seq, d)
    k_sh = vx_sh = f_sh = fk_t = None
    for i in range(depth):
        if i < n_a:
            w = a_w_in[i]
            ki_w = w[:, 3 * HD + idx_w:3 * HD + idx_w + IDX_DIM]
            wi_w = w[:, 3 * HD + idx_w + IDX_DIM:]
            w_p = jnp.concatenate(
                [w[:, :3 * HD + idx_w], ki_w, ki_w, wi_w,
                 jnp.zeros((d, LANES - IDX_HEADS), w.dtype)], axis=1).astype(BF16)
            q, k, vx, qi, ki, wi = _a_proj(h, a_norm[i][None, :], w_p, cos_t, sin_t, seq)
            o = _dsa_attention(qi, wi, ki, q, k, vx, batch, seq)
            h = _out_proj(o, a_w_out[i].astype(BF16), h)
        else:
            j = i - n_a
            if k_sh is None:
                w_p = jnp.concatenate(
                    [w_kv, jnp.zeros((d, LANES - N_HEADS), w_kv.dtype)], axis=1).astype(BF16)
                b_p = jnp.concatenate([b_f, jnp.zeros((LANES - N_HEADS,), b_f.dtype)])[None, :]
                k_sh, vx_sh, f_sh = _kv_proj(h, kv_norm[None, :], w_p, b_p, seq)
                fk_t = f_sh[:, :N_HEADS].reshape(batch, seq, N_HEADS).transpose(0, 2, 1)
            q = _q_proj(h, b_norm[j][None, :], b_w_q[j].astype(BF16))
            o = _fox_attention(q, f_sh, fk_t, k_sh, vx_sh, batch, seq)
            h = _out_proj(o, b_w_out[j].astype(BF16), h)
        if i % 2 == 0:
            wd = dense_w_down[i // 2].astype(BF16)
            h = _ffn(h, ffn_norm[i][None, :], _split_gate_up(dense_w_gate_up[i // 2], n_f), wd)
        else:
            e = i // 2
            wr = jnp.concatenate(
                [moe_router[e], jnp.zeros((d, LANES - N_EXPERTS), moe_router.dtype)], axis=1)
            wd = moe_w_down[e].astype(BF16).reshape(N_EXPERTS, -1, d)
            last = i == depth - 1
            h = _moe(h, ffn_norm[i][None, :], wr, _split_gate_up(moe_w_gate_up[e], n_f), wd,
                     final_norm[None, :], last)
    if depth % 2 == 1:
        raise NotImplementedError("final norm is fused into the last expert mixer")
    return h.reshape(batch, seq, d)
```

```python
import functools
import math

import numpy as np
import jax
import jax.numpy as jnp
from jax import lax
from jax.experimental import pallas as pl
from jax.experimental.pallas import tpu as pltpu

N_HEADS = 16
HEAD_DIM = 64
IDX_HEADS = 8
IDX_DIM = 64
TOPK_MAX = 256
CHUNK = 64
ROPE_THETA = 10000.0
N_EXPERTS = 8
RMS_EPS = 1e-6

LANES = 128
HD = N_HEADS * HEAD_DIM
N_PAIRS = N_HEADS // 2
NEG = -1e30
LOG2E = math.log2(math.e)
VMEM_LIMIT = 52 * 1024 * 1024

F32 = jnp.float32
BF16 = jnp.bfloat16


def _cparams(n_axes):
    return pltpu.CompilerParams(dimension_semantics=("arbitrary",) * n_axes,
                                vmem_limit_bytes=VMEM_LIMIT)


def _rms(x, g):
    return x * lax.rsqrt(jnp.mean(x * x, axis=-1, keepdims=True) + RMS_EPS) * g


def _half_mask(shape):
    return lax.broadcasted_iota(jnp.int32, shape, len(shape) - 1) < HEAD_DIM


def _nt_dot(a, b):
    return lax.dot_general(a, b, (((1,), (1,)), ((), ())), preferred_element_type=F32)


def _store_values_with_ones(vx_ref, p, c0):
    ones = jnp.ones((p.shape[0], LANES), BF16)
    for j in range(p.shape[1] // LANES):
        pair = (c0 // LANES) + j
        vx_ref[:, 2 * pair * LANES:(2 * pair + 1) * LANES] = p[:, j * LANES:(j + 1) * LANES].astype(BF16)
        vx_ref[:, (2 * pair + 1) * LANES:(2 * pair + 2) * LANES] = ones


def _rope_chunk(p, cos, sin_signed):
    fwd = pltpu.roll(p, LANES - 32, 1)
    bwd = pltpu.roll(p, 32, 1)
    lane = lax.broadcasted_iota(jnp.int32, p.shape, 1)
    partner = jnp.where((lane % HEAD_DIM) < HEAD_DIM // 2, fwd, bwd)
    return p * cos + partner * sin_signed


def _a_proj_kernel(x_ref, g_ref, w_ref, cos_ref, sin_ref,
                   q_ref, k_ref, vx_ref, qi_ref, ki_ref, wi_ref):
    xn = _rms(x_ref[...], g_ref[...]).astype(BF16)
    cos = cos_ref[...]
    sin = sin_ref[...]
    seg = 512

    def proj(c0, width):
        return jnp.dot(xn, w_ref[:, c0:c0 + width], preferred_element_type=F32)

    def rope_store(p, out_ref, o0, scale):
        for j in range(p.shape[1] // LANES):
            r = _rope_chunk(p[:, j * LANES:(j + 1) * LANES], cos, sin)
            if scale != 1.0:
                r = r * scale
            out_ref[:, o0 + j * LANES:o0 + (j + 1) * LANES] = r.astype(out_ref.dtype)

    for s in range(HD // seg):
        rope_store(proj(s * seg, seg), q_ref, s * seg, HEAD_DIM ** -0.5 * LOG2E)
    for s in range(HD // seg):
        rope_store(proj(HD + s * seg, seg), k_ref, s * seg, 1.0)
    for s in range(HD // seg):
        _store_values_with_ones(vx_ref, proj(2 * HD + s * seg, seg), s * seg)
    rope_store(proj(3 * HD, IDX_HEADS * IDX_DIM), qi_ref, 0, 1.0)
    tail = proj(3 * HD + IDX_HEADS * IDX_DIM, 2 * LANES)
    rope_store(tail[:, :LANES], ki_ref, 0, 1.0)
    wi_ref[...] = tail[:, LANES:] * (IDX_HEADS ** -0.5)


def _a_proj(h, g, w, cos_t, sin_t, seq, tm=512):
    t, d = h.shape
    n_w = w.shape[1]
    tiles_per_seq = seq // tm
    row = lambda i: (i, 0)
    tab = lambda i: (i % tiles_per_seq, 0)
    const = lambda i: (0, 0)
    return pl.pallas_call(
        _a_proj_kernel,
        grid=(t // tm,),
        in_specs=[pl.BlockSpec((tm, d), row),
                  pl.BlockSpec((1, d), const),
                  pl.BlockSpec((d, n_w), const),
                  pl.BlockSpec((tm, LANES), tab),
                  pl.BlockSpec((tm, LANES), tab)],
        out_specs=[pl.BlockSpec((tm, HD), row),
                   pl.BlockSpec((tm, HD), row),
                   pl.BlockSpec((tm, 2 * HD), row),
                   pl.BlockSpec((tm, IDX_HEADS * IDX_DIM), row),
                   pl.BlockSpec((tm, LANES), row),
                   pl.BlockSpec((tm, LANES), row)],
        out_shape=[jax.ShapeDtypeStruct((t, HD), BF16),
                   jax.ShapeDtypeStruct((t, HD), BF16),
                   jax.ShapeDtypeStruct((t, 2 * HD), BF16),
                   jax.ShapeDtypeStruct((t, IDX_HEADS * IDX_DIM), BF16),
                   jax.ShapeDtypeStruct((t, LANES), BF16),
                   jax.ShapeDtypeStruct((t, LANES), F32)],
        compiler_params=_cparams(1),
        name="a_proj",
    )(h, g, w, cos_t, sin_t)


def _kv_proj_kernel(x_ref, g_ref, w_ref, bf_ref, k_ref, vx_ref, cum_ref, carry_ref, *, tiles_per_seq):
    i = pl.program_id(0)
    tm = x_ref.shape[0]

    @pl.when(i % tiles_per_seq == 0)
    def _():
        carry_ref[...] = jnp.zeros_like(carry_ref)

    xn = _rms(x_ref[...], g_ref[...]).astype(BF16)
    seg = 512
    for s in range(HD // seg):
        k_ref[:, s * seg:(s + 1) * seg] = jnp.dot(
            xn, w_ref[:, s * seg:(s + 1) * seg], preferred_element_type=F32).astype(BF16)
    for s in range(HD // seg):
        _store_values_with_ones(
            vx_ref, jnp.dot(xn, w_ref[:, HD + s * seg:HD + (s + 1) * seg], preferred_element_type=F32),
            s * seg)
    f_logit = jnp.dot(xn, w_ref[:, 2 * HD:2 * HD + LANES], preferred_element_type=F32)
    log_f = jax.nn.log_sigmoid(f_logit + bf_ref[...])
    r_i = lax.broadcasted_iota(jnp.int32, (tm, tm), 0)
    c_i = lax.broadcasted_iota(jnp.int32, (tm, tm), 1)
    tri = jnp.where(c_i <= r_i, 1.0, 0.0).astype(BF16)
    x1 = log_f.astype(BF16)
    rem = log_f - x1.astype(F32)
    x2 = rem.astype(BF16)
    x3 = (rem - x2.astype(F32)).astype(BF16)
    cum = (jnp.dot(tri, x1, preferred_element_type=F32)
           + jnp.dot(tri, x2, preferred_element_type=F32)
           + jnp.dot(tri, x3, preferred_element_type=F32)) + carry_ref[...]
    cum_ref[...] = cum * LOG2E
    carry_ref[...] = cum[tm - 1:tm, :]


def _kv_proj(h, g, w, b_f, seq, tm=512):
    t, d = h.shape
    row = lambda i: (i, 0)
    const = lambda i: (0, 0)
    return pl.pallas_call(
        functools.partial(_kv_proj_kernel, tiles_per_seq=seq // tm),
        grid=(t // tm,),
        in_specs=[pl.BlockSpec((tm, d), row),
                  pl.BlockSpec((1, d), const),
                  pl.BlockSpec((d, w.shape[1]), const),
                  pl.BlockSpec((1, LANES), const)],
        out_specs=[pl.BlockSpec((tm, HD), row),
                   pl.BlockSpec((tm, 2 * HD), row),
                   pl.BlockSpec((tm, LANES), row)],
        out_shape=[jax.ShapeDtypeStruct((t, HD), BF16),
                   jax.ShapeDtypeStruct((t, 2 * HD), BF16),
                   jax.ShapeDtypeStruct((t, LANES), F32)],
        scratch_shapes=[pltpu.VMEM((1, LANES), F32)],
        compiler_params=_cparams(1),
        name="kv_proj",
    )(h, g, w, b_f)


def _q_proj_kernel(x_ref, g_ref, w_ref, q_ref):
    xn = _rms(x_ref[...], g_ref[...]).astype(BF16)
    seg = 512
    for s in range(HD // seg):
        p = jnp.dot(xn, w_ref[:, s * seg:(s + 1) * seg], preferred_element_type=F32)
        q_ref[:, s * seg:(s + 1) * seg] = (p * (HEAD_DIM ** -0.5 * LOG2E)).astype(BF16)


def _q_proj(h, g, w, tm=512):
    t, d = h.shape
    row = lambda i: (i, 0)
    const = lambda i: (0, 0)
    return pl.pallas_call(
        _q_proj_kernel,
        grid=(t // tm,),
        in_specs=[pl.BlockSpec((tm, d), row),
                  pl.BlockSpec((1, d), const),
                  pl.BlockSpec((d, HD), const)],
        out_specs=pl.BlockSpec((tm, HD), row),
        out_shape=jax.ShapeDtypeStruct((t, HD), BF16),
        compiler_params=_cparams(1),
        name="q_proj",
    )(h, g, w)


def _out_proj_kernel(o_ref, w_ref, h_ref, out_ref):
    out_ref[...] = h_ref[...] + jnp.dot(o_ref[...], w_ref[...], preferred_element_type=F32)


def _out_proj(o, w, h, tm=512):
    t, d = h.shape
    row = lambda i: (i, 0)
    const = lambda i: (0, 0)
    return pl.pallas_call(
        _out_proj_kernel,
        grid=(t // tm,),
        in_specs=[pl.BlockSpec((tm, HD), row),
                  pl.BlockSpec((HD, d), const),
                  pl.BlockSpec((tm, d), row)],
        out_specs=pl.BlockSpec((tm, d), row),
        out_shape=jax.ShapeDtypeStruct((t, d), F32),
        compiler_params=_cparams(1),
        name="out_proj",
    )(o, w, h)


def _stack_pair_queries(q_ref, q2_scr):
    tq = q_ref.shape[0]
    for j in range(N_PAIRS):
        qp = q_ref[:, j * LANES:(j + 1) * LANES]
        first = _half_mask(qp.shape)
        zero = jnp.zeros_like(qp)
        q2_scr[j, 0:tq, :] = jnp.where(first, qp, zero)
        q2_scr[j, tq:2 * tq, :] = jnp.where(first, zero, qp)


def _attn_init(m_scr, l_scr, acc_scr):
    m_scr[...] = jnp.full(m_scr.shape, NEG, F32)
    l_scr[...] = jnp.zeros_like(l_scr)
    acc_scr[...] = jnp.zeros_like(acc_scr)


def _attn_block(q2_scr, k_ref, vx_ref, m_scr, l_scr, acc_scr, s_scr, p_scr, bias_fn,
                row_bias_fn=None):
    tq = s_scr.shape[0] // 2
    for j in range(N_PAIRS):
        kp = k_ref[:, j * LANES:(j + 1) * LANES]
        sv = _nt_dot(q2_scr[j], kp)
        alphas = []
        for half in range(2):
            rows = slice(half * tq, (half + 1) * tq)
            s_scr[rows, :] = bias_fn(sv[rows], 2 * j + half)
            m_prev = m_scr[j, rows, :]
            m_blk = jnp.max(s_scr[rows, :], axis=1, keepdims=True)
            if row_bias_fn is not None:
                m_blk = m_blk + row_bias_fn(2 * j + half)
            m_new = jnp.maximum(m_prev, m_blk)
            m_scr[j, rows, :] = m_new
            alphas.append(jnp.exp2(m_prev - m_new))
            shift = m_new if row_bias_fn is None else m_new - row_bias_fn(2 * j + half)
            p_scr[rows, :] = jnp.exp2(s_scr[rows, :] - shift).astype(BF16)
        pv = jnp.dot(p_scr[...], vx_ref[:, 2 * j * LANES:2 * (j + 1) * LANES],
                     preferred_element_type=F32)
        for half in range(2):
            rows = slice(half * tq, (half + 1) * tq)
            acc_scr[j, rows, :] = alphas[half] * acc_scr[j, rows, :] + pv[rows, :LANES]
            l_scr[j, rows, :] = alphas[half] * l_scr[j, rows, :] + pv[rows, LANES:]


def _attn_finish(o_ref, l_scr, acc_scr):
    tq = o_ref.shape[0]
    for j in range(N_PAIRS):
        o2 = acc_scr[j] / l_scr[j]
        o_ref[:, j * LANES:(j + 1) * LANES] = jnp.where(
            _half_mask((tq, LANES)), o2[0:tq], o2[tq:2 * tq]).astype(o_ref.dtype)


def _attn_scratch(tq, tk):
    return [pltpu.VMEM((N_PAIRS, 2 * tq, LANES), BF16),
            pltpu.VMEM((N_PAIRS, 2 * tq, 1), F32),
            pltpu.VMEM((N_PAIRS, 2 * tq, LANES), F32),
            pltpu.VMEM((N_PAIRS, 2 * tq, LANES), F32),
            pltpu.VMEM((2 * tq, tk), F32),
            pltpu.VMEM((2 * tq, tk), BF16)]


def _ordered_key_to_f32(key):
    return lax.bitcast_convert_type(key ^ ((key >> 31) & jnp.int32(0x7FFFFFFF)), F32)


def _dsa_kernel(b_s, qt_s, ph_s, kb_s, nkb_s,
                qi_ref, wi_ref, ki_ref, q_ref, k_ref, vx_ref, o_ref,
                score_scr, q2_scr, m_scr, l_scr, acc_scr, s_scr, p_scr, *, k_sel):
    step = pl.program_id(0)
    qt = qt_s[step]
    phase = ph_s[step]
    kb = kb_s[step]
    nkb = nkb_s[step]
    tq = q_ref.shape[0]
    tk = k_ref.shape[0]

    row = lax.broadcasted_iota(jnp.int32, (tq, 1), 0) + qt * tq
    limit = (row // CHUNK + 1) * CHUNK

    @pl.when(phase == 0)
    def _index():
        kk = ki_ref[...]
        w = wi_ref[...]
        acc = jnp.zeros((tq, tk), F32)
        for j in range(IDX_HEADS // 2):
            qp = qi_ref[:, j * LANES:(j + 1) * LANES]
            first = _half_mask(qp.shape)
            zero = jnp.zeros_like(qp)
            for half in range(2):
                h = 2 * j + half
                qh = jnp.where(first, qp, zero) if half == 0 else jnp.where(first, zero, qp)
                sc = _nt_dot(qh, kk)
                acc = acc + jnp.maximum(sc, 0.0) * w[:, h:h + 1]
        key = lax.broadcasted_iota(jnp.int32, (1, tk), 1) + kb * tk
        sc = jnp.where(key < limit, acc, -jnp.inf)
        score_scr[kb] = sc

    @pl.when(jnp.logical_and(phase == 0, kb == nkb - 1))
    def _select():
        one = jnp.ones((tq, LANES), F32)
        zero = jnp.zeros((tq, LANES), F32)

        def count(cand, strict):
            c_b = jnp.broadcast_to(cand, (tq, LANES))

            def body(j, acc):
                blk = score_scr[j]
                for c in range(tk // LANES):
                    part = blk[:, c * LANES:(c + 1) * LANES]
                    hit = (part > c_b) if strict else (part >= c_b)
                    acc = acc + jnp.where(hit, one, zero)
                return acc

            acc = lax.fori_loop(0, nkb, body, zero)
            return jnp.sum(acc, axis=1, keepdims=True)

        def bit_body(i, r):
            cand_u = r | jnp.left_shift(jnp.int32(1), 31 - i)
            cnt = count(_ordered_key_to_f32(cand_u ^ jnp.int32(-2 ** 31)), False)
            return jnp.where(cnt >= k_sel, cand_u, r)

        r = lax.fori_loop(0, 32, bit_body, jnp.zeros((tq, 1), jnp.int32))
        few = limit <= k_sel
        thr = jnp.where(few, -jnp.inf, _ordered_key_to_f32(r ^ jnp.int32(-2 ** 31)))
        need = k_sel - count(thr, True)

        sub = 256
        r_i = lax.broadcasted_iota(jnp.int32, (sub, sub), 0)
        c_i = lax.broadcasted_iota(jnp.int32, (sub, sub), 1)
        tri = jnp.where(r_i <= c_i, 1.0, 0.0).astype(BF16)

        def sel_body(j, carry):
            for c in range(tk // sub):
                cols = slice(c * sub, (c + 1) * sub)
                blk = score_scr[j, :, cols]
                key = lax.broadcasted_iota(jnp.int32, (1, sub), 1) + (j * tk + c * sub)
                eq = blk == thr
                rank = carry + jnp.dot(jnp.where(eq, 1.0, 0.0).astype(BF16), tri,
                                       preferred_element_type=F32)
                bias = jnp.where(eq, jnp.where(rank <= need, 0.0, NEG),
                                 jnp.where(blk > thr, 0.0, NEG))
                score_scr[j, :, cols] = jnp.where(key < limit, bias, NEG)
                carry = rank[:, sub - 1:sub]
            return carry

        lax.fori_loop(0, nkb, sel_body, jnp.zeros((tq, 1), F32))

    @pl.when(jnp.logical_and(phase == 1, kb == 0))
    def _init():
        _stack_pair_queries(q_ref, q2_scr)
        _attn_init(m_scr, l_scr, acc_scr)

    @pl.when(phase == 1)
    def _attend():
        _attn_block(q2_scr, k_ref, vx_ref, m_scr, l_scr, acc_scr, s_scr, p_scr,
                    lambda sv, h: sv + score_scr[kb])

    @pl.when(jnp.logical_and(phase == 1, kb == nkb - 1))
    def _finish():
        _attn_finish(o_ref, l_scr, acc_scr)


def _dsa_schedule(batch, seq, tq, tk):
    rows = []
    for b in range(batch):
        for qt in range(seq // tq):
            nkb = -(-((qt + 1) * tq) // tk)
            for phase in range(2):
                for kb in range(nkb):
                    rows.append((b, qt, phase, kb, nkb))
    return [jnp.asarray(c, jnp.int32) for c in np.asarray(rows, np.int32).T]


def _dsa_attention(qi, wi, ki, q, k, vx, batch, seq, tq=256, tk=1024):
    t = q.shape[0]
    qt_per_b = seq // tq
    kb_per_b = seq // tk
    sched = _dsa_schedule(batch, seq, tq, tk)
    k_sel = min(TOPK_MAX, seq // 4)

    def q_row(i, b, qt, ph, kb, nkb):
        return (b[i] * qt_per_b + qt[i], 0)

    def ki_row(i, b, qt, ph, kb, nkb):
        return (b[i] * kb_per_b + jnp.where(ph[i] == 0, kb[i], nkb[i] - 1), 0)

    def kv_row(i, b, qt, ph, kb, nkb):
        return (b[i] * kb_per_b + jnp.where(ph[i] == 0, 0, kb[i]), 0)

    grid_spec = pltpu.PrefetchScalarGridSpec(
        num_scalar_prefetch=5,
        grid=(int(sched[0].shape[0]),),
        in_specs=[pl.BlockSpec((tq, IDX_HEADS * IDX_DIM), q_row),
                  pl.BlockSpec((tq, LANES), q_row),
                  pl.BlockSpec((tk, LANES), ki_row),
                  pl.BlockSpec((tq, HD), q_row),
                  pl.BlockSpec((tk, HD), kv_row),
                  pl.BlockSpec((tk, 2 * HD), kv_row)],
        out_specs=pl.BlockSpec((tq, HD), q_row),
        scratch_shapes=[pltpu.VMEM((kb_per_b, tq, tk), F32)] + _attn_scratch(tq, tk),
    )
    return pl.pallas_call(
        functools.partial(_dsa_kernel, k_sel=k_sel),
        grid_spec=grid_spec,
        out_shape=jax.ShapeDtypeStruct((t, HD), BF16),
        compiler_params=_cparams(1),
        name="dsa_attention",
    )(*sched, qi, wi, ki, q, k, vx)


def _fox_kernel(b_s, qt_s, kb_s, nkb_s, q_ref, fq_ref, k_ref, vx_ref, fk_ref, o_ref,
                q2_scr, m_scr, l_scr, acc_scr, s_scr, p_scr, fq_scr):
    step = pl.program_id(0)
    qt = qt_s[step]
    kb = kb_s[step]
    nkb = nkb_s[step]
    tq = q_ref.shape[0]
    tk = k_ref.shape[0]

    @pl.when(kb == 0)
    def _init():
        _stack_pair_queries(q_ref, q2_scr)
        _attn_init(m_scr, l_scr, acc_scr)
        fq = fq_ref[...]
        for h in range(N_HEADS):
            fq_scr[h] = fq[:, h:h + 1]

    fk = fk_ref[0]
    row_bias = lambda h: fq_scr[h]
    diagonal = (kb + 1) * tk > qt * tq

    @pl.when(jnp.logical_not(diagonal))
    def _full():
        _attn_block(q2_scr, k_ref, vx_ref, m_scr, l_scr, acc_scr, s_scr, p_scr,
                    lambda sv, h: sv - fk[h:h + 1, :], row_bias)

    @pl.when(diagonal)
    def _masked():
        row = lax.broadcasted_iota(jnp.int32, (tq, 1), 0) + qt * tq
        key = lax.broadcasted_iota(jnp.int32, (1, tk), 1) + kb * tk
        causal = jnp.where(key <= row, 0.0, NEG)
        _attn_block(q2_scr, k_ref, vx_ref, m_scr, l_scr, acc_scr, s_scr, p_scr,
                    lambda sv, h: sv + (causal - fk[h:h + 1, :]), row_bias)

    @pl.when(kb == nkb - 1)
    def _finish():
        _attn_finish(o_ref, l_scr, acc_scr)


def _fox_schedule(batch, seq, tq, tk):
    rows = []
    for b in range(batch):
        for qt in range(seq // tq):
            nkb = -(-((qt + 1) * tq) // tk)
            for kb in range(nkb):
                rows.append((b, qt, kb, nkb))
    return [jnp.asarray(c, jnp.int32) for c in np.asarray(rows, np.int32).T]


def _fox_attention(q, fq, fk_t, k, vx, batch, seq, tq=256, tk=1024):
    t = q.shape[0]
    qt_per_b = seq // tq
    kb_per_b = seq // tk
    sched = _fox_schedule(batch, seq, tq, tk)

    def q_row(i, b, qt, kb, nkb):
        return (b[i] * qt_per_b + qt[i], 0)

    def kv_row(i, b, qt, kb, nkb):
        return (b[i] * kb_per_b + kb[i], 0)

    def fk_row(i, b, qt, kb, nkb):
        return (b[i], 0, kb[i])

    grid_spec = pltpu.PrefetchScalarGridSpec(
        num_scalar_prefetch=4,
        grid=(int(sched[0].shape[0]),),
        in_specs=[pl.BlockSpec((tq, HD), q_row),
                  pl.BlockSpec((tq, LANES), q_row),
                  pl.BlockSpec((tk, HD), kv_row),
                  pl.BlockSpec((tk, 2 * HD), kv_row),
                  pl.BlockSpec((1, N_HEADS, tk), fk_row)],
        out_specs=pl.BlockSpec((tq, HD), q_row),
        scratch_shapes=_attn_scratch(tq, tk) + [pltpu.VMEM((N_HEADS, tq, 1), F32)],
    )
    return pl.pallas_call(
        _fox_kernel,
        grid_spec=grid_spec,
        out_shape=jax.ShapeDtypeStruct((t, HD), BF16),
        compiler_params=_cparams(1),
        name="fox_attention",
    )(*sched, q, fq, k, vx, fk_t)


def _swiglu_partial(xn, wgu_ref, wd_ref, row_scale):
    tf = wd_ref.shape[-2]
    wgu = wgu_ref[(0,) * (len(wgu_ref.shape) - 2)]
    wd = wd_ref[(0,) * (len(wd_ref.shape) - 2)]
    gu = jnp.dot(xn, wgu, preferred_element_type=F32)
    g = gu[:, :tf]
    u = gu[:, tf:]
    a = g * jax.nn.sigmoid(g) * u
    if row_scale is not None:
        a = a * row_scale
    return jnp.dot(a.astype(BF16), wd, preferred_element_type=F32)


def _ffn_kernel(x_ref, g_ref, wgu_ref, wd_ref, o_ref, xn_scr, acc_scr):
    f = pl.program_id(1)

    @pl.when(f == 0)
    def _():
        xn_scr[...] = _rms(x_ref[...], g_ref[...]).astype(BF16)
        acc_scr[...] = jnp.zeros_like(acc_scr)

    acc_scr[...] += _swiglu_partial(xn_scr[...], wgu_ref, wd_ref, None)

    @pl.when(f == pl.num_programs(1) - 1)
    def _():
        o_ref[...] = x_ref[...] + acc_scr[...]


def _ffn(h, g, wgu, wd, tm=512):
    t, d = h.shape
    n_f, _, two_tf = wgu.shape
    tf = two_tf // 2
    return pl.pallas_call(
        _ffn_kernel,
        grid=(t // tm, n_f),
        in_specs=[pl.BlockSpec((tm, d), lambda i, f: (i, 0)),
                  pl.BlockSpec((1, d), lambda i, f: (0, 0)),
                  pl.BlockSpec((1, d, two_tf), lambda i, f: (f, 0, 0)),
                  pl.BlockSpec((tf, d), lambda i, f: (f, 0))],
        out_specs=pl.BlockSpec((tm, d), lambda i, f: (i, 0)),
        out_shape=jax.ShapeDtypeStruct((t, d), F32),
        scratch_shapes=[pltpu.VMEM((tm, d), BF16), pltpu.VMEM((tm, d), F32)],
        compiler_params=_cparams(2),
        name="ffn_dense",
    )(h, g, wgu, wd)


META_E1, META_E2, META_POS1, META_POS2, META_G1, META_G2 = range(6)
ROW_TILES = 8


def _to_row_tiles(ref, x):
    for s in range(ROW_TILES):
        ref[:, s, :] = x[:, s * LANES:(s + 1) * LANES]


def _moe_route_kernel(x_ref, g_ref, wr_ref, xn_ref, meta_ref, cnt_ref, carry_ref):
    i = pl.program_id(0)
    tm = x_ref.shape[0]

    @pl.when(i == 0)
    def _():
        carry_ref[...] = jnp.zeros_like(carry_ref)

    xn = _rms(x_ref[...], g_ref[...])
    _to_row_tiles(xn_ref, xn)
    logits = jnp.dot(xn, wr_ref[...], preferred_element_type=F32, precision=lax.Precision.HIGHEST)
    lane = lax.broadcasted_iota(jnp.int32, (tm, LANES), 1)
    lg = jnp.where(lane < N_EXPERTS, logits, -jnp.inf)
    m1 = jnp.max(lg, axis=1, keepdims=True)
    i1 = jnp.min(jnp.where(lg == m1, lane, LANES), axis=1, keepdims=True)
    lg2 = jnp.where(lane == i1, -jnp.inf, lg)
    m2 = jnp.max(lg2, axis=1, keepdims=True)
    i2 = jnp.min(jnp.where(lg2 == m2, lane, LANES), axis=1, keepdims=True)
    e2 = jnp.exp(m2 - m1)
    den = 1.0 + e2
    routed = jnp.where(jnp.logical_or(lane == i1, lane == i2), 1.0, 0.0)
    r_i = lax.broadcasted_iota(jnp.int32, (tm, tm), 0)
    c_i = lax.broadcasted_iota(jnp.int32, (tm, tm), 1)
    tri = jnp.where(c_i < r_i, 1.0, 0.0).astype(BF16)
    before = jnp.dot(tri, routed.astype(BF16), preferred_element_type=F32) + carry_ref[...]
    pos1 = jnp.sum(jnp.where(lane == i1, before, 0.0), axis=1, keepdims=True)
    pos2 = jnp.sum(jnp.where(lane == i2, before, 0.0), axis=1, keepdims=True)
    total = before[tm - 1:tm, :] + routed[tm - 1:tm, :]
    carry_ref[...] = total
    cnt_ref[...] = total
    meta = jnp.zeros((tm, LANES), F32)
    for col, val in ((META_E1, i1.astype(F32)), (META_E2, i2.astype(F32)), (META_POS1, pos1),
                     (META_POS2, pos2), (META_G1, 1.0 / den), (META_G2, e2 / den)):
        meta = jnp.where(lane == col, val, meta)
    meta_ref[...] = meta


def _moe_route(h, g, w_router, tm=512):
    t, d = h.shape
    return pl.pallas_call(
        _moe_route_kernel,
        grid=(t // tm,),
        in_specs=[pl.BlockSpec((tm, d), lambda i: (i, 0)),
                  pl.BlockSpec((1, d), lambda i: (0, 0)),
                  pl.BlockSpec((d, LANES), lambda i: (0, 0))],
        out_specs=[pl.BlockSpec((tm, ROW_TILES, LANES), lambda i: (i, 0, 0)),
                   pl.BlockSpec((tm, LANES), lambda i: (i, 0)),
                   pl.BlockSpec((1, LANES), lambda i: (0, 0))],
        out_shape=[jax.ShapeDtypeStruct((t, ROW_TILES, LANES), F32),
                   jax.ShapeDtypeStruct((t, LANES), F32),
                   jax.ShapeDtypeStruct((1, LANES), F32)],
        scratch_shapes=[pltpu.VMEM((1, LANES), F32)],
        compiler_params=_cparams(1),
        name="moe_route",
    )(h, g, w_router)


def _gather_rows(idx_ref, base, n, src_hbm, dst_ref, sem):
    def start(r, c):
        pltpu.make_async_copy(src_hbm.at[idx_ref[base + r]], dst_ref.at[r], sem).start()
        return c
    lax.fori_loop(0, n, start, 0)

    def wait(r, c):
        pltpu.make_async_copy(src_hbm.at[0], dst_ref.at[r], sem).wait()
        return c
    lax.fori_loop(0, n, wait, 0)


def _moe_expert_kernel(src_s, tile_e_s, n_used_s, xn_hbm, wgu_ref, wd_ref, y_ref,
                       rows_scr, xs_scr, acc_scr, sem):
    i = pl.program_id(0)
    f = pl.program_id(1)
    tm = y_ref.shape[0]
    used = i < n_used_s[0]

    @pl.when(jnp.logical_and(used, f == 0))
    def _gather():
        _gather_rows(src_s, i * tm, tm, xn_hbm, rows_scr, sem)
        for s in range(ROW_TILES):
            xs_scr[:, s * LANES:(s + 1) * LANES] = rows_scr[:, s, :].astype(BF16)
        acc_scr[...] = jnp.zeros_like(acc_scr)

    @pl.when(used)
    def _compute():
        acc_scr[...] += _swiglu_partial(xs_scr[...], wgu_ref, wd_ref, None)

    @pl.when(f == pl.num_programs(1) - 1)
    def _store():
        @pl.when(used)
        def _():
            _to_row_tiles(y_ref, acc_scr[...])

        @pl.when(jnp.logical_not(used))
        def _():
            y_ref[...] = jnp.zeros_like(y_ref)


def _moe_experts(xn3, src, tile_e, n_used, wgu, wd, tm):
    n_rows = src.shape[0]
    n_e, n_f, d, two_tf = wgu.shape
    tf = two_tf // 2
    grid_spec = pltpu.PrefetchScalarGridSpec(
        num_scalar_prefetch=3,
        grid=(n_rows // tm, n_f),
        in_specs=[pl.BlockSpec(memory_space=pl.ANY),
                  pl.BlockSpec((1, 1, d, two_tf), lambda i, f, src, te, nu: (te[i], f, 0, 0)),
                  pl.BlockSpec((1, tf, d), lambda i, f, src, te, nu: (te[i], f, 0))],
        out_specs=pl.BlockSpec((tm, ROW_TILES, LANES), lambda i, f, src, te, nu: (i, 0, 0)),
        scratch_shapes=[pltpu.VMEM((tm, ROW_TILES, LANES), F32),
                        pltpu.VMEM((tm, d), BF16),
                        pltpu.VMEM((tm, d), F32),
                        pltpu.SemaphoreType.DMA(())],
    )
    return pl.pallas_call(
        _moe_expert_kernel,
        grid_spec=grid_spec,
        out_shape=jax.ShapeDtypeStruct((n_rows, ROW_TILES, LANES), F32),
        compiler_params=_cparams(2),
        name="moe_experts",
    )(src, tile_e, n_used, xn3, wgu, wd)


def _moe_combine_kernel(d1_s, d2_s, x_ref, meta_ref, gf_ref, y_hbm, o_ref, y1_scr, y2_scr, sem1, sem2,
                        *, final_norm):
    i = pl.program_id(0)
    tm = x_ref.shape[0]
    _gather_rows(d1_s, i * tm, tm, y_hbm, y1_scr, sem1)
    _gather_rows(d2_s, i * tm, tm, y_hbm, y2_scr, sem2)
    meta = meta_ref[...]
    g1 = meta[:, META_G1:META_G1 + 1]
    g2 = meta[:, META_G2:META_G2 + 1]
    for s in range(ROW_TILES):
        cols = slice(s * LANES, (s + 1) * LANES)
        o_ref[:, cols] = x_ref[:, cols] + g1 * y1_scr[:, s, :] + g2 * y2_scr[:, s, :]
    if final_norm:
        o_ref[...] = _rms(o_ref[...], gf_ref[...])


def _moe_combine(h, meta, g_final, y3, dest1, dest2, final_norm, tm=512):
    t, d = h.shape
    grid_spec = pltpu.PrefetchScalarGridSpec(
        num_scalar_prefetch=2,
        grid=(t // tm,),
        in_specs=[pl.BlockSpec((tm, d), lambda i, d1, d2: (i, 0)),
                  pl.BlockSpec((tm, LANES), lambda i, d1, d2: (i, 0)),
                  pl.BlockSpec((1, d), lambda i, d1, d2: (0, 0)),
                  pl.BlockSpec(memory_space=pl.ANY)],
        out_specs=pl.BlockSpec((tm, d), lambda i, d1, d2: (i, 0)),
        scratch_shapes=[pltpu.VMEM((tm, ROW_TILES, LANES), F32),
                        pltpu.VMEM((tm, ROW_TILES, LANES), F32),
                        pltpu.SemaphoreType.DMA(()),
                        pltpu.SemaphoreType.DMA(())],
    )
    return pl.pallas_call(
        functools.partial(_moe_combine_kernel, final_norm=final_norm),
        grid_spec=grid_spec,
        out_shape=jax.ShapeDtypeStruct((t, d), F32),
        compiler_params=_cparams(1),
        name="moe_combine",
    )(dest1, dest2, h, meta, g_final, y3)


def _moe(h, g, w_router, wgu, wd, g_final, final_norm, tm=512):
    t, d = h.shape
    xn3, meta, cnt = _moe_route(h, g, w_router)
    counts = cnt[0, :N_EXPERTS].astype(jnp.int32)
    padded = (counts + tm - 1) // tm * tm
    ends = jnp.cumsum(padded)
    starts = ends - padded
    e1 = meta[:, META_E1].astype(jnp.int32)
    e2 = meta[:, META_E2].astype(jnp.int32)
    dest1 = starts[e1] + meta[:, META_POS1].astype(jnp.int32)
    dest2 = starts[e2] + meta[:, META_POS2].astype(jnp.int32)
    n_rows = 2 * t + N_EXPERTS * tm
    tok = jnp.arange(t, dtype=jnp.int32)
    src = jnp.zeros((n_rows,), jnp.int32).at[dest1].set(tok).at[dest2].set(tok)
    tile_start = jnp.arange(n_rows // tm, dtype=jnp.int32) * tm
    tile_e = jnp.minimum(jnp.searchsorted(ends, tile_start, side="right"), N_EXPERTS - 1).astype(jnp.int32)
    n_used = (ends[-1:] // tm).astype(jnp.int32)
    y3 = _moe_experts(xn3, src, tile_e, n_used, wgu, wd, tm)
    return _moe_combine(h, meta, g_final, y3, dest1, dest2, final_norm)


def _split_gate_up(w, n_f):
    d, two_f = w.shape[-2:]
    tf = two_f // 2 // n_f
    g = w[..., :two_f // 2].reshape(*w.shape[:-1], n_f, tf)
    u = w[..., two_f // 2:].reshape(*w.shape[:-1], n_f, tf)
    gu = jnp.concatenate([g, u], axis=-1)
    return jnp.moveaxis(gu, -2, -3).astype(BF16)


def _rope_tables(seq):
    pos = jnp.arange(seq, dtype=F32)
    inv_freq = 1.0 / (ROPE_THETA ** (jnp.arange(0, HEAD_DIM, 2, dtype=F32) / HEAD_DIM))
    ang = pos[:, None] * inv_freq[None, :]
    cos, sin = jnp.cos(ang), jnp.sin(ang)
    cos_t = jnp.tile(cos, (1, LANES // (HEAD_DIM // 2)))
    sin_t = jnp.tile(jnp.concatenate([-sin, sin], axis=1), (1, LANES // HEAD_DIM))
    return cos_t, sin_t


def kernel(x, a_norm, a_w_in, a_w_out, kv_norm, w_kv, b_f, b_norm, b_w_q, b_w_out, ffn_norm,
           dense_w_gate_up, dense_w_down, moe_router, moe_w_gate_up, moe_w_down, final_norm):
    batch, seq, d = x.shape
    depth = ffn_norm.shape[0]
    n_a = a_norm.shape[0]
    n_f = 2
    cos_t, sin_t = _rope_tables(seq)
    idx_w = IDX_HEADS * IDX_DIM

    h = x.reshape(batch * seq, d)
    k_sh = vx_sh = f_sh = fk_t = None
    for i in range(depth):
        if i < n_a:
            w = a_w_in[i]
            ki_w = w[:, 3 * HD + idx_w:3 * HD + idx_w + IDX_DIM]
            wi_w = w[:, 3 * HD + idx_w + IDX_DIM:]
            w_p = jnp.concatenate(
                [w[:, :3 * HD + idx_w], ki_w, ki_w, wi_w,
                 jnp.zeros((d, LANES - IDX_HEADS), w.dtype)], axis=1).astype(BF16)
            q, k, vx, qi, ki, wi = _a_proj(h, a_norm[i][None, :], w_p, cos_t, sin_t, seq)
            o = _dsa_attention(qi, wi, ki, q, k, vx, batch, seq)
            h = _out_proj(o, a_w_out[i].astype(BF16), h)
        else:
            j = i - n_a
            if k_sh is None:
                w_p = jnp.concatenate(
                    [w_kv, jnp.zeros((d, LANES - N_HEADS), w_kv.dtype)], axis=1).astype(BF16)
                b_p = jnp.concatenate([b_f, jnp.zeros((LANES - N_HEADS,), b_f.dtype)])[None, :]
                k_sh, vx_sh, f_sh = _kv_proj(h, kv_norm[None, :], w_p, b_p, seq)
                fk_t = f_sh[:, :N_HEADS].reshape(batch, seq, N_HEADS).transpose(0, 2, 1)
            q = _q_proj(h, b_norm[j][None, :], b_w_q[j].astype(BF16))
            o = _fox_attention(q, f_sh, fk_t, k_sh, vx_sh, batch, seq)
            h = _out_proj(o, b_w_out[j].astype(BF16), h)
        if i % 2 == 0:
            wd = dense_w_down[i // 2].astype(BF16)
            h = _ffn(h, ffn_norm[i][None, :], _split_gate_up(dense_w_gate_up[i // 2], n_f), wd)
        else:
            e = i // 2
            wr = jnp.concatenate(
                [moe_router[e], jnp.zeros((d, LANES - N_EXPERTS), moe_router.dtype)], axis=1)
            wd = moe_w_down[e].astype(BF16).reshape(N_EXPERTS, -1, d)
            last = i == depth - 1
            h = _moe(h, ffn_norm[i][None, :], wr, _split_gate_up(moe_w_gate_up[e], n_f), wd,
                     final_norm[None, :], last)
    if depth % 2 == 1:
        raise NotImplementedError("final norm is fused into the last expert mixer")
    return h.reshape(batch, seq, d)
```

```python
import functools
import math

import numpy as np
import jax
import jax.numpy as jnp
from jax import lax
from jax.experimental import pallas as pl
from jax.experimental.pallas import tpu as pltpu

N_HEADS = 16
HEAD_DIM = 64
IDX_HEADS = 8
IDX_DIM = 64
TOPK_MAX = 256
CHUNK = 64
ROPE_THETA = 10000.0
N_EXPERTS = 8
RMS_EPS = 1e-6

LANES = 128
HD = N_HEADS * HEAD_DIM
N_PAIRS = N_HEADS // 2
NEG = -1e30
LOG2E = math.log2(math.e)
VMEM_LIMIT = 52 * 1024 * 1024

F32 = jnp.float32
BF16 = jnp.bfloat16


def _cparams(n_axes):
    return pltpu.CompilerParams(dimension_semantics=("arbitrary",) * n_axes,
                                vmem_limit_bytes=VMEM_LIMIT)


def _rms(x, g):
    return x * lax.rsqrt(jnp.mean(x * x, axis=-1, keepdims=True) + RMS_EPS) * g


def _half_mask(shape):
    return lax.broadcasted_iota(jnp.int32, shape, len(shape) - 1) < HEAD_DIM


def _nt_dot(a, b):
    return lax.dot_general(a, b, (((1,), (1,)), ((), ())), preferred_element_type=F32)


def _store_values_with_ones(vx_ref, p, c0):
    ones = jnp.ones((p.shape[0], LANES), BF16)
    for j in range(p.shape[1] // LANES):
        pair = (c0 // LANES) + j
        vx_ref[:, 2 * pair * LANES:(2 * pair + 1) * LANES] = p[:, j * LANES:(j + 1) * LANES].astype(BF16)
        vx_ref[:, (2 * pair + 1) * LANES:(2 * pair + 2) * LANES] = ones


def _rope_chunk(p, cos, sin_signed):
    fwd = pltpu.roll(p, LANES - 32, 1)
    bwd = pltpu.roll(p, 32, 1)
    lane = lax.broadcasted_iota(jnp.int32, p.shape, 1)
    partner = jnp.where((lane % HEAD_DIM) < HEAD_DIM // 2, fwd, bwd)
    return p * cos + partner * sin_signed


def _a_proj_kernel(x_ref, g_ref, w_ref, cos_ref, sin_ref,
                   q_ref, k_ref, vx_ref, qi_ref, ki_ref, wi_ref):
    xn = _rms(x_ref[...], g_ref[...]).astype(BF16)
    cos = cos_ref[...]
    sin = sin_ref[...]
    seg = 512

    def proj(c0, width):
        return jnp.dot(xn, w_ref[:, c0:c0 + width], preferred_element_type=F32)

    def rope_store(p, out_ref, o0, scale):
        for j in range(p.shape[1] // LANES):
            r = _rope_chunk(p[:, j * LANES:(j + 1) * LANES], cos, sin)
            if scale != 1.0:
                r = r * scale
            out_ref[:, o0 + j * LANES:o0 + (j + 1) * LANES] = r.astype(out_ref.dtype)

    for s in range(HD // seg):
        rope_store(proj(s * seg, seg), q_ref, s * seg, HEAD_DIM ** -0.5 * LOG2E)
    for s in range(HD // seg):
        rope_store(proj(HD + s * seg, seg), k_ref, s * seg, 1.0)
    for s in range(HD // seg):
        _store_values_with_ones(vx_ref, proj(2 * HD + s * seg, seg), s * seg)
    rope_store(proj(3 * HD, IDX_HEADS * IDX_DIM), qi_ref, 0, 1.0)
    tail = proj(3 * HD + IDX_HEADS * IDX_DIM, 2 * LANES)
    rope_store(tail[:, :LANES], ki_ref, 0, 1.0)
    wi_ref[...] = tail[:, LANES:] * (IDX_HEADS ** -0.5)


def _a_proj(h, g, w, cos_t, sin_t, seq, tm=512):
    t, d = h.shape
    n_w = w.shape[1]
    tiles_per_seq = seq // tm
    row = lambda i: (i, 0)
    tab = lambda i: (i % tiles_per_seq, 0)
    const = lambda i: (0, 0)
    return pl.pallas_call(
        _a_proj_kernel,
        grid=(t // tm,),
        in_specs=[pl.BlockSpec((tm, d), row),
                  pl.BlockSpec((1, d), const),
                  pl.BlockSpec((d, n_w), const),
                  pl.BlockSpec((tm, LANES), tab),
                  pl.BlockSpec((tm, LANES), tab)],
        out_specs=[pl.BlockSpec((tm, HD), row),
                   pl.BlockSpec((tm, HD), row),
                   pl.BlockSpec((tm, 2 * HD), row),
                   pl.BlockSpec((tm, IDX_HEADS * IDX_DIM), row),
                   pl.BlockSpec((tm, LANES), row),
                   pl.BlockSpec((tm, LANES), row)],
        out_shape=[jax.ShapeDtypeStruct((t, HD), BF16),
                   jax.ShapeDtypeStruct((t, HD), BF16),
                   jax.ShapeDtypeStruct((t, 2 * HD), BF16),
                   jax.ShapeDtypeStruct((t, IDX_HEADS * IDX_DIM), BF16),
                   jax.ShapeDtypeStruct((t, LANES), BF16),
                   jax.ShapeDtypeStruct((t, LANES), F32)],
        compiler_params=_cparams(1),
        name="a_proj",
    )(h, g, w, cos_t, sin_t)


def _kv_proj_kernel(x_ref, g_ref, w_ref, bf_ref, k_ref, vx_ref, cum_ref, carry_ref, *, tiles_per_seq):
    i = pl.program_id(0)
    tm = x_ref.shape[0]

    @pl.when(i % tiles_per_seq == 0)
    def _():
        carry_ref[...] = jnp.zeros_like(carry_ref)

    xn = _rms(x_ref[...], g_ref[...]).astype(BF16)
    seg = 512
    for s in range(HD // seg):
        k_ref[:, s * seg:(s + 1) * seg] = jnp.dot(
            xn, w_ref[:, s * seg:(s + 1) * seg], preferred_element_type=F32).astype(BF16)
    for s in range(HD // seg):
        _store_values_with_ones(
            vx_ref, jnp.dot(xn, w_ref[:, HD + s * seg:HD + (s + 1) * seg], preferred_element_type=F32),
            s * seg)
    f_logit = jnp.dot(xn, w_ref[:, 2 * HD:2 * HD + LANES], preferred_element_type=F32)
    log_f = jax.nn.log_sigmoid(f_logit + bf_ref[...])
    r_i = lax.broadcasted_iota(jnp.int32, (tm, tm), 0)
    c_i = lax.broadcasted_iota(jnp.int32, (tm, tm), 1)
    tri = jnp.where(c_i <= r_i, 1.0, 0.0).astype(BF16)
    x1 = log_f.astype(BF16)
    rem = log_f - x1.astype(F32)
    x2 = rem.astype(BF16)
    x3 = (rem - x2.astype(F32)).astype(BF16)
    cum = (jnp.dot(tri, x1, preferred_element_type=F32)
           + jnp.dot(tri, x2, preferred_element_type=F32)
           + jnp.dot(tri, x3, preferred_element_type=F32)) + carry_ref[...]
    cum_ref[...] = cum * LOG2E
    carry_ref[...] = cum[tm - 1:tm, :]


def _kv_proj(h, g, w, b_f, seq, tm=512):
    t, d = h.shape
    row = lambda i: (i, 0)
    const = lambda i: (0, 0)
    return pl.pallas_call(
        functools.partial(_kv_proj_kernel, tiles_per_seq=seq // tm),
        grid=(t // tm,),
        in_specs=[pl.BlockSpec((tm, d), row),
                  pl.BlockSpec((1, d), const),
                  pl.BlockSpec((d, w.shape[1]), const),
                  pl.BlockSpec((1, LANES), const)],
        out_specs=[pl.BlockSpec((tm, HD), row),
                   pl.BlockSpec((tm, 2 * HD), row),
                   pl.BlockSpec((tm, LANES), row)],
        out_shape=[jax.ShapeDtypeStruct((t, HD), BF16),
                   jax.ShapeDtypeStruct((t, 2 * HD), BF16),
                   jax.ShapeDtypeStruct((t, LANES), F32)],
        scratch_shapes=[pltpu.VMEM((1, LANES), F32)],
        compiler_params=_cparams(1),
        name="kv_proj",
    )(h, g, w, b_f)


def _q_proj_kernel(x_ref, g_ref, w_ref, q_ref):
    xn = _rms(x_ref[...], g_ref[...]).astype(BF16)
    seg = 512
    for s in range(HD // seg):
        p = jnp.dot(xn, w_ref[:, s * seg:(s + 1) * seg], preferred_element_type=F32)
        q_ref[:, s * seg:(s + 1) * seg] = (p * (HEAD_DIM ** -0.5 * LOG2E)).astype(BF16)


def _q_proj(h, g, w, tm=512):
    t, d = h.shape
    row = lambda i: (i, 0)
    const = lambda i: (0, 0)
    return pl.pallas_call(
        _q_proj_kernel,
        grid=(t // tm,),
        in_specs=[pl.BlockSpec((tm, d), row),
                  pl.BlockSpec((1, d), const),
                  pl.BlockSpec((d, HD), const)],
        out_specs=pl.BlockSpec((tm, HD), row),
        out_shape=jax.ShapeDtypeStruct((t, HD), BF16),
        compiler_params=_cparams(1),
        name="q_proj",
    )(h, g, w)


def _out_proj_kernel(o_ref, w_ref, h_ref, out_ref):
    out_ref[...] = h_ref[...] + jnp.dot(o_ref[...], w_ref[...], preferred_element_type=F32)


def _out_proj(o, w, h, tm=512):
    t, d = h.shape
    row = lambda i: (i, 0)
    const = lambda i: (0, 0)
    return pl.pallas_call(
        _out_proj_kernel,
        grid=(t // tm,),
        in_specs=[pl.BlockSpec((tm, HD), row),
                  pl.BlockSpec((HD, d), const),
                  pl.BlockSpec((tm, d), row)],
        out_specs=pl.BlockSpec((tm, d), row),
        out_shape=jax.ShapeDtypeStruct((t, d), F32),
        compiler_params=_cparams(1),
        name="out_proj",
    )(o, w, h)


def _stack_pair_queries(q_ref, q2_scr):
    tq = q_ref.shape[0]
    for j in range(N_PAIRS):
        qp = q_ref[:, j * LANES:(j + 1) * LANES]
        first = _half_mask(qp.shape)
        zero = jnp.zeros_like(qp)
        q2_scr[j, 0:tq, :] = jnp.where(first, qp, zero)
        q2_scr[j, tq:2 * tq, :] = jnp.where(first, zero, qp)


def _attn_init(m_scr, l_scr, acc_scr):
    m_scr[...] = jnp.full(m_scr.shape, NEG, F32)
    l_scr[...] = jnp.zeros_like(l_scr)
    acc_scr[...] = jnp.zeros_like(acc_scr)


def _attn_block(q2_scr, k_ref, vx_ref, m_scr, l_scr, acc_scr, s_scr, p_scr, bias_fn,
                row_bias_fn=None):
    tq = s_scr.shape[0] // 2
    for j in range(N_PAIRS):
        kp = k_ref[:, j * LANES:(j + 1) * LANES]
        sv = _nt_dot(q2_scr[j], kp)
        alphas = []
        for half in range(2):
            rows = slice(half * tq, (half + 1) * tq)
            s_scr[rows, :] = bias_fn(sv[rows], 2 * j + half)
            m_prev = m_scr[j, rows, :]
            m_blk = jnp.max(s_scr[rows, :], axis=1, keepdims=True)
            if row_bias_fn is not None:
                m_blk = m_blk + row_bias_fn(2 * j + half)
            m_new = jnp.maximum(m_prev, m_blk)
            m_scr[j, rows, :] = m_new
            alphas.append(jnp.exp2(m_prev - m_new))
            shift = m_new if row_bias_fn is None else m_new - row_bias_fn(2 * j + half)
            p_scr[rows, :] = jnp.exp2(s_scr[rows, :] - shift).astype(BF16)
        pv = jnp.dot(p_scr[...], vx_ref[:, 2 * j * LANES:2 * (j + 1) * LANES],
                     preferred_element_type=F32)
        for half in range(2):
            rows = slice(half * tq, (half + 1) * tq)
            acc_scr[j, rows, :] = alphas[half] * acc_scr[j, rows, :] + pv[rows, :LANES]
            l_scr[j, rows, :] = alphas[half] * l_scr[j, rows, :] + pv[rows, LANES:]


def _attn_finish(o_ref, l_scr, acc_scr):
    tq = o_ref.shape[0]
    for j in range(N_PAIRS):
        o2 = acc_scr[j] / l_scr[j]
        o_ref[:, j * LANES:(j + 1) * LANES] = jnp.where(
            _half_mask((tq, LANES)), o2[0:tq], o2[tq:2 * tq]).astype(o_ref.dtype)


def _attn_scratch(tq, tk):
    return [pltpu.VMEM((N_PAIRS, 2 * tq, LANES), BF16),
            pltpu.VMEM((N_PAIRS, 2 * tq, 1), F32),
            pltpu.VMEM((N_PAIRS, 2 * tq, LANES), F32),
            pltpu.VMEM((N_PAIRS, 2 * tq, LANES), F32),
            pltpu.VMEM((2 * tq, tk), F32),
            pltpu.VMEM((2 * tq, tk), BF16)]


def _ordered_key_to_f32(key):
    return lax.bitcast_convert_type(key ^ ((key >> 31) & jnp.int32(0x7FFFFFFF)), F32)


def _dsa_kernel(b_s, qt_s, ph_s, kb_s, nkb_s,
                qi_ref, wi_ref, ki_ref, q_ref, k_ref, vx_ref, o_ref,
                score_scr, q2_scr, m_scr, l_scr, acc_scr, s_scr, p_scr, *, k_sel):
    step = pl.program_id(0)
    qt = qt_s[step]
    phase = ph_s[step]
    kb = kb_s[step]
    nkb = nkb_s[step]
    tq = q_ref.shape[0]
    tk = k_ref.shape[0]

    row = lax.broadcasted_iota(jnp.int32, (tq, 1), 0) + qt * tq
    limit = (row // CHUNK + 1) * CHUNK

    @pl.when(phase == 0)
    def _index():
        kk = ki_ref[...]
        w = wi_ref[...]
        acc = jnp.zeros((tq, tk), F32)
        for j in range(IDX_HEADS // 2):
            qp = qi_ref[:, j * LANES:(j + 1) * LANES]
            first = _half_mask(qp.shape)
            zero = jnp.zeros_like(qp)
            for half in range(2):
                h = 2 * j + half
                qh = jnp.where(first, qp, zero) if half == 0 else jnp.where(first, zero, qp)
                sc = _nt_dot(qh, kk)
                acc = acc + jnp.maximum(sc, 0.0) * w[:, h:h + 1]
        key = lax.broadcasted_iota(jnp.int32, (1, tk), 1) + kb * tk
        sc = jnp.where(key < limit, acc, -jnp.inf)
        score_scr[kb] = sc

    @pl.when(jnp.logical_and(phase == 0, kb == nkb - 1))
    def _select():
        band = 128
        one = jnp.ones((band, LANES), F32)
        zero = jnp.zeros((band, LANES), F32)

        def count(cand, strict):
            c_full = jnp.broadcast_to(cand, (tq, LANES))
            accs = []
            for r0 in range(0, tq, band):
                c_b = c_full[r0:r0 + band]

                def body(j, acc, r0=r0, c_b=c_b):
                    blk = score_scr[j, r0:r0 + band, :]
                    for c in range(tk // LANES):
                        part = blk[:, c * LANES:(c + 1) * LANES]
                        hit = (part > c_b) if strict else (part >= c_b)
                        acc = acc + jnp.where(hit, one, zero)
                    return acc

                accs.append(lax.fori_loop(0, nkb, body, zero))
            return jnp.sum(jnp.concatenate(accs, axis=0), axis=1, keepdims=True)

        def bit_body(i, r):
            cand_u = r | jnp.left_shift(jnp.int32(1), 31 - i)
            cnt = count(_ordered_key_to_f32(cand_u ^ jnp.int32(-2 ** 31)), False)
            return jnp.where(cnt >= k_sel, cand_u, r)

        r = lax.fori_loop(0, 32, bit_body, jnp.zeros((tq, 1), jnp.int32))
        few = limit <= k_sel
        thr = jnp.where(few, -jnp.inf, _ordered_key_to_f32(r ^ jnp.int32(-2 ** 31)))
        need = k_sel - count(thr, True)

        sub = 256
        r_i = lax.broadcasted_iota(jnp.int32, (sub, sub), 0)
        c_i = lax.broadcasted_iota(jnp.int32, (sub, sub), 1)
        tri = jnp.where(r_i <= c_i, 1.0, 0.0).astype(BF16)

        def sel_body(j, carry):
            for c in range(tk // sub):
                cols = slice(c * sub, (c + 1) * sub)
                blk = score_scr[j, :, cols]
                key = lax.broadcasted_iota(jnp.int32, (1, sub), 1) + (j * tk + c * sub)
                eq = blk == thr
                rank = carry + jnp.dot(jnp.where(eq, 1.0, 0.0).astype(BF16), tri,
                                       preferred_element_type=F32)
                bias = jnp.where(eq, jnp.where(rank <= need, 0.0, NEG),
                                 jnp.where(blk > thr, 0.0, NEG))
                score_scr[j, :, cols] = jnp.where(key < limit, bias, NEG)
                carry = rank[:, sub - 1:sub]
            return carry

        lax.fori_loop(0, nkb, sel_body, jnp.zeros((tq, 1), F32))

    @pl.when(jnp.logical_and(phase == 1, kb == 0))
    def _init():
        _stack_pair_queries(q_ref, q2_scr)
        _attn_init(m_scr, l_scr, acc_scr)

    @pl.when(phase == 1)
    def _attend():
        _attn_block(q2_scr, k_ref, vx_ref, m_scr, l_scr, acc_scr, s_scr, p_scr,
                    lambda sv, h: sv + score_scr[kb])

    @pl.when(jnp.logical_and(phase == 1, kb == nkb - 1))
    def _finish():
        _attn_finish(o_ref, l_scr, acc_scr)


def _dsa_schedule(batch, seq, tq, tk):
    rows = []
    for b in range(batch):
        for qt in range(seq // tq):
            nkb = -(-((qt + 1) * tq) // tk)
            for phase in range(2):
                for kb in range(nkb):
                    rows.append((b, qt, phase, kb, nkb))
    return [jnp.asarray(c, jnp.int32) for c in np.asarray(rows, np.int32).T]


def _dsa_attention(qi, wi, ki, q, k, vx, batch, seq, tq=256, tk=1024):
    t = q.shape[0]
    qt_per_b = seq // tq
    kb_per_b = seq // tk
    sched = _dsa_schedule(batch, seq, tq, tk)
    k_sel = min(TOPK_MAX, seq // 4)

    def q_row(i, b, qt, ph, kb, nkb):
        return (b[i] * qt_per_b + qt[i], 0)

    def ki_row(i, b, qt, ph, kb, nkb):
        return (b[i] * kb_per_b + jnp.where(ph[i] == 0, kb[i], nkb[i] - 1), 0)

    def kv_row(i, b, qt, ph, kb, nkb):
        return (b[i] * kb_per_b + jnp.where(ph[i] == 0, 0, kb[i]), 0)

    grid_spec = pltpu.PrefetchScalarGridSpec(
        num_scalar_prefetch=5,
        grid=(int(sched[0].shape[0]),),
        in_specs=[pl.BlockSpec((tq, IDX_HEADS * IDX_DIM), q_row),
                  pl.BlockSpec((tq, LANES), q_row),
                  pl.BlockSpec((tk, LANES), ki_row),
                  pl.BlockSpec((tq, HD), q_row),
                  pl.BlockSpec((tk, HD), kv_row),
                  pl.BlockSpec((tk, 2 * HD), kv_row)],
        out_specs=pl.BlockSpec((tq, HD), q_row),
        scratch_shapes=[pltpu.VMEM((kb_per_b, tq, tk), F32)] + _attn_scratch(tq, tk),
    )
    return pl.pallas_call(
        functools.partial(_dsa_kernel, k_sel=k_sel),
        grid_spec=grid_spec,
        out_shape=jax.ShapeDtypeStruct((t, HD), BF16),
        compiler_params=_cparams(1),
        name="dsa_attention",
    )(*sched, qi, wi, ki, q, k, vx)


def _fox_kernel(b_s, qt_s, kb_s, nkb_s, q_ref, fq_ref, k_ref, vx_ref, fk_ref, o_ref,
                q2_scr, m_scr, l_scr, acc_scr, s_scr, p_scr, fq_scr):
    step = pl.program_id(0)
    qt = qt_s[step]
    kb = kb_s[step]
    nkb = nkb_s[step]
    tq = q_ref.shape[0]
    tk = k_ref.shape[0]

    @pl.when(kb == 0)
    def _init():
        _stack_pair_queries(q_ref, q2_scr)
        _attn_init(m_scr, l_scr, acc_scr)
        fq = fq_ref[...]
        for h in range(N_HEADS):
            fq_scr[h] = fq[:, h:h + 1]

    fk = fk_ref[0]
    row_bias = lambda h: fq_scr[h]
    diagonal = (kb + 1) * tk > qt * tq

    @pl.when(jnp.logical_not(diagonal))
    def _full():
        _attn_block(q2_scr, k_ref, vx_ref, m_scr, l_scr, acc_scr, s_scr, p_scr,
                    lambda sv, h: sv - fk[h:h + 1, :], row_bias)

    @pl.when(diagonal)
    def _masked():
        row = lax.broadcasted_iota(jnp.int32, (tq, 1), 0) + qt * tq
        key = lax.broadcasted_iota(jnp.int32, (1, tk), 1) + kb * tk
        causal = jnp.where(key <= row, 0.0, NEG)
        _attn_block(q2_scr, k_ref, vx_ref, m_scr, l_scr, acc_scr, s_scr, p_scr,
                    lambda sv, h: sv + (causal - fk[h:h + 1, :]), row_bias)

    @pl.when(kb == nkb - 1)
    def _finish():
        _attn_finish(o_ref, l_scr, acc_scr)


def _fox_schedule(batch, seq, tq, tk):
    rows = []
    for b in range(batch):
        for qt in range(seq // tq):
            nkb = -(-((qt + 1) * tq) // tk)
            for kb in range(nkb):
                rows.append((b, qt, kb, nkb))
    return [jnp.asarray(c, jnp.int32) for c in np.asarray(rows, np.int32).T]


def _fox_attention(q, fq, fk_t, k, vx, batch, seq, tq=256, tk=1024):
    t = q.shape[0]
    qt_per_b = seq // tq
    kb_per_b = seq // tk
    sched = _fox_schedule(batch, seq, tq, tk)

    def q_row(i, b, qt, kb, nkb):
        return (b[i] * qt_per_b + qt[i], 0)

    def kv_row(i, b, qt, kb, nkb):
        return (b[i] * kb_per_b + kb[i], 0)

    def fk_row(i, b, qt, kb, nkb):
        return (b[i], 0, kb[i])

    grid_spec = pltpu.PrefetchScalarGridSpec(
        num_scalar_prefetch=4,
        grid=(int(sched[0].shape[0]),),
        in_specs=[pl.BlockSpec((tq, HD), q_row),
                  pl.BlockSpec((tq, LANES), q_row),
                  pl.BlockSpec((tk, HD), kv_row),
                  pl.BlockSpec((tk, 2 * HD), kv_row),
                  pl.BlockSpec((1, N_HEADS, tk), fk_row)],
        out_specs=pl.BlockSpec((tq, HD), q_row),
        scratch_shapes=_attn_scratch(tq, tk) + [pltpu.VMEM((N_HEADS, tq, 1), F32)],
    )
    return pl.pallas_call(
        _fox_kernel,
        grid_spec=grid_spec,
        out_shape=jax.ShapeDtypeStruct((t, HD), BF16),
        compiler_params=_cparams(1),
        name="fox_attention",
    )(*sched, q, fq, k, vx, fk_t)


def _swiglu_partial(xn, wgu_ref, wd_ref, row_scale):
    tf = wd_ref.shape[-2]
    wgu = wgu_ref[(0,) * (len(wgu_ref.shape) - 2)]
    wd = wd_ref[(0,) * (len(wd_ref.shape) - 2)]
    gu = jnp.dot(xn, wgu, preferred_element_type=F32)
    g = gu[:, :tf]
    u = gu[:, tf:]
    a = g * jax.nn.sigmoid(g) * u
    if row_scale is not None:
        a = a * row_scale
    return jnp.dot(a.astype(BF16), wd, preferred_element_type=F32)


def _ffn_kernel(x_ref, g_ref, wgu_ref, wd_ref, o_ref, xn_scr, acc_scr):
    f = pl.program_id(1)

    @pl.when(f == 0)
    def _():
        xn_scr[...] = _rms(x_ref[...], g_ref[...]).astype(BF16)
        acc_scr[...] = jnp.zeros_like(acc_scr)

    acc_scr[...] += _swiglu_partial(xn_scr[...], wgu_ref, wd_ref, None)

    @pl.when(f == pl.num_programs(1) - 1)
    def _():
        o_ref[...] = x_ref[...] + acc_scr[...]


def _ffn(h, g, wgu, wd, tm=512):
    t, d = h.shape
    n_f, _, two_tf = wgu.shape
    tf = two_tf // 2
    return pl.pallas_call(
        _ffn_kernel,
        grid=(t // tm, n_f),
        in_specs=[pl.BlockSpec((tm, d), lambda i, f: (i, 0)),
                  pl.BlockSpec((1, d), lambda i, f: (0, 0)),
                  pl.BlockSpec((1, d, two_tf), lambda i, f: (f, 0, 0)),
                  pl.BlockSpec((tf, d), lambda i, f: (f, 0))],
        out_specs=pl.BlockSpec((tm, d), lambda i, f: (i, 0)),
        out_shape=jax.ShapeDtypeStruct((t, d), F32),
        scratch_shapes=[pltpu.VMEM((tm, d), BF16), pltpu.VMEM((tm, d), F32)],
        compiler_params=_cparams(2),
        name="ffn_dense",
    )(h, g, wgu, wd)


META_E1, META_E2, META_POS1, META_POS2, META_G1, META_G2 = range(6)
ROW_TILES = 8


def _to_row_tiles(ref, x):
    for s in range(ROW_TILES):
        ref[:, s, :] = x[:, s * LANES:(s + 1) * LANES]


def _moe_route_kernel(x_ref, g_ref, wr_ref, xn_ref, meta_ref, cnt_ref, carry_ref):
    i = pl.program_id(0)
    tm = x_ref.shape[0]

    @pl.when(i == 0)
    def _():
        carry_ref[...] = jnp.zeros_like(carry_ref)

    xn = _rms(x_ref[...], g_ref[...])
    _to_row_tiles(xn_ref, xn)
    logits = jnp.dot(xn, wr_ref[...], preferred_element_type=F32, precision=lax.Precision.HIGHEST)
    lane = lax.broadcasted_iota(jnp.int32, (tm, LANES), 1)
    lg = jnp.where(lane < N_EXPERTS, logits, -jnp.inf)
    m1 = jnp.max(lg, axis=1, keepdims=True)
    i1 = jnp.min(jnp.where(lg == m1, lane, LANES), axis=1, keepdims=True)
    lg2 = jnp.where(lane == i1, -jnp.inf, lg)
    m2 = jnp.max(lg2, axis=1, keepdims=True)
    i2 = jnp.min(jnp.where(lg2 == m2, lane, LANES), axis=1, keepdims=True)
    e2 = jnp.exp(m2 - m1)
    den = 1.0 + e2
    routed = jnp.where(jnp.logical_or(lane == i1, lane == i2), 1.0, 0.0)
    r_i = lax.broadcasted_iota(jnp.int32, (tm, tm), 0)
    c_i = lax.broadcasted_iota(jnp.int32, (tm, tm), 1)
    tri = jnp.where(c_i < r_i, 1.0, 0.0).astype(BF16)
    before = jnp.dot(tri, routed.astype(BF16), preferred_element_type=F32) + carry_ref[...]
    pos1 = jnp.sum(jnp.where(lane == i1, before, 0.0), axis=1, keepdims=True)
    pos2 = jnp.sum(jnp.where(lane == i2, before, 0.0), axis=1, keepdims=True)
    total = before[tm - 1:tm, :] + routed[tm - 1:tm, :]
    carry_ref[...] = total
    cnt_ref[...] = total
    meta = jnp.zeros((tm, LANES), F32)
    for col, val in ((META_E1, i1.astype(F32)), (META_E2, i2.astype(F32)), (META_POS1, pos1),
                     (META_POS2, pos2), (META_G1, 1.0 / den), (META_G2, e2 / den)):
        meta = jnp.where(lane == col, val, meta)
    meta_ref[...] = meta


def _moe_route(h, g, w_router, tm=512):
    t, d = h.shape
    return pl.pallas_call(
        _moe_route_kernel,
        grid=(t // tm,),
        in_specs=[pl.BlockSpec((tm, d), lambda i: (i, 0)),
                  pl.BlockSpec((1, d), lambda i: (0, 0)),
                  pl.BlockSpec((d, LANES), lambda i: (0, 0))],
        out_specs=[pl.BlockSpec((tm, ROW_TILES, LANES), lambda i: (i, 0, 0)),
                   pl.BlockSpec((tm, LANES), lambda i: (i, 0)),
                   pl.BlockSpec((1, LANES), lambda i: (0, 0))],
        out_shape=[jax.ShapeDtypeStruct((t, ROW_TILES, LANES), F32),
                   jax.ShapeDtypeStruct((t, LANES), F32),
                   jax.ShapeDtypeStruct((1, LANES), F32)],
        scratch_shapes=[pltpu.VMEM((1, LANES), F32)],
        compiler_params=_cparams(1),
        name="moe_route",
    )(h, g, w_router)


def _moe_expert_kernel(src_s, tile_e_s, n_used_s, xn_hbm, wgu_ref, wd_ref, y_ref,
                       rows_scr, xs_scr, acc_scr, sem, *, n_f):
    i = pl.program_id(0)
    f = pl.program_id(1)
    tm = y_ref.shape[0]
    used = i < n_used_s[0]
    next_used = i + 1 < n_used_s[0]
    slot = i % 2
    part = tm // n_f

    def whole_tile(s):
        return pltpu.make_async_copy(xn_hbm.at[pl.ds(0, tm)], rows_scr.at[s], sem.at[s])

    @pl.when(jnp.logical_and(i == 0, f == 0))
    def _first():
        def start(r, c):
            pltpu.make_async_copy(xn_hbm.at[src_s[r]], rows_scr.at[0, r], sem.at[0]).start()
            return c
        lax.fori_loop(0, tm, start, 0)

    @pl.when(jnp.logical_and(used, f == 0))
    def _unpack():
        whole_tile(slot).wait()
        for s in range(ROW_TILES):
            xs_scr[:, s * LANES:(s + 1) * LANES] = rows_scr[slot, :, s, :].astype(BF16)
        acc_scr[...] = jnp.zeros_like(acc_scr)

    def fetch_next():
        base = (i + 1) * tm + f * part
        for r in range(part):
            pltpu.make_async_copy(xn_hbm.at[src_s[base + r]], rows_scr.at[1 - slot, f * part + r],
                                  sem.at[1 - slot]).start()

    @pl.when(jnp.logical_and(used, next_used))
    def _compute_and_fetch():
        fetch_next()
        acc_scr[...] += _swiglu_partial(xs_scr[...], wgu_ref, wd_ref, None)

    @pl.when(jnp.logical_and(used, jnp.logical_not(next_used)))
    def _compute():
        acc_scr[...] += _swiglu_partial(xs_scr[...], wgu_ref, wd_ref, None)

    @pl.when(f == n_f - 1)
    def _store():
        @pl.when(used)
        def _():
            _to_row_tiles(y_ref, acc_scr[...])

        @pl.when(jnp.logical_not(used))
        def _():
            y_ref[...] = jnp.zeros_like(y_ref)


def _moe_experts(xn3, src, tile_e, n_used, wgu, wd, tm):
    n_rows = src.shape[0]
    n_e, n_f, d, two_tf = wgu.shape
    tf = two_tf // 2
    grid_spec = pltpu.PrefetchScalarGridSpec(
        num_scalar_prefetch=3,
        grid=(n_rows // tm, n_f),
        in_specs=[pl.BlockSpec(memory_space=pl.ANY),
                  pl.BlockSpec((1, 1, d, two_tf), lambda i, f, src, te, nu: (te[i], f, 0, 0)),
                  pl.BlockSpec((1, tf, d), lambda i, f, src, te, nu: (te[i], f, 0))],
        out_specs=pl.BlockSpec((tm, ROW_TILES, LANES), lambda i, f, src, te, nu: (i, 0, 0)),
        scratch_shapes=[pltpu.VMEM((2, tm, ROW_TILES, LANES), F32),
                        pltpu.VMEM((tm, d), BF16),
                        pltpu.VMEM((tm, d), F32),
                        pltpu.SemaphoreType.DMA((2,))],
    )
    return pl.pallas_call(
        functools.partial(_moe_expert_kernel, n_f=n_f),
        grid_spec=grid_spec,
        out_shape=jax.ShapeDtypeStruct((n_rows, ROW_TILES, LANES), F32),
        compiler_params=_cparams(2),
        name="moe_experts",
    )(src, tile_e, n_used, xn3, wgu, wd)


def _moe_combine_kernel(d1_s, d2_s, x_ref, meta_ref, gf_ref, y_hbm, o_ref, y_scr, sem, *, final_norm):
    i = pl.program_id(0)
    tm = x_ref.shape[0]
    slot = i % 2

    def fetch(tile, s):
        def start(r, c):
            pltpu.make_async_copy(y_hbm.at[d1_s[tile * tm + r]], y_scr.at[s, 0, r], sem.at[s]).start()
            pltpu.make_async_copy(y_hbm.at[d2_s[tile * tm + r]], y_scr.at[s, 1, r], sem.at[s]).start()
            return c
        lax.fori_loop(0, tm, start, 0)

    @pl.when(i == 0)
    def _first():
        fetch(0, 0)

    @pl.when(i + 1 < pl.num_programs(0))
    def _next():
        fetch(i + 1, 1 - slot)

    for which in range(2):
        pltpu.make_async_copy(y_hbm.at[pl.ds(0, tm)], y_scr.at[slot, which], sem.at[slot]).wait()
    meta = meta_ref[...]
    g1 = meta[:, META_G1:META_G1 + 1]
    g2 = meta[:, META_G2:META_G2 + 1]
    for s in range(ROW_TILES):
        cols = slice(s * LANES, (s + 1) * LANES)
        o_ref[:, cols] = (x_ref[:, cols] + g1 * y_scr[slot, 0, :, s, :] + g2 * y_scr[slot, 1, :, s, :])
    if final_norm:
        o_ref[...] = _rms(o_ref[...], gf_ref[...])


def _moe_combine(h, meta, g_final, y3, dest1, dest2, final_norm, tm=512):
    t, d = h.shape
    grid_spec = pltpu.PrefetchScalarGridSpec(
        num_scalar_prefetch=2,
        grid=(t // tm,),
        in_specs=[pl.BlockSpec((tm, d), lambda i, d1, d2: (i, 0)),
                  pl.BlockSpec((tm, LANES), lambda i, d1, d2: (i, 0)),
                  pl.BlockSpec((1, d), lambda i, d1, d2: (0, 0)),
                  pl.BlockSpec(memory_space=pl.ANY)],
        out_specs=pl.BlockSpec((tm, d), lambda i, d1, d2: (i, 0)),
        scratch_shapes=[pltpu.VMEM((2, 2, tm, ROW_TILES, LANES), F32),
                        pltpu.SemaphoreType.DMA((2,))],
    )
    return pl.pallas_call(
        functools.partial(_moe_combine_kernel, final_norm=final_norm),
        grid_spec=grid_spec,
        out_shape=jax.ShapeDtypeStruct((t, d), F32),
        compiler_params=_cparams(1),
        name="moe_combine",
    )(dest1, dest2, h, meta, g_final, y3)


def _moe(h, g, w_router, wgu, wd, g_final, final_norm, tm=512):
    t, d = h.shape
    xn3, meta, cnt = _moe_route(h, g, w_router)
    counts = cnt[0, :N_EXPERTS].astype(jnp.int32)
    padded = (counts + tm - 1) // tm * tm
    ends = jnp.cumsum(padded)
    starts = ends - padded
    e1 = meta[:, META_E1].astype(jnp.int32)
    e2 = meta[:, META_E2].astype(jnp.int32)
    dest1 = starts[e1] + meta[:, META_POS1].astype(jnp.int32)
    dest2 = starts[e2] + meta[:, META_POS2].astype(jnp.int32)
    n_rows = 2 * t + N_EXPERTS * tm
    tok = jnp.arange(t, dtype=jnp.int32)
    src = jnp.zeros((n_rows,), jnp.int32).at[dest1].set(tok).at[dest2].set(tok)
    tile_start = jnp.arange(n_rows // tm, dtype=jnp.int32) * tm
    tile_e = jnp.minimum(jnp.searchsorted(ends, tile_start, side="right"), N_EXPERTS - 1).astype(jnp.int32)
    n_used = (ends[-1:] // tm).astype(jnp.int32)
    y3 = _moe_experts(xn3, src, tile_e, n_used, wgu, wd, tm)
    return _moe_combine(h, meta, g_final, y3, dest1, dest2, final_norm)


def _split_gate_up(w, n_f):
    d, two_f = w.shape[-2:]
    tf = two_f // 2 // n_f
    g = w[..., :two_f // 2].reshape(*w.shape[:-1], n_f, tf)
    u = w[..., two_f // 2:].reshape(*w.shape[:-1], n_f, tf)
    gu = jnp.concatenate([g, u], axis=-1)
    return jnp.moveaxis(gu, -2, -3).astype(BF16)


def _rope_tables(seq):
    pos = jnp.arange(seq, dtype=F32)
    inv_freq = 1.0 / (ROPE_THETA ** (jnp.arange(0, HEAD_DIM, 2, dtype=F32) / HEAD_DIM))
    ang = pos[:, None] * inv_freq[None, :]
    cos, sin = jnp.cos(ang), jnp.sin(ang)
    cos_t = jnp.tile(cos, (1, LANES // (HEAD_DIM // 2)))
    sin_t = jnp.tile(jnp.concatenate([-sin, sin], axis=1), (1, LANES // HEAD_DIM))
    return cos_t, sin_t


def kernel(x, a_norm, a_w_in, a_w_out, kv_norm, w_kv, b_f, b_norm, b_w_q, b_w_out, ffn_norm,
           dense_w_gate_up, dense_w_down, moe_router, moe_w_gate_up, moe_w_down, final_norm):
    batch, seq, d = x.shape
    depth = ffn_norm.shape[0]
    n_a = a_norm.shape[0]
    n_f = 2
    cos_t, sin_t = _rope_tables(seq)
    idx_w = IDX_HEADS * IDX_DIM

    h = x.reshape(batch * seq, d)
    k_sh = vx_sh = f_sh = fk_t = None
    for i in range(depth):
        if i < n_a:
            w = a_w_in[i]
            ki_w = w[:, 3 * HD + idx_w:3 * HD + idx_w + IDX_DIM]
            wi_w = w[:, 3 * HD + idx_w + IDX_DIM:]
            w_p = jnp.concatenate(
                [w[:, :3 * HD + idx_w], ki_w, ki_w, wi_w,
                 jnp.zeros((d, LANES - IDX_HEADS), w.dtype)], axis=1).astype(BF16)
            q, k, vx, qi, ki, wi = _a_proj(h, a_norm[i][None, :], w_p, cos_t, sin_t, seq)
            o = _dsa_attention(qi, wi, ki, q, k, vx, batch, seq)
            h = _out_proj(o, a_w_out[i].astype(BF16), h)
        else:
            j = i - n_a
            if k_sh is None:
                w_p = jnp.concatenate(
                    [w_kv, jnp.zeros((d, LANES - N_HEADS), w_kv.dtype)], axis=1).astype(BF16)
                b_p = jnp.concatenate([b_f, jnp.zeros((LANES - N_HEADS,), b_f.dtype)])[None, :]
                k_sh, vx_sh, f_sh = _kv_proj(h, kv_norm[None, :], w_p, b_p, seq)
                fk_t = f_sh[:, :N_HEADS].reshape(batch, seq, N_HEADS).transpose(0, 2, 1)
            q = _q_proj(h, b_norm[j][None, :], b_w_q[j].astype(BF16))
            o = _fox_attention(q, f_sh, fk_t, k_sh, vx_sh, batch, seq)
            h = _out_proj(o, b_w_out[j].astype(BF16), h)
        if i % 2 == 0:
            wd = dense_w_down[i // 2].astype(BF16)
            h = _ffn(h, ffn_norm[i][None, :], _split_gate_up(dense_w_gate_up[i // 2], n_f), wd)
        else:
            e = i // 2
            wr = jnp.concatenate(
                [moe_router[e], jnp.zeros((d, LANES - N_EXPERTS), moe_router.dtype)], axis=1)
            wd = moe_w_down[e].astype(BF16).reshape(N_EXPERTS, -1, d)
            last = i == depth - 1
            h = _moe(h, ffn_norm[i][None, :], wr, _split_gate_up(moe_w_gate_up[e], n_f), wd,
                     final_norm[None, :], last)
    if depth % 2 == 1:
        raise NotImplementedError("final norm is fused into the last expert mixer")
    return h.reshape(batch, seq, d)
```

```python
import functools
import math

import numpy as np
import jax
import jax.numpy as jnp
from jax import lax
from jax.experimental import pallas as pl
from jax.experimental.pallas import tpu as pltpu

N_HEADS = 16
HEAD_DIM = 64
IDX_HEADS = 8
IDX_DIM = 64
TOPK_MAX = 256
CHUNK = 64
ROPE_THETA = 10000.0
N_EXPERTS = 8
RMS_EPS = 1e-6

LANES = 128
HD = N_HEADS * HEAD_DIM
N_PAIRS = N_HEADS // 2
NEG = -1e30
LOG2E = math.log2(math.e)
VMEM_LIMIT = 52 * 1024 * 1024

F32 = jnp.float32
BF16 = jnp.bfloat16


def _cparams(n_axes):
    return pltpu.CompilerParams(dimension_semantics=("arbitrary",) * n_axes,
                                vmem_limit_bytes=VMEM_LIMIT)


def _rms(x, g):
    return x * lax.rsqrt(jnp.mean(x * x, axis=-1, keepdims=True) + RMS_EPS) * g


def _half_mask(shape):
    return lax.broadcasted_iota(jnp.int32, shape, len(shape) - 1) < HEAD_DIM


def _nt_dot(a, b):
    return lax.dot_general(a, b, (((1,), (1,)), ((), ())), preferred_element_type=F32)


def _store_values_with_ones(vx_ref, p, c0):
    ones = jnp.ones((p.shape[0], LANES), BF16)
    for j in range(p.shape[1] // LANES):
        pair = (c0 // LANES) + j
        vx_ref[:, 2 * pair * LANES:(2 * pair + 1) * LANES] = p[:, j * LANES:(j + 1) * LANES].astype(BF16)
        vx_ref[:, (2 * pair + 1) * LANES:(2 * pair + 2) * LANES] = ones


def _rope_chunk(p, cos, sin_signed):
    fwd = pltpu.roll(p, LANES - 32, 1)
    bwd = pltpu.roll(p, 32, 1)
    lane = lax.broadcasted_iota(jnp.int32, p.shape, 1)
    partner = jnp.where((lane % HEAD_DIM) < HEAD_DIM // 2, fwd, bwd)
    return p * cos + partner * sin_signed


def _a_proj_kernel(x_ref, g_ref, w_ref, cos_ref, sin_ref,
                   q_ref, k_ref, vx_ref, qi_ref, ki_ref, wi_ref):
    xn = _rms(x_ref[...], g_ref[...]).astype(BF16)
    cos = cos_ref[...]
    sin = sin_ref[...]
    seg = 512

    def proj(c0, width):
        return jnp.dot(xn, w_ref[:, c0:c0 + width], preferred_element_type=F32)

    def rope_store(p, out_ref, o0, scale):
        for j in range(p.shape[1] // LANES):
            r = _rope_chunk(p[:, j * LANES:(j + 1) * LANES], cos, sin)
            if scale != 1.0:
                r = r * scale
            out_ref[:, o0 + j * LANES:o0 + (j + 1) * LANES] = r.astype(out_ref.dtype)

    for s in range(HD // seg):
        rope_store(proj(s * seg, seg), q_ref, s * seg, HEAD_DIM ** -0.5 * LOG2E)
    for s in range(HD // seg):
        rope_store(proj(HD + s * seg, seg), k_ref, s * seg, 1.0)
    for s in range(HD // seg):
        _store_values_with_ones(vx_ref, proj(2 * HD + s * seg, seg), s * seg)
    rope_store(proj(3 * HD, IDX_HEADS * IDX_DIM), qi_ref, 0, 1.0)
    tail = proj(3 * HD + IDX_HEADS * IDX_DIM, 2 * LANES)
    rope_store(tail[:, :LANES], ki_ref, 0, 1.0)
    wi_ref[...] = tail[:, LANES:] * (IDX_HEADS ** -0.5)


def _a_proj(h, g, w, cos_t, sin_t, seq, tm=512):
    t, d = h.shape
    n_w = w.shape[1]
    tiles_per_seq = seq // tm
    row = lambda i: (i, 0)
    tab = lambda i: (i % tiles_per_seq, 0)
    const = lambda i: (0, 0)
    return pl.pallas_call(
        _a_proj_kernel,
        grid=(t // tm,),
        in_specs=[pl.BlockSpec((tm, d), row),
                  pl.BlockSpec((1, d), const),
                  pl.BlockSpec((d, n_w), const),
                  pl.BlockSpec((tm, LANES), tab),
                  pl.BlockSpec((tm, LANES), tab)],
        out_specs=[pl.BlockSpec((tm, HD), row),
                   pl.BlockSpec((tm, HD), row),
                   pl.BlockSpec((tm, 2 * HD), row),
                   pl.BlockSpec((tm, IDX_HEADS * IDX_DIM), row),
                   pl.BlockSpec((tm, LANES), row),
                   pl.BlockSpec((tm, LANES), row)],
        out_shape=[jax.ShapeDtypeStruct((t, HD), BF16),
                   jax.ShapeDtypeStruct((t, HD), BF16),
                   jax.ShapeDtypeStruct((t, 2 * HD), BF16),
                   jax.ShapeDtypeStruct((t, IDX_HEADS * IDX_DIM), BF16),
                   jax.ShapeDtypeStruct((t, LANES), BF16),
                   jax.ShapeDtypeStruct((t, LANES), F32)],
        compiler_params=_cparams(1),
        name="a_proj",
    )(h, g, w, cos_t, sin_t)


def _kv_proj_kernel(x_ref, g_ref, w_ref, bf_ref, k_ref, vx_ref, cum_ref, carry_ref, *, tiles_per_seq):
    i = pl.program_id(0)
    tm = x_ref.shape[0]

    @pl.when(i % tiles_per_seq == 0)
    def _():
        carry_ref[...] = jnp.zeros_like(carry_ref)

    xn = _rms(x_ref[...], g_ref[...]).astype(BF16)
    seg = 512
    for s in range(HD // seg):
        k_ref[:, s * seg:(s + 1) * seg] = jnp.dot(
            xn, w_ref[:, s * seg:(s + 1) * seg], preferred_element_type=F32).astype(BF16)
    for s in range(HD // seg):
        _store_values_with_ones(
            vx_ref, jnp.dot(xn, w_ref[:, HD + s * seg:HD + (s + 1) * seg], preferred_element_type=F32),
            s * seg)
    f_logit = jnp.dot(xn, w_ref[:, 2 * HD:2 * HD + LANES], preferred_element_type=F32)
    log_f = jax.nn.log_sigmoid(f_logit + bf_ref[...])
    r_i = lax.broadcasted_iota(jnp.int32, (tm, tm), 0)
    c_i = lax.broadcasted_iota(jnp.int32, (tm, tm), 1)
    tri = jnp.where(c_i <= r_i, 1.0, 0.0).astype(BF16)
    x1 = log_f.astype(BF16)
    rem = log_f - x1.astype(F32)
    x2 = rem.astype(BF16)
    x3 = (rem - x2.astype(F32)).astype(BF16)
    cum = (jnp.dot(tri, x1, preferred_element_type=F32)
           + jnp.dot(tri, x2, preferred_element_type=F32)
           + jnp.dot(tri, x3, preferred_element_type=F32)) + carry_ref[...]
    cum_ref[...] = cum * LOG2E
    carry_ref[...] = cum[tm - 1:tm, :]


def _kv_proj(h, g, w, b_f, seq, tm=512):
    t, d = h.shape
    row = lambda i: (i, 0)
    const = lambda i: (0, 0)
    return pl.pallas_call(
        functools.partial(_kv_proj_kernel, tiles_per_seq=seq // tm),
        grid=(t // tm,),
        in_specs=[pl.BlockSpec((tm, d), row),
                  pl.BlockSpec((1, d), const),
                  pl.BlockSpec((d, w.shape[1]), const),
                  pl.BlockSpec((1, LANES), const)],
        out_specs=[pl.BlockSpec((tm, HD), row),
                   pl.BlockSpec((tm, 2 * HD), row),
                   pl.BlockSpec((tm, LANES), row)],
        out_shape=[jax.ShapeDtypeStruct((t, HD), BF16),
                   jax.ShapeDtypeStruct((t, 2 * HD), BF16),
                   jax.ShapeDtypeStruct((t, LANES), F32)],
        scratch_shapes=[pltpu.VMEM((1, LANES), F32)],
        compiler_params=_cparams(1),
        name="kv_proj",
    )(h, g, w, b_f)


def _q_proj_kernel(x_ref, g_ref, w_ref, q_ref):
    xn = _rms(x_ref[...], g_ref[...]).astype(BF16)
    seg = 512
    for s in range(HD // seg):
        p = jnp.dot(xn, w_ref[:, s * seg:(s + 1) * seg], preferred_element_type=F32)
        q_ref[:, s * seg:(s + 1) * seg] = (p * (HEAD_DIM ** -0.5 * LOG2E)).astype(BF16)


def _q_proj(h, g, w, tm=512):
    t, d = h.shape
    row = lambda i: (i, 0)
    const = lambda i: (0, 0)
    return pl.pallas_call(
        _q_proj_kernel,
        grid=(t // tm,),
        in_specs=[pl.BlockSpec((tm, d), row),
                  pl.BlockSpec((1, d), const),
                  pl.BlockSpec((d, HD), const)],
        out_specs=pl.BlockSpec((tm, HD), row),
        out_shape=jax.ShapeDtypeStruct((t, HD), BF16),
        compiler_params=_cparams(1),
        name="q_proj",
    )(h, g, w)


def _out_proj_kernel(o_ref, w_ref, h_ref, out_ref):
    out_ref[...] = h_ref[...] + jnp.dot(o_ref[...], w_ref[...], preferred_element_type=F32)


def _out_proj(o, w, h, tm=512):
    t, d = h.shape
    row = lambda i: (i, 0)
    const = lambda i: (0, 0)
    return pl.pallas_call(
        _out_proj_kernel,
        grid=(t // tm,),
        in_specs=[pl.BlockSpec((tm, HD), row),
                  pl.BlockSpec((HD, d), const),
                  pl.BlockSpec((tm, d), row)],
        out_specs=pl.BlockSpec((tm, d), row),
        out_shape=jax.ShapeDtypeStruct((t, d), F32),
        compiler_params=_cparams(1),
        name="out_proj",
    )(o, w, h)


def _stack_pair_queries(q_ref, q2_scr):
    tq = q_ref.shape[0]
    for j in range(N_PAIRS):
        qp = q_ref[:, j * LANES:(j + 1) * LANES]
        first = _half_mask(qp.shape)
        zero = jnp.zeros_like(qp)
        q2_scr[j, 0:tq, :] = jnp.where(first, qp, zero)
        q2_scr[j, tq:2 * tq, :] = jnp.where(first, zero, qp)


def _attn_init(m_scr, l_scr, acc_scr):
    m_scr[...] = jnp.full(m_scr.shape, NEG, F32)
    l_scr[...] = jnp.zeros_like(l_scr)
    acc_scr[...] = jnp.zeros_like(acc_scr)


def _attn_block(q2_scr, k_ref, vx_ref, m_scr, l_scr, acc_scr, s_scr, p_scr, bias_fn,
                row_bias_fn=None):
    tq = s_scr.shape[0] // 2
    for j in range(N_PAIRS):
        kp = k_ref[:, j * LANES:(j + 1) * LANES]
        sv = _nt_dot(q2_scr[j], kp)
        alphas = []
        for half in range(2):
            rows = slice(half * tq, (half + 1) * tq)
            s_scr[rows, :] = bias_fn(sv[rows], 2 * j + half)
            m_prev = m_scr[j, rows, :]
            m_blk = jnp.max(s_scr[rows, :], axis=1, keepdims=True)
            if row_bias_fn is not None:
                m_blk = m_blk + row_bias_fn(2 * j + half)
            m_new = jnp.maximum(m_prev, m_blk)
            m_scr[j, rows, :] = m_new
            alphas.append(jnp.exp2(m_prev - m_new))
            shift = m_new if row_bias_fn is None else m_new - row_bias_fn(2 * j + half)
            p_scr[rows, :] = jnp.exp2(s_scr[rows, :] - shift).astype(BF16)
        pv = jnp.dot(p_scr[...], vx_ref[:, 2 * j * LANES:2 * (j + 1) * LANES],
                     preferred_element_type=F32)
        for half in range(2):
            rows = slice(half * tq, (half + 1) * tq)
            acc_scr[j, rows, :] = alphas[half] * acc_scr[j, rows, :] + pv[rows, :LANES]
            l_scr[j, rows, :] = alphas[half] * l_scr[j, rows, :] + pv[rows, LANES:]


def _attn_finish(o_ref, l_scr, acc_scr):
    tq = o_ref.shape[0]
    for j in range(N_PAIRS):
        o2 = acc_scr[j] / l_scr[j]
        o_ref[:, j * LANES:(j + 1) * LANES] = jnp.where(
            _half_mask((tq, LANES)), o2[0:tq], o2[tq:2 * tq]).astype(o_ref.dtype)


def _attn_scratch(tq, tk):
    return [pltpu.VMEM((N_PAIRS, 2 * tq, LANES), BF16),
            pltpu.VMEM((N_PAIRS, 2 * tq, 1), F32),
            pltpu.VMEM((N_PAIRS, 2 * tq, LANES), F32),
            pltpu.VMEM((N_PAIRS, 2 * tq, LANES), F32),
            pltpu.VMEM((2 * tq, tk), F32),
            pltpu.VMEM((2 * tq, tk), BF16)]


def _ordered_key_to_f32(key):
    return lax.bitcast_convert_type(key ^ ((key >> 31) & jnp.int32(0x7FFFFFFF)), F32)


def _dsa_kernel(b_s, qt_s, ph_s, kb_s, nkb_s,
                qi_ref, wi_ref, ki_ref, q_ref, k_ref, vx_ref, o_ref,
                score_scr, q2_scr, m_scr, l_scr, acc_scr, s_scr, p_scr, *, k_sel):
    step = pl.program_id(0)
    qt = qt_s[step]
    phase = ph_s[step]
    kb = kb_s[step]
    nkb = nkb_s[step]
    tq = q_ref.shape[0]
    tk = k_ref.shape[0]

    row = lax.broadcasted_iota(jnp.int32, (tq, 1), 0) + qt * tq
    limit = (row // CHUNK + 1) * CHUNK

    @pl.when(phase == 0)
    def _index():
        kk = ki_ref[...]
        w = wi_ref[...]
        acc = jnp.zeros((tq, tk), F32)
        for j in range(IDX_HEADS // 2):
            qp = qi_ref[:, j * LANES:(j + 1) * LANES]
            first = _half_mask(qp.shape)
            zero = jnp.zeros_like(qp)
            for half in range(2):
                h = 2 * j + half
                qh = jnp.where(first, qp, zero) if half == 0 else jnp.where(first, zero, qp)
                sc = _nt_dot(qh, kk)
                acc = acc + jnp.maximum(sc, 0.0) * w[:, h:h + 1]
        key = lax.broadcasted_iota(jnp.int32, (1, tk), 1) + kb * tk
        sc = jnp.where(key < limit, acc, -jnp.inf)
        score_scr[kb] = sc

    @pl.when(jnp.logical_and(phase == 0, kb == nkb - 1))
    def _select():
        band = 128
        one = jnp.ones((band, LANES), F32)
        zero = jnp.zeros((band, LANES), F32)

        def count(cand, strict):
            c_full = jnp.broadcast_to(cand, (tq, LANES))
            accs = []
            for r0 in range(0, tq, band):
                c_b = c_full[r0:r0 + band]

                def body(j, acc, r0=r0, c_b=c_b):
                    blk = score_scr[j, r0:r0 + band, :]
                    for c in range(tk // LANES):
                        part = blk[:, c * LANES:(c + 1) * LANES]
                        hit = (part > c_b) if strict else (part >= c_b)
                        acc = acc + jnp.where(hit, one, zero)
                    return acc

                accs.append(lax.fori_loop(0, nkb, body, zero))
            return jnp.sum(jnp.concatenate(accs, axis=0), axis=1, keepdims=True)

        def bit_body(i, r):
            cand_u = r | jnp.left_shift(jnp.int32(1), 31 - i)
            cnt = count(_ordered_key_to_f32(cand_u ^ jnp.int32(-2 ** 31)), False)
            return jnp.where(cnt >= k_sel, cand_u, r)

        r = lax.fori_loop(0, 32, bit_body, jnp.zeros((tq, 1), jnp.int32))
        few = limit <= k_sel
        thr = jnp.where(few, -jnp.inf, _ordered_key_to_f32(r ^ jnp.int32(-2 ** 31)))
        need = k_sel - count(thr, True)

        sub = 256
        r_i = lax.broadcasted_iota(jnp.int32, (sub, sub), 0)
        c_i = lax.broadcasted_iota(jnp.int32, (sub, sub), 1)
        tri = jnp.where(r_i <= c_i, 1.0, 0.0).astype(BF16)

        def sel_body(j, carry):
            for c in range(tk // sub):
                cols = slice(c * sub, (c + 1) * sub)
                blk = score_scr[j, :, cols]
                key = lax.broadcasted_iota(jnp.int32, (1, sub), 1) + (j * tk + c * sub)
                eq = blk == thr
                rank = carry + jnp.dot(jnp.where(eq, 1.0, 0.0).astype(BF16), tri,
                                       preferred_element_type=F32)
                bias = jnp.where(eq, jnp.where(rank <= need, 0.0, NEG),
                                 jnp.where(blk > thr, 0.0, NEG))
                score_scr[j, :, cols] = jnp.where(key < limit, bias, NEG)
                carry = rank[:, sub - 1:sub]
            return carry

        lax.fori_loop(0, nkb, sel_body, jnp.zeros((tq, 1), F32))

    @pl.when(jnp.logical_and(phase == 1, kb == 0))
    def _init():
        _stack_pair_queries(q_ref, q2_scr)
        _attn_init(m_scr, l_scr, acc_scr)

    @pl.when(phase == 1)
    def _attend():
        _attn_block(q2_scr, k_ref, vx_ref, m_scr, l_scr, acc_scr, s_scr, p_scr,
                    lambda sv, h: sv + score_scr[kb])

    @pl.when(jnp.logical_and(phase == 1, kb == nkb - 1))
    def _finish():
        _attn_finish(o_ref, l_scr, acc_scr)


def _dsa_schedule(batch, seq, tq, tk):
    rows = []
    for b in range(batch):
        for qt in range(seq // tq):
            nkb = -(-((qt + 1) * tq) // tk)
            for phase in range(2):
                for kb in range(nkb):
                    rows.append((b, qt, phase, kb, nkb))
    return [jnp.asarray(c, jnp.int32) for c in np.asarray(rows, np.int32).T]


def _dsa_attention(qi, wi, ki, q, k, vx, batch, seq, tq=256, tk=1024):
    t = q.shape[0]
    qt_per_b = seq // tq
    kb_per_b = seq // tk
    sched = _dsa_schedule(batch, seq, tq, tk)
    k_sel = min(TOPK_MAX, seq // 4)

    def q_row(i, b, qt, ph, kb, nkb):
        return (b[i] * qt_per_b + qt[i], 0)

    def ki_row(i, b, qt, ph, kb, nkb):
        return (b[i] * kb_per_b + jnp.where(ph[i] == 0, kb[i], nkb[i] - 1), 0)

    def kv_row(i, b, qt, ph, kb, nkb):
        return (b[i] * kb_per_b + jnp.where(ph[i] == 0, 0, kb[i]), 0)

    grid_spec = pltpu.PrefetchScalarGridSpec(
        num_scalar_prefetch=5,
        grid=(int(sched[0].shape[0]),),
        in_specs=[pl.BlockSpec((tq, IDX_HEADS * IDX_DIM), q_row),
                  pl.BlockSpec((tq, LANES), q_row),
                  pl.BlockSpec((tk, LANES), ki_row),
                  pl.BlockSpec((tq, HD), q_row),
                  pl.BlockSpec((tk, HD), kv_row),
                  pl.BlockSpec((tk, 2 * HD), kv_row)],
        out_specs=pl.BlockSpec((tq, HD), q_row),
        scratch_shapes=[pltpu.VMEM((kb_per_b, tq, tk), F32)] + _attn_scratch(tq, tk),
    )
    return pl.pallas_call(
        functools.partial(_dsa_kernel, k_sel=k_sel),
        grid_spec=grid_spec,
        out_shape=jax.ShapeDtypeStruct((t, HD), BF16),
        compiler_params=_cparams(1),
        name="dsa_attention",
    )(*sched, qi, wi, ki, q, k, vx)


def _fox_kernel(b_s, qt_s, kb_s, nkb_s, q_ref, fq_ref, k_ref, vx_ref, fk_ref, o_ref,
                q2_scr, m_scr, l_scr, acc_scr, s_scr, p_scr, fq_scr):
    step = pl.program_id(0)
    qt = qt_s[step]
    kb = kb_s[step]
    nkb = nkb_s[step]
    tq = q_ref.shape[0]
    tk = k_ref.shape[0]

    @pl.when(kb == 0)
    def _init():
        _stack_pair_queries(q_ref, q2_scr)
        _attn_init(m_scr, l_scr, acc_scr)
        fq = fq_ref[...]
        for h in range(N_HEADS):
            fq_scr[h] = fq[:, h:h + 1]

    fk = fk_ref[0]
    row_bias = lambda h: fq_scr[h]
    diagonal = (kb + 1) * tk > qt * tq

    @pl.when(jnp.logical_not(diagonal))
    def _full():
        _attn_block(q2_scr, k_ref, vx_ref, m_scr, l_scr, acc_scr, s_scr, p_scr,
                    lambda sv, h: sv - fk[h:h + 1, :], row_bias)

    @pl.when(diagonal)
    def _masked():
        row = lax.broadcasted_iota(jnp.int32, (tq, 1), 0) + qt * tq
        key = lax.broadcasted_iota(jnp.int32, (1, tk), 1) + kb * tk
        causal = jnp.where(key <= row, 0.0, NEG)
        _attn_block(q2_scr, k_ref, vx_ref, m_scr, l_scr, acc_scr, s_scr, p_scr,
                    lambda sv, h: sv + (causal - fk[h:h + 1, :]), row_bias)

    @pl.when(kb == nkb - 1)
    def _finish():
        _attn_finish(o_ref, l_scr, acc_scr)


def _fox_schedule(batch, seq, tq, tk):
    rows = []
    for b in range(batch):
        for qt in range(seq // tq):
            nkb = -(-((qt + 1) * tq) // tk)
            for kb in range(nkb):
                rows.append((b, qt, kb, nkb))
    return [jnp.asarray(c, jnp.int32) for c in np.asarray(rows, np.int32).T]


def _fox_attention(q, fq, fk_t, k, vx, batch, seq, tq=256, tk=1024):
    t = q.shape[0]
    qt_per_b = seq // tq
    kb_per_b = seq // tk
    sched = _fox_schedule(batch, seq, tq, tk)

    def q_row(i, b, qt, kb, nkb):
        return (b[i] * qt_per_b + qt[i], 0)

    def kv_row(i, b, qt, kb, nkb):
        return (b[i] * kb_per_b + kb[i], 0)

    def fk_row(i, b, qt, kb, nkb):
        return (b[i], 0, kb[i])

    grid_spec = pltpu.PrefetchScalarGridSpec(
        num_scalar_prefetch=4,
        grid=(int(sched[0].shape[0]),),
        in_specs=[pl.BlockSpec((tq, HD), q_row),
                  pl.BlockSpec((tq, LANES), q_row),
                  pl.BlockSpec((tk, HD), kv_row),
                  pl.BlockSpec((tk, 2 * HD), kv_row),
                  pl.BlockSpec((1, N_HEADS, tk), fk_row)],
        out_specs=pl.BlockSpec((tq, HD), q_row),
        scratch_shapes=_attn_scratch(tq, tk) + [pltpu.VMEM((N_HEADS, tq, 1), F32)],
    )
    return pl.pallas_call(
        _fox_kernel,
        grid_spec=grid_spec,
        out_shape=jax.ShapeDtypeStruct((t, HD), BF16),
        compiler_params=_cparams(1),
        name="fox_attention",
    )(*sched, q, fq, k, vx, fk_t)


def _swiglu_partial(xn, wgu_ref, wd_ref, row_scale):
    tf = wd_ref.shape[-2]
    wgu = wgu_ref[(0,) * (len(wgu_ref.shape) - 2)]
    wd = wd_ref[(0,) * (len(wd_ref.shape) - 2)]
    gu = jnp.dot(xn, wgu, preferred_element_type=F32)
    g = gu[:, :tf]
    u = gu[:, tf:]
    a = g * jax.nn.sigmoid(g) * u
    if row_scale is not None:
        a = a * row_scale
    return jnp.dot(a.astype(BF16), wd, preferred_element_type=F32)


def _ffn_kernel(x_ref, g_ref, wgu_ref, wd_ref, o_ref, xn_scr, acc_scr):
    f = pl.program_id(1)

    @pl.when(f == 0)
    def _():
        xn_scr[...] = _rms(x_ref[...], g_ref[...]).astype(BF16)
        acc_scr[...] = jnp.zeros_like(acc_scr)

    acc_scr[...] += _swiglu_partial(xn_scr[...], wgu_ref, wd_ref, None)

    @pl.when(f == pl.num_programs(1) - 1)
    def _():
        o_ref[...] = x_ref[...] + acc_scr[...]


def _ffn(h, g, wgu, wd, tm=512):
    t, d = h.shape
    n_f, _, two_tf = wgu.shape
    tf = two_tf // 2
    return pl.pallas_call(
        _ffn_kernel,
        grid=(t // tm, n_f),
        in_specs=[pl.BlockSpec((tm, d), lambda i, f: (i, 0)),
                  pl.BlockSpec((1, d), lambda i, f: (0, 0)),
                  pl.BlockSpec((1, d, two_tf), lambda i, f: (f, 0, 0)),
                  pl.BlockSpec((tf, d), lambda i, f: (f, 0))],
        out_specs=pl.BlockSpec((tm, d), lambda i, f: (i, 0)),
        out_shape=jax.ShapeDtypeStruct((t, d), F32),
        scratch_shapes=[pltpu.VMEM((tm, d), BF16), pltpu.VMEM((tm, d), F32)],
        compiler_params=_cparams(2),
        name="ffn_dense",
    )(h, g, wgu, wd)


META_E1, META_E2, META_POS1, META_POS2, META_G1, META_G2 = range(6)
ROW_TILES = 8


def _to_row_tiles(ref, x):
    rows = x.shape[0]
    for s in range(ROW_TILES):
        ref[pl.ds(s, rows, stride=ROW_TILES), :] = x[:, s * LANES:(s + 1) * LANES]


def _from_row_tiles(ref, rows, s):
    return ref[pl.ds(s, rows, stride=ROW_TILES), :]


def _row_tile(ref, r):
    return ref.at[pl.ds(pl.multiple_of(r * ROW_TILES, ROW_TILES), ROW_TILES)]


def _moe_route_kernel(x_ref, g_ref, wr_ref, xn_ref, meta_ref, cnt_ref, carry_ref):
    i = pl.program_id(0)
    tm = x_ref.shape[0]

    @pl.when(i == 0)
    def _():
        carry_ref[...] = jnp.zeros_like(carry_ref)

    xn = _rms(x_ref[...], g_ref[...])
    _to_row_tiles(xn_ref, xn)
    logits = jnp.dot(xn, wr_ref[...], preferred_element_type=F32, precision=lax.Precision.HIGHEST)
    lane = lax.broadcasted_iota(jnp.int32, (tm, LANES), 1)
    lg = jnp.where(lane < N_EXPERTS, logits, -jnp.inf)
    m1 = jnp.max(lg, axis=1, keepdims=True)
    i1 = jnp.min(jnp.where(lg == m1, lane, LANES), axis=1, keepdims=True)
    lg2 = jnp.where(lane == i1, -jnp.inf, lg)
    m2 = jnp.max(lg2, axis=1, keepdims=True)
    i2 = jnp.min(jnp.where(lg2 == m2, lane, LANES), axis=1, keepdims=True)
    e2 = jnp.exp(m2 - m1)
    den = 1.0 + e2
    routed = jnp.where(jnp.logical_or(lane == i1, lane == i2), 1.0, 0.0)
    r_i = lax.broadcasted_iota(jnp.int32, (tm, tm), 0)
    c_i = lax.broadcasted_iota(jnp.int32, (tm, tm), 1)
    tri = jnp.where(c_i < r_i, 1.0, 0.0).astype(BF16)
    before = jnp.dot(tri, routed.astype(BF16), preferred_element_type=F32) + carry_ref[...]
    pos1 = jnp.sum(jnp.where(lane == i1, before, 0.0), axis=1, keepdims=True)
    pos2 = jnp.sum(jnp.where(lane == i2, before, 0.0), axis=1, keepdims=True)
    total = before[tm - 1:tm, :] + routed[tm - 1:tm, :]
    carry_ref[...] = total
    cnt_ref[...] = total
    meta = jnp.zeros((tm, LANES), F32)
    for col, val in ((META_E1, i1.astype(F32)), (META_E2, i2.astype(F32)), (META_POS1, pos1),
                     (META_POS2, pos2), (META_G1, 1.0 / den), (META_G2, e2 / den)):
        meta = jnp.where(lane == col, val, meta)
    meta_ref[...] = meta


def _moe_route(h, g, w_router, tm=512):
    t, d = h.shape
    return pl.pallas_call(
        _moe_route_kernel,
        grid=(t // tm,),
        in_specs=[pl.BlockSpec((tm, d), lambda i: (i, 0)),
                  pl.BlockSpec((1, d), lambda i: (0, 0)),
                  pl.BlockSpec((d, LANES), lambda i: (0, 0))],
        out_specs=[pl.BlockSpec((tm * ROW_TILES, LANES), lambda i: (i, 0)),
                   pl.BlockSpec((tm, LANES), lambda i: (i, 0)),
                   pl.BlockSpec((1, LANES), lambda i: (0, 0))],
        out_shape=[jax.ShapeDtypeStruct((t * ROW_TILES, LANES), F32),
                   jax.ShapeDtypeStruct((t, LANES), F32),
                   jax.ShapeDtypeStruct((1, LANES), F32)],
        scratch_shapes=[pltpu.VMEM((1, LANES), F32)],
        compiler_params=_cparams(1),
        name="moe_route",
    )(h, g, w_router)


def _moe_expert_kernel(src_s, tile_e_s, n_used_s, xn_hbm, wgu_ref, wd_ref, y_ref,
                       rows_scr, xs_scr, acc_scr, sem, *, n_f):
    i = pl.program_id(0)
    f = pl.program_id(1)
    tm = xs_scr.shape[0]
    used = i < n_used_s[0]
    next_used = i + 1 < n_used_s[0]
    slot = i % 2
    part = tm // n_f

    def whole_tile(s):
        return pltpu.make_async_copy(xn_hbm.at[pl.ds(0, tm * ROW_TILES)], rows_scr.at[s], sem.at[s])

    @pl.when(jnp.logical_and(i == 0, f == 0))
    def _first():
        def start(r, c):
            pltpu.make_async_copy(_row_tile(xn_hbm, src_s[r]), _row_tile(rows_scr.at[0], r),
                                  sem.at[0]).start()
            return c
        lax.fori_loop(0, tm, start, 0)

    @pl.when(jnp.logical_and(used, f == 0))
    def _unpack():
        whole_tile(slot).wait()
        for s in range(ROW_TILES):
            xs_scr[:, s * LANES:(s + 1) * LANES] = _from_row_tiles(rows_scr.at[slot], tm, s).astype(BF16)
        acc_scr[...] = jnp.zeros_like(acc_scr)

    def fetch_next():
        base = (i + 1) * tm + f * part
        for r in range(part):
            pltpu.make_async_copy(_row_tile(xn_hbm, src_s[base + r]),
                                  _row_tile(rows_scr.at[1 - slot], f * part + r),
                                  sem.at[1 - slot]).start()

    @pl.when(jnp.logical_and(used, next_used))
    def _compute_and_fetch():
        fetch_next()
        acc_scr[...] += _swiglu_partial(xs_scr[...], wgu_ref, wd_ref, None)

    @pl.when(jnp.logical_and(used, jnp.logical_not(next_used)))
    def _compute():
        acc_scr[...] += _swiglu_partial(xs_scr[...], wgu_ref, wd_ref, None)

    @pl.when(f == n_f - 1)
    def _store():
        @pl.when(used)
        def _():
            _to_row_tiles(y_ref, acc_scr[...])

        @pl.when(jnp.logical_not(used))
        def _():
            y_ref[...] = jnp.zeros_like(y_ref)


def _moe_experts(xn3, src, tile_e, n_used, wgu, wd, tm):
    n_rows = src.shape[0]
    n_e, n_f, d, two_tf = wgu.shape
    tf = two_tf // 2
    grid_spec = pltpu.PrefetchScalarGridSpec(
        num_scalar_prefetch=3,
        grid=(n_rows // tm, n_f),
        in_specs=[pl.BlockSpec(memory_space=pl.ANY),
                  pl.BlockSpec((1, 1, d, two_tf), lambda i, f, src, te, nu: (te[i], f, 0, 0)),
                  pl.BlockSpec((1, tf, d), lambda i, f, src, te, nu: (te[i], f, 0))],
        out_specs=pl.BlockSpec((tm * ROW_TILES, LANES), lambda i, f, src, te, nu: (i, 0)),
        scratch_shapes=[pltpu.VMEM((2, tm * ROW_TILES, LANES), F32),
                        pltpu.VMEM((tm, d), BF16),
                        pltpu.VMEM((tm, d), F32),
                        pltpu.SemaphoreType.DMA((2,))],
    )
    return pl.pallas_call(
        functools.partial(_moe_expert_kernel, n_f=n_f),
        grid_spec=grid_spec,
        out_shape=jax.ShapeDtypeStruct((n_rows * ROW_TILES, LANES), F32),
        compiler_params=_cparams(2),
        name="moe_experts",
    )(src, tile_e, n_used, xn3, wgu, wd)


def _moe_combine_kernel(d1_s, d2_s, x_ref, meta_ref, gf_ref, y_hbm, o_ref, y_scr, sem, *, final_norm):
    i = pl.program_id(0)
    tm = x_ref.shape[0]
    slot = i % 2

    def fetch(tile, s):
        def start(r, c):
            pltpu.make_async_copy(_row_tile(y_hbm, d1_s[tile * tm + r]), _row_tile(y_scr.at[s, 0], r),
                                  sem.at[s]).start()
            pltpu.make_async_copy(_row_tile(y_hbm, d2_s[tile * tm + r]), _row_tile(y_scr.at[s, 1], r),
                                  sem.at[s]).start()
            return c
        lax.fori_loop(0, tm, start, 0)

    @pl.when(i == 0)
    def _first():
        fetch(0, 0)

    @pl.when(i + 1 < pl.num_programs(0))
    def _next():
        fetch(i + 1, 1 - slot)

    for which in range(2):
        pltpu.make_async_copy(y_hbm.at[pl.ds(0, tm * ROW_TILES)], y_scr.at[slot, which],
                              sem.at[slot]).wait()
    meta = meta_ref[...]
    g1 = meta[:, META_G1:META_G1 + 1]
    g2 = meta[:, META_G2:META_G2 + 1]
    for s in range(ROW_TILES):
        cols = slice(s * LANES, (s + 1) * LANES)
        o_ref[:, cols] = (x_ref[:, cols] + g1 * _from_row_tiles(y_scr.at[slot, 0], tm, s)
                          + g2 * _from_row_tiles(y_scr.at[slot, 1], tm, s))
    if final_norm:
        o_ref[...] = _rms(o_ref[...], gf_ref[...])


def _moe_combine(h, meta, g_final, y3, dest1, dest2, final_norm, tm=512):
    t, d = h.shape
    grid_spec = pltpu.PrefetchScalarGridSpec(
        num_scalar_prefetch=2,
        grid=(t // tm,),
        in_specs=[pl.BlockSpec((tm, d), lambda i, d1, d2: (i, 0)),
                  pl.BlockSpec((tm, LANES), lambda i, d1, d2: (i, 0)),
                  pl.BlockSpec((1, d), lambda i, d1, d2: (0, 0)),
                  pl.BlockSpec(memory_space=pl.ANY)],
        out_specs=pl.BlockSpec((tm, d), lambda i, d1, d2: (i, 0)),
        scratch_shapes=[pltpu.VMEM((2, 2, tm * ROW_TILES, LANES), F32),
                        pltpu.SemaphoreType.DMA((2,))],
    )
    return pl.pallas_call(
        functools.partial(_moe_combine_kernel, final_norm=final_norm),
        grid_spec=grid_spec,
        out_shape=jax.ShapeDtypeStruct((t, d), F32),
        compiler_params=_cparams(1),
        name="moe_combine",
    )(dest1, dest2, h, meta, g_final, y3)


def _moe(h, g, w_router, wgu, wd, g_final, final_norm, tm=512):
    t, d = h.shape
    xn3, meta, cnt = _moe_route(h, g, w_router)
    counts = cnt[0, :N_EXPERTS].astype(jnp.int32)
    padded = (counts + tm - 1) // tm * tm
    ends = jnp.cumsum(padded)
    starts = ends - padded
    idx = meta[:, :META_G1].astype(jnp.int32)
    experts = jnp.arange(N_EXPERTS, dtype=jnp.int32)

    def group_row(e, pos):
        return pos + jnp.sum(jnp.where(e[:, None] == experts[None, :], starts[None, :], 0), axis=1)

    dest1 = group_row(idx[:, META_E1], idx[:, META_POS1])
    dest2 = group_row(idx[:, META_E2], idx[:, META_POS2])
    n_rows = 2 * t + N_EXPERTS * tm
    tok = jnp.arange(t, dtype=jnp.int32)
    src = jnp.zeros((n_rows,), jnp.int32).at[jnp.concatenate([dest1, dest2])].set(
        jnp.concatenate([tok, tok]), unique_indices=True)
    tile_start = jnp.arange(n_rows // tm, dtype=jnp.int32) * tm
    tile_e = jnp.minimum(jnp.sum((ends[None, :] <= tile_start[:, None]).astype(jnp.int32), axis=1),
                         N_EXPERTS - 1)
    n_used = (ends[-1:] // tm).astype(jnp.int32)
    y3 = _moe_experts(xn3, src, tile_e, n_used, wgu, wd, tm)
    return _moe_combine(h, meta, g_final, y3, dest1, dest2, final_norm)


def _split_gate_up(w, n_f):
    d, two_f = w.shape[-2:]
    tf = two_f // 2 // n_f
    g = w[..., :two_f // 2].reshape(*w.shape[:-1], n_f, tf)
    u = w[..., two_f // 2:].reshape(*w.shape[:-1], n_f, tf)
    gu = jnp.concatenate([g, u], axis=-1)
    return jnp.moveaxis(gu, -2, -3).astype(BF16)


def _rope_tables(seq):
    pos = jnp.arange(seq, dtype=F32)
    inv_freq = 1.0 / (ROPE_THETA ** (jnp.arange(0, HEAD_DIM, 2, dtype=F32) / HEAD_DIM))
    ang = pos[:, None] * inv_freq[None, :]
    cos, sin = jnp.cos(ang), jnp.sin(ang)
    cos_t = jnp.tile(cos, (1, LANES // (HEAD_DIM // 2)))
    sin_t = jnp.tile(jnp.concatenate([-sin, sin], axis=1), (1, LANES // HEAD_DIM))
    return cos_t, sin_t


def kernel(x, a_norm, a_w_in, a_w_out, kv_norm, w_kv, b_f, b_norm, b_w_q, b_w_out, ffn_norm,
           dense_w_gate_up, dense_w_down, moe_router, moe_w_gate_up, moe_w_down, final_norm):
    batch, seq, d = x.shape
    depth = ffn_norm.shape[0]
    n_a = a_norm.shape[0]
    n_f = 2
    cos_t, sin_t = _rope_tables(seq)
    idx_w = IDX_HEADS * IDX_DIM

    h = x.reshape(batch * seq, d)
    k_sh = vx_sh = f_sh = fk_t = None
    for i in range(depth):
        if i < n_a:
            w = a_w_in[i]
            ki_w = w[:, 3 * HD + idx_w:3 * HD + idx_w + IDX_DIM]
            wi_w = w[:, 3 * HD + idx_w + IDX_DIM:]
            w_p = jnp.concatenate(
                [w[:, :3 * HD + idx_w], ki_w, ki_w, wi_w,
                 jnp.zeros((d, LANES - IDX_HEADS), w.dtype)], axis=1).astype(BF16)
            q, k, vx, qi, ki, wi = _a_proj(h, a_norm[i][None, :], w_p, cos_t, sin_t, seq)
            o = _dsa_attention(qi, wi, ki, q, k, vx, batch, seq)
            h = _out_proj(o, a_w_out[i].astype(BF16), h)
        else:
            j = i - n_a
            if k_sh is None:
                w_p = jnp.concatenate(
                    [w_kv, jnp.zeros((d, LANES - N_HEADS), w_kv.dtype)], axis=1).astype(BF16)
                b_p = jnp.concatenate([b_f, jnp.zeros((LANES - N_HEADS,), b_f.dtype)])[None, :]
                k_sh, vx_sh, f_sh = _kv_proj(h, kv_norm[None, :], w_p, b_p, seq)
                fk_t = f_sh[:, :N_HEADS].reshape(batch, seq, N_HEADS).transpose(0, 2, 1)
            q = _q_proj(h, b_norm[j][None, :], b_w_q[j].astype(BF16))
            o = _fox_attention(q, f_sh, fk_t, k_sh, vx_sh, batch, seq)
            h = _out_proj(o, b_w_out[j].astype(BF16), h)
        if i % 2 == 0:
            wd = dense_w_down[i // 2].astype(BF16)
            h = _ffn(h, ffn_norm[i][None, :], _split_gate_up(dense_w_gate_up[i // 2], n_f), wd)
        else:
            e = i // 2
            wr = jnp.concatenate(
                [moe_router[e], jnp.zeros((d, LANES - N_EXPERTS), moe_router.dtype)], axis=1)
            wd = moe_w_down[e].astype(BF16).reshape(N_EXPERTS, -1, d)
            last = i == depth - 1
            h = _moe(h, ffn_norm[i][None, :], wr, _split_gate_up(moe_w_gate_up[e], n_f), wd,
                     final_norm[None, :], last)
    if depth % 2 == 1:
        raise NotImplementedError("final norm is fused into the last expert mixer")
    return h.reshape(batch, seq, d)
```

```python
import functools
import math

import numpy as np
import jax
import jax.numpy as jnp
from jax import lax
from jax.experimental import pallas as pl
from jax.experimental.pallas import tpu as pltpu

N_HEADS = 16
HEAD_DIM = 64
IDX_HEADS = 8
IDX_DIM = 64
TOPK_MAX = 256
CHUNK = 64
ROPE_THETA = 10000.0
N_EXPERTS = 8
RMS_EPS = 1e-6

LANES = 128
HD = N_HEADS * HEAD_DIM
N_PAIRS = N_HEADS // 2
NEG = -1e30
LOG2E = math.log2(math.e)
VMEM_LIMIT = 52 * 1024 * 1024

F32 = jnp.float32
BF16 = jnp.bfloat16


def _cparams(n_axes):
    return pltpu.CompilerParams(dimension_semantics=("arbitrary",) * n_axes,
                                vmem_limit_bytes=VMEM_LIMIT)


def _rms(x, g):
    return x * lax.rsqrt(jnp.mean(x * x, axis=-1, keepdims=True) + RMS_EPS) * g


def _half_mask(shape):
    return lax.broadcasted_iota(jnp.int32, shape, len(shape) - 1) < HEAD_DIM


def _nt_dot(a, b):
    return lax.dot_general(a, b, (((1,), (1,)), ((), ())), preferred_element_type=F32)


def _store_values_with_ones(vx_ref, p, c0):
    ones = jnp.ones((p.shape[0], LANES), BF16)
    for j in range(p.shape[1] // LANES):
        pair = (c0 // LANES) + j
        vx_ref[:, 2 * pair * LANES:(2 * pair + 1) * LANES] = p[:, j * LANES:(j + 1) * LANES].astype(BF16)
        vx_ref[:, (2 * pair + 1) * LANES:(2 * pair + 2) * LANES] = ones


def _rope_chunk(p, cos, sin_signed):
    fwd = pltpu.roll(p, LANES - 32, 1)
    bwd = pltpu.roll(p, 32, 1)
    lane = lax.broadcasted_iota(jnp.int32, p.shape, 1)
    partner = jnp.where((lane % HEAD_DIM) < HEAD_DIM // 2, fwd, bwd)
    return p * cos + partner * sin_signed


def _a_proj_kernel(x_ref, g_ref, w_ref, cos_ref, sin_ref,
                   q_ref, k_ref, vx_ref, qi_ref, ki_ref, wi_ref):
    xn = _rms(x_ref[...], g_ref[...]).astype(BF16)
    cos = cos_ref[...]
    sin = sin_ref[...]
    seg = 512

    def proj(c0, width):
        return jnp.dot(xn, w_ref[:, c0:c0 + width], preferred_element_type=F32)

    def rope_store(p, out_ref, o0, scale):
        for j in range(p.shape[1] // LANES):
            r = _rope_chunk(p[:, j * LANES:(j + 1) * LANES], cos, sin)
            if scale != 1.0:
                r = r * scale
            out_ref[:, o0 + j * LANES:o0 + (j + 1) * LANES] = r.astype(out_ref.dtype)

    for s in range(HD // seg):
        rope_store(proj(s * seg, seg), q_ref, s * seg, HEAD_DIM ** -0.5 * LOG2E)
    for s in range(HD // seg):
        rope_store(proj(HD + s * seg, seg), k_ref, s * seg, 1.0)
    for s in range(HD // seg):
        _store_values_with_ones(vx_ref, proj(2 * HD + s * seg, seg), s * seg)
    rope_store(proj(3 * HD, IDX_HEADS * IDX_DIM), qi_ref, 0, 1.0)
    tail = proj(3 * HD + IDX_HEADS * IDX_DIM, 2 * LANES)
    rope_store(tail[:, :LANES], ki_ref, 0, 1.0)
    wi_ref[...] = tail[:, LANES:] * (IDX_HEADS ** -0.5)


def _a_proj(h, g, w, cos_t, sin_t, seq, tm=512):
    t, d = h.shape
    n_w = w.shape[1]
    tiles_per_seq = seq // tm
    row = lambda i: (i, 0)
    tab = lambda i: (i % tiles_per_seq, 0)
    const = lambda i: (0, 0)
    return pl.pallas_call(
        _a_proj_kernel,
        grid=(t // tm,),
        in_specs=[pl.BlockSpec((tm, d), row),
                  pl.BlockSpec((1, d), const),
                  pl.BlockSpec((d, n_w), const),
                  pl.BlockSpec((tm, LANES), tab),
                  pl.BlockSpec((tm, LANES), tab)],
        out_specs=[pl.BlockSpec((tm, HD), row),
                   pl.BlockSpec((tm, HD), row),
                   pl.BlockSpec((tm, 2 * HD), row),
                   pl.BlockSpec((tm, IDX_HEADS * IDX_DIM), row),
                   pl.BlockSpec((tm, LANES), row),
                   pl.BlockSpec((tm, LANES), row)],
        out_shape=[jax.ShapeDtypeStruct((t, HD), BF16),
                   jax.ShapeDtypeStruct((t, HD), BF16),
                   jax.ShapeDtypeStruct((t, 2 * HD), BF16),
                   jax.ShapeDtypeStruct((t, IDX_HEADS * IDX_DIM), BF16),
                   jax.ShapeDtypeStruct((t, LANES), BF16),
                   jax.ShapeDtypeStruct((t, LANES), F32)],
        compiler_params=_cparams(1),
        name="a_proj",
    )(h, g, w, cos_t, sin_t)


def _kv_proj_kernel(x_ref, g_ref, w_ref, bf_ref, k_ref, vx_ref, cum_ref, carry_ref, *, tiles_per_seq):
    i = pl.program_id(0)
    tm = x_ref.shape[0]

    @pl.when(i % tiles_per_seq == 0)
    def _():
        carry_ref[...] = jnp.zeros_like(carry_ref)

    xn = _rms(x_ref[...], g_ref[...]).astype(BF16)
    seg = 512
    for s in range(HD // seg):
        k_ref[:, s * seg:(s + 1) * seg] = jnp.dot(
            xn, w_ref[:, s * seg:(s + 1) * seg], preferred_element_type=F32).astype(BF16)
    for s in range(HD // seg):
        _store_values_with_ones(
            vx_ref, jnp.dot(xn, w_ref[:, HD + s * seg:HD + (s + 1) * seg], preferred_element_type=F32),
            s * seg)
    f_logit = jnp.dot(xn, w_ref[:, 2 * HD:2 * HD + LANES], preferred_element_type=F32)
    log_f = jax.nn.log_sigmoid(f_logit + bf_ref[...])
    r_i = lax.broadcasted_iota(jnp.int32, (tm, tm), 0)
    c_i = lax.broadcasted_iota(jnp.int32, (tm, tm), 1)
    tri = jnp.where(c_i <= r_i, 1.0, 0.0).astype(BF16)
    x1 = log_f.astype(BF16)
    rem = log_f - x1.astype(F32)
    x2 = rem.astype(BF16)
    x3 = (rem - x2.astype(F32)).astype(BF16)
    cum = (jnp.dot(tri, x1, preferred_element_type=F32)
           + jnp.dot(tri, x2, preferred_element_type=F32)
           + jnp.dot(tri, x3, preferred_element_type=F32)) + carry_ref[...]
    cum_ref[...] = cum * LOG2E
    carry_ref[...] = cum[tm - 1:tm, :]


def _kv_proj(h, g, w, b_f, seq, tm=512):
    t, d = h.shape
    row = lambda i: (i, 0)
    const = lambda i: (0, 0)
    return pl.pallas_call(
        functools.partial(_kv_proj_kernel, tiles_per_seq=seq // tm),
        grid=(t // tm,),
        in_specs=[pl.BlockSpec((tm, d), row),
                  pl.BlockSpec((1, d), const),
                  pl.BlockSpec((d, w.shape[1]), const),
                  pl.BlockSpec((1, LANES), const)],
        out_specs=[pl.BlockSpec((tm, HD), row),
                   pl.BlockSpec((tm, 2 * HD), row),
                   pl.BlockSpec((tm, LANES), row)],
        out_shape=[jax.ShapeDtypeStruct((t, HD), BF16),
                   jax.ShapeDtypeStruct((t, 2 * HD), BF16),
                   jax.ShapeDtypeStruct((t, LANES), F32)],
        scratch_shapes=[pltpu.VMEM((1, LANES), F32)],
        compiler_params=_cparams(1),
        name="kv_proj",
    )(h, g, w, b_f)


def _q_proj_kernel(x_ref, g_ref, w_ref, q_ref):
    xn = _rms(x_ref[...], g_ref[...]).astype(BF16)
    seg = 512
    for s in range(HD // seg):
        p = jnp.dot(xn, w_ref[:, s * seg:(s + 1) * seg], preferred_element_type=F32)
        q_ref[:, s * seg:(s + 1) * seg] = (p * (HEAD_DIM ** -0.5 * LOG2E)).astype(BF16)


def _q_proj(h, g, w, tm=512):
    t, d = h.shape
    row = lambda i: (i, 0)
    const = lambda i: (0, 0)
    return pl.pallas_call(
        _q_proj_kernel,
        grid=(t // tm,),
        in_specs=[pl.BlockSpec((tm, d), row),
                  pl.BlockSpec((1, d), const),
                  pl.BlockSpec((d, HD), const)],
        out_specs=pl.BlockSpec((tm, HD), row),
        out_shape=jax.ShapeDtypeStruct((t, HD), BF16),
        compiler_params=_cparams(1),
        name="q_proj",
    )(h, g, w)


def _out_proj_kernel(o_ref, w_ref, h_ref, out_ref):
    out_ref[...] = h_ref[...] + jnp.dot(o_ref[...], w_ref[...], preferred_element_type=F32)


def _out_proj(o, w, h, tm=512):
    t, d = h.shape
    row = lambda i: (i, 0)
    const = lambda i: (0, 0)
    return pl.pallas_call(
        _out_proj_kernel,
        grid=(t // tm,),
        in_specs=[pl.BlockSpec((tm, HD), row),
                  pl.BlockSpec((HD, d), const),
                  pl.BlockSpec((tm, d), row)],
        out_specs=pl.BlockSpec((tm, d), row),
        out_shape=jax.ShapeDtypeStruct((t, d), F32),
        compiler_params=_cparams(1),
        name="out_proj",
    )(o, w, h)


def _stack_pair_queries(q_ref, q2_scr):
    tq = q_ref.shape[0]
    for j in range(N_PAIRS):
        qp = q_ref[:, j * LANES:(j + 1) * LANES]
        first = _half_mask(qp.shape)
        zero = jnp.zeros_like(qp)
        q2_scr[j, 0:tq, :] = jnp.where(first, qp, zero)
        q2_scr[j, tq:2 * tq, :] = jnp.where(first, zero, qp)


def _attn_init(m_scr, l_scr, acc_scr):
    m_scr[...] = jnp.full(m_scr.shape, NEG, F32)
    l_scr[...] = jnp.zeros_like(l_scr)
    acc_scr[...] = jnp.zeros_like(acc_scr)


def _attn_block(q2_scr, k_ref, vx_ref, m_scr, l_scr, acc_scr, s_scr, p_scr, bias_fn,
                row_bias_fn=None):
    tq = s_scr.shape[0] // 2
    for j in range(N_PAIRS):
        kp = k_ref[:, j * LANES:(j + 1) * LANES]
        sv = _nt_dot(q2_scr[j], kp)
        alphas = []
        for half in range(2):
            rows = slice(half * tq, (half + 1) * tq)
            s_scr[rows, :] = bias_fn(sv[rows], 2 * j + half)
            m_prev = m_scr[j, rows, :]
            m_blk = jnp.max(s_scr[rows, :], axis=1, keepdims=True)
            if row_bias_fn is not None:
                m_blk = m_blk + row_bias_fn(2 * j + half)
            m_new = jnp.maximum(m_prev, m_blk)
            m_scr[j, rows, :] = m_new
            alphas.append(jnp.exp2(m_prev - m_new))
            shift = m_new if row_bias_fn is None else m_new - row_bias_fn(2 * j + half)
            p_scr[rows, :] = jnp.exp2(s_scr[rows, :] - shift).astype(BF16)
        pv = jnp.dot(p_scr[...], vx_ref[:, 2 * j * LANES:2 * (j + 1) * LANES],
                     preferred_element_type=F32)
        for half in range(2):
            rows = slice(half * tq, (half + 1) * tq)
            acc_scr[j, rows, :] = alphas[half] * acc_scr[j, rows, :] + pv[rows, :LANES]
            l_scr[j, rows, :] = alphas[half] * l_scr[j, rows, :] + pv[rows, LANES:]


def _attn_finish(o_ref, l_scr, acc_scr):
    tq = o_ref.shape[0]
    for j in range(N_PAIRS):
        o2 = acc_scr[j] / l_scr[j]
        o_ref[:, j * LANES:(j + 1) * LANES] = jnp.where(
            _half_mask((tq, LANES)), o2[0:tq], o2[tq:2 * tq]).astype(o_ref.dtype)


def _attn_scratch(tq, tk):
    return [pltpu.VMEM((N_PAIRS, 2 * tq, LANES), BF16),
            pltpu.VMEM((N_PAIRS, 2 * tq, 1), F32),
            pltpu.VMEM((N_PAIRS, 2 * tq, LANES), F32),
            pltpu.VMEM((N_PAIRS, 2 * tq, LANES), F32),
            pltpu.VMEM((2 * tq, tk), F32),
            pltpu.VMEM((2 * tq, tk), BF16)]


def _ordered_key_to_f32(key):
    return lax.bitcast_convert_type(key ^ ((key >> 31) & jnp.int32(0x7FFFFFFF)), F32)


def _dsa_kernel(b_s, qt_s, ph_s, kb_s, nkb_s,
                qi_ref, wi_ref, ki_ref, q_ref, k_ref, vx_ref, o_ref,
                score_scr, q2_scr, m_scr, l_scr, acc_scr, s_scr, p_scr, *, k_sel):
    step = pl.program_id(0)
    qt = qt_s[step]
    phase = ph_s[step]
    kb = kb_s[step]
    nkb = nkb_s[step]
    tq = q_ref.shape[0]
    tk = k_ref.shape[0]

    row = lax.broadcasted_iota(jnp.int32, (tq, 1), 0) + qt * tq
    limit = (row // CHUNK + 1) * CHUNK

    @pl.when(phase == 0)
    def _index():
        kk = ki_ref[...]
        w = wi_ref[...]
        acc = jnp.zeros((tq, tk), F32)
        for j in range(IDX_HEADS // 2):
            qp = qi_ref[:, j * LANES:(j + 1) * LANES]
            first = _half_mask(qp.shape)
            zero = jnp.zeros_like(qp)
            for half in range(2):
                h = 2 * j + half
                qh = jnp.where(first, qp, zero) if half == 0 else jnp.where(first, zero, qp)
                sc = _nt_dot(qh, kk)
                acc = acc + jnp.maximum(sc, 0.0) * w[:, h:h + 1]
        key = lax.broadcasted_iota(jnp.int32, (1, tk), 1) + kb * tk
        sc = jnp.where(key < limit, acc, -jnp.inf)
        score_scr[kb] = sc

    @pl.when(jnp.logical_and(phase == 0, kb == nkb - 1))
    def _select():
        band = 128
        one = jnp.ones((band, LANES), F32)
        zero = jnp.zeros((band, LANES), F32)

        def count(cand, strict):
            c_full = jnp.broadcast_to(cand, (tq, LANES))
            accs = []
            for r0 in range(0, tq, band):
                c_b = c_full[r0:r0 + band]

                def body(j, acc, r0=r0, c_b=c_b):
                    blk = score_scr[j, r0:r0 + band, :]
                    for c in range(tk // LANES):
                        part = blk[:, c * LANES:(c + 1) * LANES]
                        hit = (part > c_b) if strict else (part >= c_b)
                        acc = acc + jnp.where(hit, one, zero)
                    return acc

                accs.append(lax.fori_loop(0, nkb, body, zero))
            return jnp.sum(jnp.concatenate(accs, axis=0), axis=1, keepdims=True)

        def bit_body(i, r):
            cand_u = r | jnp.left_shift(jnp.int32(1), 31 - i)
            cnt = count(_ordered_key_to_f32(cand_u ^ jnp.int32(-2 ** 31)), False)
            return jnp.where(cnt >= k_sel, cand_u, r)

        r = lax.fori_loop(0, 32, bit_body, jnp.zeros((tq, 1), jnp.int32))
        few = limit <= k_sel
        thr = jnp.where(few, -jnp.inf, _ordered_key_to_f32(r ^ jnp.int32(-2 ** 31)))
        need = k_sel - count(thr, True)

        sub = 256
        r_i = lax.broadcasted_iota(jnp.int32, (sub, sub), 0)
        c_i = lax.broadcasted_iota(jnp.int32, (sub, sub), 1)
        tri = jnp.where(r_i <= c_i, 1.0, 0.0).astype(BF16)

        def sel_body(j, carry):
            for c in range(tk // sub):
                cols = slice(c * sub, (c + 1) * sub)
                blk = score_scr[j, :, cols]
                key = lax.broadcasted_iota(jnp.int32, (1, sub), 1) + (j * tk + c * sub)
                eq = blk == thr
                rank = carry + jnp.dot(jnp.where(eq, 1.0, 0.0).astype(BF16), tri,
                                       preferred_element_type=F32)
                bias = jnp.where(eq, jnp.where(rank <= need, 0.0, NEG),
                                 jnp.where(blk > thr, 0.0, NEG))
                score_scr[j, :, cols] = jnp.where(key < limit, bias, NEG)
                carry = rank[:, sub - 1:sub]
            return carry

        lax.fori_loop(0, nkb, sel_body, jnp.zeros((tq, 1), F32))

    @pl.when(jnp.logical_and(phase == 1, kb == 0))
    def _init():
        _stack_pair_queries(q_ref, q2_scr)
        _attn_init(m_scr, l_scr, acc_scr)

    @pl.when(phase == 1)
    def _attend():
        _attn_block(q2_scr, k_ref, vx_ref, m_scr, l_scr, acc_scr, s_scr, p_scr,
                    lambda sv, h: sv + score_scr[kb])

    @pl.when(jnp.logical_and(phase == 1, kb == nkb - 1))
    def _finish():
        _attn_finish(o_ref, l_scr, acc_scr)


def _dsa_schedule(batch, seq, tq, tk):
    rows = []
    for b in range(batch):
        for qt in range(seq // tq):
            nkb = -(-((qt + 1) * tq) // tk)
            for phase in range(2):
                for kb in range(nkb):
                    rows.append((b, qt, phase, kb, nkb))
    return [jnp.asarray(c, jnp.int32) for c in np.asarray(rows, np.int32).T]


def _dsa_attention(qi, wi, ki, q, k, vx, batch, seq, tq=256, tk=1024):
    t = q.shape[0]
    qt_per_b = seq // tq
    kb_per_b = seq // tk
    sched = _dsa_schedule(batch, seq, tq, tk)
    k_sel = min(TOPK_MAX, seq // 4)

    def q_row(i, b, qt, ph, kb, nkb):
        return (b[i] * qt_per_b + qt[i], 0)

    def ki_row(i, b, qt, ph, kb, nkb):
        return (b[i] * kb_per_b + jnp.where(ph[i] == 0, kb[i], nkb[i] - 1), 0)

    def kv_row(i, b, qt, ph, kb, nkb):
        return (b[i] * kb_per_b + jnp.where(ph[i] == 0, 0, kb[i]), 0)

    grid_spec = pltpu.PrefetchScalarGridSpec(
        num_scalar_prefetch=5,
        grid=(int(sched[0].shape[0]),),
        in_specs=[pl.BlockSpec((tq, IDX_HEADS * IDX_DIM), q_row),
                  pl.BlockSpec((tq, LANES), q_row),
                  pl.BlockSpec((tk, LANES), ki_row),
                  pl.BlockSpec((tq, HD), q_row),
                  pl.BlockSpec((tk, HD), kv_row),
                  pl.BlockSpec((tk, 2 * HD), kv_row)],
        out_specs=pl.BlockSpec((tq, HD), q_row),
        scratch_shapes=[pltpu.VMEM((kb_per_b, tq, tk), F32)] + _attn_scratch(tq, tk),
    )
    return pl.pallas_call(
        functools.partial(_dsa_kernel, k_sel=k_sel),
        grid_spec=grid_spec,
        out_shape=jax.ShapeDtypeStruct((t, HD), BF16),
        compiler_params=_cparams(1),
        name="dsa_attention",
    )(*sched, qi, wi, ki, q, k, vx)


def _fox_kernel(b_s, qt_s, kb_s, nkb_s, q_ref, fq_ref, k_ref, vx_ref, fk_ref, o_ref,
                q2_scr, m_scr, l_scr, acc_scr, s_scr, p_scr, fq_scr):
    step = pl.program_id(0)
    qt = qt_s[step]
    kb = kb_s[step]
    nkb = nkb_s[step]
    tq = q_ref.shape[0]
    tk = k_ref.shape[0]

    @pl.when(kb == 0)
    def _init():
        _stack_pair_queries(q_ref, q2_scr)
        _attn_init(m_scr, l_scr, acc_scr)
        fq = fq_ref[...]
        for h in range(N_HEADS):
            fq_scr[h] = fq[:, h:h + 1]

    fk = fk_ref[0]
    row_bias = lambda h: fq_scr[h]
    diagonal = (kb + 1) * tk > qt * tq

    @pl.when(jnp.logical_not(diagonal))
    def _full():
        _attn_block(q2_scr, k_ref, vx_ref, m_scr, l_scr, acc_scr, s_scr, p_scr,
                    lambda sv, h: sv - fk[h:h + 1, :], row_bias)

    @pl.when(diagonal)
    def _masked():
        row = lax.broadcasted_iota(jnp.int32, (tq, 1), 0) + qt * tq
        key = lax.broadcasted_iota(jnp.int32, (1, tk), 1) + kb * tk
        causal = jnp.where(key <= row, 0.0, NEG)
        _attn_block(q2_scr, k_ref, vx_ref, m_scr, l_scr, acc_scr, s_scr, p_scr,
                    lambda sv, h: sv + (causal - fk[h:h + 1, :]), row_bias)

    @pl.when(kb == nkb - 1)
    def _finish():
        _attn_finish(o_ref, l_scr, acc_scr)


def _fox_schedule(batch, seq, tq, tk):
    rows = []
    for b in range(batch):
        for qt in range(seq // tq):
            nkb = -(-((qt + 1) * tq) // tk)
            for kb in range(nkb):
                rows.append((b, qt, kb, nkb))
    return [jnp.asarray(c, jnp.int32) for c in np.asarray(rows, np.int32).T]


def _fox_attention(q, fq, fk_t, k, vx, batch, seq, tq=256, tk=1024):
    t = q.shape[0]
    qt_per_b = seq // tq
    kb_per_b = seq // tk
    sched = _fox_schedule(batch, seq, tq, tk)

    def q_row(i, b, qt, kb, nkb):
        return (b[i] * qt_per_b + qt[i], 0)

    def kv_row(i, b, qt, kb, nkb):
        return (b[i] * kb_per_b + kb[i], 0)

    def fk_row(i, b, qt, kb, nkb):
        return (b[i], 0, kb[i])

    grid_spec = pltpu.PrefetchScalarGridSpec(
        num_scalar_prefetch=4,
        grid=(int(sched[0].shape[0]),),
        in_specs=[pl.BlockSpec((tq, HD), q_row),
                  pl.BlockSpec((tq, LANES), q_row),
                  pl.BlockSpec((tk, HD), kv_row),
                  pl.BlockSpec((tk, 2 * HD), kv_row),
                  pl.BlockSpec((1, N_HEADS, tk), fk_row)],
        out_specs=pl.BlockSpec((tq, HD), q_row),
        scratch_shapes=_attn_scratch(tq, tk) + [pltpu.VMEM((N_HEADS, tq, 1), F32)],
    )
    return pl.pallas_call(
        _fox_kernel,
        grid_spec=grid_spec,
        out_shape=jax.ShapeDtypeStruct((t, HD), BF16),
        compiler_params=_cparams(1),
        name="fox_attention",
    )(*sched, q, fq, k, vx, fk_t)


N_FF_TILES = 2


def _swiglu_partial(xn, wg_ref, wu_ref, wd_ref):
    lead = (0,) * (len(wg_ref.shape) - 2)
    g = jnp.dot(xn, wg_ref[lead], preferred_element_type=F32)
    u = jnp.dot(xn, wu_ref[lead], preferred_element_type=F32)
    a = g * jax.nn.sigmoid(g) * u
    return jnp.dot(a.astype(BF16), wd_ref[lead], preferred_element_type=F32)


def _ffn_kernel(x_ref, g_ref, wg_ref, wu_ref, wd_ref, o_ref, xn_scr, acc_scr):
    f = pl.program_id(1)

    @pl.when(f == 0)
    def _():
        xn_scr[...] = _rms(x_ref[...], g_ref[...]).astype(BF16)
        acc_scr[...] = jnp.zeros_like(acc_scr)

    acc_scr[...] += _swiglu_partial(xn_scr[...], wg_ref, wu_ref, wd_ref)

    @pl.when(f == pl.num_programs(1) - 1)
    def _():
        o_ref[...] = x_ref[...] + acc_scr[...]


def _ffn(h, g, wgu, wd, layer, tm=512):
    t, d = h.shape
    n_f = N_FF_TILES
    tf = wd.shape[1] // n_f
    return pl.pallas_call(
        _ffn_kernel,
        grid=(t // tm, n_f),
        in_specs=[pl.BlockSpec((tm, d), lambda i, f: (i, 0)),
                  pl.BlockSpec((1, d), lambda i, f: (0, 0)),
                  pl.BlockSpec((1, d, tf), lambda i, f: (layer, 0, f)),
                  pl.BlockSpec((1, d, tf), lambda i, f: (layer, 0, n_f + f)),
                  pl.BlockSpec((1, tf, d), lambda i, f: (layer, f, 0))],
        out_specs=pl.BlockSpec((tm, d), lambda i, f: (i, 0)),
        out_shape=jax.ShapeDtypeStruct((t, d), F32),
        scratch_shapes=[pltpu.VMEM((tm, d), BF16), pltpu.VMEM((tm, d), F32)],
        compiler_params=_cparams(2),
        name="ffn_dense",
    )(h, g, wgu, wgu, wd)


META_E1, META_E2, META_POS1, META_POS2, META_G1, META_G2 = range(6)
ROW_TILES = 8


def _to_row_tiles(ref, x):
    rows = x.shape[0]
    for s in range(ROW_TILES):
        ref[pl.ds(s, rows, stride=ROW_TILES), :] = x[:, s * LANES:(s + 1) * LANES]


def _from_row_tiles(ref, rows, s):
    return ref[pl.ds(s, rows, stride=ROW_TILES), :]


def _row_tile(ref, r):
    return ref.at[pl.ds(pl.multiple_of(r * ROW_TILES, ROW_TILES), ROW_TILES)]


def _moe_route_kernel(x_ref, g_ref, wr_ref, xn_ref, meta_ref, cnt_ref, carry_ref):
    i = pl.program_id(0)
    tm = x_ref.shape[0]

    @pl.when(i == 0)
    def _():
        carry_ref[...] = jnp.zeros_like(carry_ref)

    xn = _rms(x_ref[...], g_ref[...])
    _to_row_tiles(xn_ref, xn)
    logits = jnp.dot(xn, wr_ref[...], preferred_element_type=F32, precision=lax.Precision.HIGHEST)
    lane = lax.broadcasted_iota(jnp.int32, (tm, LANES), 1)
    lg = jnp.where(lane < N_EXPERTS, logits, -jnp.inf)
    m1 = jnp.max(lg, axis=1, keepdims=True)
    i1 = jnp.min(jnp.where(lg == m1, lane, LANES), axis=1, keepdims=True)
    lg2 = jnp.where(lane == i1, -jnp.inf, lg)
    m2 = jnp.max(lg2, axis=1, keepdims=True)
    i2 = jnp.min(jnp.where(lg2 == m2, lane, LANES), axis=1, keepdims=True)
    e2 = jnp.exp(m2 - m1)
    den = 1.0 + e2
    routed = jnp.where(jnp.logical_or(lane == i1, lane == i2), 1.0, 0.0)
    r_i = lax.broadcasted_iota(jnp.int32, (tm, tm), 0)
    c_i = lax.broadcasted_iota(jnp.int32, (tm, tm), 1)
    tri = jnp.where(c_i < r_i, 1.0, 0.0).astype(BF16)
    before = jnp.dot(tri, routed.astype(BF16), preferred_element_type=F32) + carry_ref[...]
    pos1 = jnp.sum(jnp.where(lane == i1, before, 0.0), axis=1, keepdims=True)
    pos2 = jnp.sum(jnp.where(lane == i2, before, 0.0), axis=1, keepdims=True)
    total = before[tm - 1:tm, :] + routed[tm - 1:tm, :]
    carry_ref[...] = total
    cnt_ref[...] = total
    meta = jnp.zeros((tm, LANES), F32)
    for col, val in ((META_E1, i1.astype(F32)), (META_E2, i2.astype(F32)), (META_POS1, pos1),
                     (META_POS2, pos2), (META_G1, 1.0 / den), (META_G2, e2 / den)):
        meta = jnp.where(lane == col, val, meta)
    meta_ref[...] = meta


def _moe_route(h, g, w_router, tm=512):
    t, d = h.shape
    return pl.pallas_call(
        _moe_route_kernel,
        grid=(t // tm,),
        in_specs=[pl.BlockSpec((tm, d), lambda i: (i, 0)),
                  pl.BlockSpec((1, d), lambda i: (0, 0)),
                  pl.BlockSpec((d, LANES), lambda i: (0, 0))],
        out_specs=[pl.BlockSpec((tm * ROW_TILES, LANES), lambda i: (i, 0)),
                   pl.BlockSpec((tm, LANES), lambda i: (i, 0)),
                   pl.BlockSpec((1, LANES), lambda i: (0, 0))],
        out_shape=[jax.ShapeDtypeStruct((t * ROW_TILES, LANES), F32),
                   jax.ShapeDtypeStruct((t, LANES), F32),
                   jax.ShapeDtypeStruct((1, LANES), F32)],
        scratch_shapes=[pltpu.VMEM((1, LANES), F32)],
        compiler_params=_cparams(1),
        name="moe_route",
    )(h, g, w_router)


def _moe_expert_kernel(src_s, tile_e_s, n_used_s, xn_hbm, wg_ref, wu_ref, wd_ref, y_ref,
                       rows_scr, xs_scr, acc_scr, sem, *, n_f):
    i = pl.program_id(0)
    f = pl.program_id(1)
    tm = xs_scr.shape[0]
    used = i < n_used_s[0]
    next_used = i + 1 < n_used_s[0]
    slot = i % 2
    part = tm // n_f

    def whole_tile(s):
        return pltpu.make_async_copy(xn_hbm.at[pl.ds(0, tm * ROW_TILES)], rows_scr.at[s], sem.at[s])

    @pl.when(jnp.logical_and(i == 0, f == 0))
    def _first():
        def start(r, c):
            pltpu.make_async_copy(_row_tile(xn_hbm, src_s[r]), _row_tile(rows_scr.at[0], r),
                                  sem.at[0]).start()
            return c
        lax.fori_loop(0, tm, start, 0)

    @pl.when(jnp.logical_and(used, f == 0))
    def _unpack():
        whole_tile(slot).wait()
        for s in range(ROW_TILES):
            xs_scr[:, s * LANES:(s + 1) * LANES] = _from_row_tiles(rows_scr.at[slot], tm, s).astype(BF16)
        acc_scr[...] = jnp.zeros_like(acc_scr)

    def fetch_next():
        base = (i + 1) * tm + f * part
        for r in range(part):
            pltpu.make_async_copy(_row_tile(xn_hbm, src_s[base + r]),
                                  _row_tile(rows_scr.at[1 - slot], f * part + r),
                                  sem.at[1 - slot]).start()

    @pl.when(jnp.logical_and(used, next_used))
    def _compute_and_fetch():
        fetch_next()
        acc_scr[...] += _swiglu_partial(xs_scr[...], wg_ref, wu_ref, wd_ref)

    @pl.when(jnp.logical_and(used, jnp.logical_not(next_used)))
    def _compute():
        acc_scr[...] += _swiglu_partial(xs_scr[...], wg_ref, wu_ref, wd_ref)

    @pl.when(f == n_f - 1)
    def _store():
        @pl.when(used)
        def _():
            _to_row_tiles(y_ref, acc_scr[...])

        @pl.when(jnp.logical_not(used))
        def _():
            y_ref[...] = jnp.zeros_like(y_ref)


def _moe_experts(xn3, src, tile_e, n_used, wgu, wd, layer, tm):
    n_rows = src.shape[0]
    d = wd.shape[-1]
    n_f = N_FF_TILES
    tf = wd.shape[-2] // n_f
    grid_spec = pltpu.PrefetchScalarGridSpec(
        num_scalar_prefetch=3,
        grid=(n_rows // tm, n_f),
        in_specs=[pl.BlockSpec(memory_space=pl.ANY),
                  pl.BlockSpec((1, 1, d, tf), lambda i, f, src, te, nu: (layer, te[i], 0, f)),
                  pl.BlockSpec((1, 1, d, tf), lambda i, f, src, te, nu: (layer, te[i], 0, n_f + f)),
                  pl.BlockSpec((1, 1, tf, d), lambda i, f, src, te, nu: (layer, te[i], f, 0))],
        out_specs=pl.BlockSpec((tm * ROW_TILES, LANES), lambda i, f, src, te, nu: (i, 0)),
        scratch_shapes=[pltpu.VMEM((2, tm * ROW_TILES, LANES), F32),
                        pltpu.VMEM((tm, d), BF16),
                        pltpu.VMEM((tm, d), F32),
                        pltpu.SemaphoreType.DMA((2,))],
    )
    return pl.pallas_call(
        functools.partial(_moe_expert_kernel, n_f=n_f),
        grid_spec=grid_spec,
        out_shape=jax.ShapeDtypeStruct((n_rows * ROW_TILES, LANES), F32),
        compiler_params=_cparams(2),
        name="moe_experts",
    )(src, tile_e, n_used, xn3, wgu, wgu, wd)


def _moe_combine_kernel(d1_s, d2_s, x_ref, meta_ref, gf_ref, y_hbm, o_ref, y_scr, sem, *, final_norm):
    i = pl.program_id(0)
    tm = x_ref.shape[0]
    slot = i % 2

    def fetch(tile, s):
        def start(r, c):
            pltpu.make_async_copy(_row_tile(y_hbm, d1_s[tile * tm + r]), _row_tile(y_scr.at[s, 0], r),
                                  sem.at[s]).start()
            pltpu.make_async_copy(_row_tile(y_hbm, d2_s[tile * tm + r]), _row_tile(y_scr.at[s, 1], r),
                                  sem.at[s]).start()
            return c
        lax.fori_loop(0, tm, start, 0)

    @pl.when(i == 0)
    def _first():
        fetch(0, 0)

    @pl.when(i + 1 < pl.num_programs(0))
    def _next():
        fetch(i + 1, 1 - slot)

    for which in range(2):
        pltpu.make_async_copy(y_hbm.at[pl.ds(0, tm * ROW_TILES)], y_scr.at[slot, which],
                              sem.at[slot]).wait()
    meta = meta_ref[...]
    g1 = meta[:, META_G1:META_G1 + 1]
    g2 = meta[:, META_G2:META_G2 + 1]
    for s in range(ROW_TILES):
        cols = slice(s * LANES, (s + 1) * LANES)
        o_ref[:, cols] = (x_ref[:, cols] + g1 * _from_row_tiles(y_scr.at[slot, 0], tm, s)
                          + g2 * _from_row_tiles(y_scr.at[slot, 1], tm, s))
    if final_norm:
        o_ref[...] = _rms(o_ref[...], gf_ref[...])


def _moe_combine(h, meta, g_final, y3, dest1, dest2, final_norm, tm=512):
    t, d = h.shape
    grid_spec = pltpu.PrefetchScalarGridSpec(
        num_scalar_prefetch=2,
        grid=(t // tm,),
        in_specs=[pl.BlockSpec((tm, d), lambda i, d1, d2: (i, 0)),
                  pl.BlockSpec((tm, LANES), lambda i, d1, d2: (i, 0)),
                  pl.BlockSpec((1, d), lambda i, d1, d2: (0, 0)),
                  pl.BlockSpec(memory_space=pl.ANY)],
        out_specs=pl.BlockSpec((tm, d), lambda i, d1, d2: (i, 0)),
        scratch_shapes=[pltpu.VMEM((2, 2, tm * ROW_TILES, LANES), F32),
                        pltpu.SemaphoreType.DMA((2,))],
    )
    return pl.pallas_call(
        functools.partial(_moe_combine_kernel, final_norm=final_norm),
        grid_spec=grid_spec,
        out_shape=jax.ShapeDtypeStruct((t, d), F32),
        compiler_params=_cparams(1),
        name="moe_combine",
    )(dest1, dest2, h, meta, g_final, y3)


def _moe(h, g, w_router, wgu, wd, layer, g_final, final_norm, tm=512):
    t, d = h.shape
    xn3, meta, cnt = _moe_route(h, g, w_router)
    counts = cnt[0, :N_EXPERTS].astype(jnp.int32)
    padded = (counts + tm - 1) // tm * tm
    ends = jnp.cumsum(padded)
    starts = ends - padded
    idx = meta[:, :META_G1].astype(jnp.int32)
    experts = jnp.arange(N_EXPERTS, dtype=jnp.int32)

    def group_row(e, pos):
        return pos + jnp.sum(jnp.where(e[:, None] == experts[None, :], starts[None, :], 0), axis=1)

    dest1 = group_row(idx[:, META_E1], idx[:, META_POS1])
    dest2 = group_row(idx[:, META_E2], idx[:, META_POS2])
    n_rows = 2 * t + N_EXPERTS * tm
    tok = jnp.arange(t, dtype=jnp.int32)
    src = jnp.zeros((n_rows,), jnp.int32).at[jnp.concatenate([dest1, dest2])].set(
        jnp.concatenate([tok, tok]), unique_indices=True)
    tile_start = jnp.arange(n_rows // tm, dtype=jnp.int32) * tm
    tile_e = jnp.minimum(jnp.sum((ends[None, :] <= tile_start[:, None]).astype(jnp.int32), axis=1),
                         N_EXPERTS - 1)
    n_used = (ends[-1:] // tm).astype(jnp.int32)
    y3 = _moe_experts(xn3, src, tile_e, n_used, wgu, wd, layer, tm)
    return _moe_combine(h, meta, g_final, y3, dest1, dest2, final_norm)


def _rope_tables(seq):
    pos = jnp.arange(seq, dtype=F32)
    inv_freq = 1.0 / (ROPE_THETA ** (jnp.arange(0, HEAD_DIM, 2, dtype=F32) / HEAD_DIM))
    ang = pos[:, None] * inv_freq[None, :]
    cos, sin = jnp.cos(ang), jnp.sin(ang)
    cos_t = jnp.tile(cos, (1, LANES // (HEAD_DIM // 2)))
    sin_t = jnp.tile(jnp.concatenate([-sin, sin], axis=1), (1, LANES // HEAD_DIM))
    return cos_t, sin_t


def kernel(x, a_norm, a_w_in, a_w_out, kv_norm, w_kv, b_f, b_norm, b_w_q, b_w_out, ffn_norm,
           dense_w_gate_up, dense_w_down, moe_router, moe_w_gate_up, moe_w_down, final_norm):
    batch, seq, d = x.shape
    depth = ffn_norm.shape[0]
    n_a = a_norm.shape[0]
    cos_t, sin_t = _rope_tables(seq)
    idx_w = IDX_HEADS * IDX_DIM
    dense_gu, dense_down = dense_w_gate_up.astype(BF16), dense_w_down.astype(BF16)
    moe_gu, moe_down = moe_w_gate_up.astype(BF16), moe_w_down.astype(BF16)

    h = x.reshape(batch * seq, d)
    k_sh = vx_sh = f_sh = fk_t = None
    for i in range(depth):
        if i < n_a:
            w = a_w_in[i]
            ki_w = w[:, 3 * HD + idx_w:3 * HD + idx_w + IDX_DIM]
            wi_w = w[:, 3 * HD + idx_w + IDX_DIM:]
            w_p = jnp.concatenate(
                [w[:, :3 * HD + idx_w], ki_w, ki_w, wi_w,
                 jnp.zeros((d, LANES - IDX_HEADS), w.dtype)], axis=1).astype(BF16)
            q, k, vx, qi, ki, wi = _a_proj(h, a_norm[i][None, :], w_p, cos_t, sin_t, seq)
            o = _dsa_attention(qi, wi, ki, q, k, vx, batch, seq)
            h = _out_proj(o, a_w_out[i].astype(BF16), h)
        else:
            j = i - n_a
            if k_sh is None:
                w_p = jnp.concatenate(
                    [w_kv, jnp.zeros((d, LANES - N_HEADS), w_kv.dtype)], axis=1).astype(BF16)
                b_p = jnp.concatenate([b_f, jnp.zeros((LANES - N_HEADS,), b_f.dtype)])[None, :]
                k_sh, vx_sh, f_sh = _kv_proj(h, kv_norm[None, :], w_p, b_p, seq)
                fk_t = f_sh[:, :N_HEADS].reshape(batch, seq, N_HEADS).transpose(0, 2, 1)
            q = _q_proj(h, b_norm[j][None, :], b_w_q[j].astype(BF16))
            o = _fox_attention(q, f_sh, fk_t, k_sh, vx_sh, batch, seq)
            h = _out_proj(o, b_w_out[j].astype(BF16), h)
        if i % 2 == 0:
            h = _ffn(h, ffn_norm[i][None, :], dense_gu, dense_down, i // 2)
        else:
            e = i // 2
            wr = jnp.concatenate(
                [moe_router[e], jnp.zeros((d, LANES - N_EXPERTS), moe_router.dtype)], axis=1)
            last = i == depth - 1
            h = _moe(h, ffn_norm[i][None, :], wr, moe_gu, moe_down, e, final_norm[None, :], last)
    if depth % 2 == 1:
        raise NotImplementedError("final norm is fused into the last expert mixer")
    return h.reshape(batch, seq, d)
```

```python
import functools
import math

import numpy as np
import jax
import jax.numpy as jnp
from jax import lax
from jax.experimental import pallas as pl
from jax.experimental.pallas import tpu as pltpu

N_HEADS = 16
HEAD_DIM = 64
IDX_HEADS = 8
IDX_DIM = 64
TOPK_MAX = 256
CHUNK = 64
ROPE_THETA = 10000.0
N_EXPERTS = 8
RMS_EPS = 1e-6

LANES = 128
HD = N_HEADS * HEAD_DIM
N_PAIRS = N_HEADS // 2
NEG = -1e30
LOG2E = math.log2(math.e)
VMEM_LIMIT = 52 * 1024 * 1024

ROW_TILE = 512
ATTN_TQ = 256
ATTN_TK = 1024

F32 = jnp.float32
BF16 = jnp.bfloat16


def _cparams(n_axes):
    return pltpu.CompilerParams(dimension_semantics=("arbitrary",) * n_axes,
                                vmem_limit_bytes=VMEM_LIMIT)


def _rms(x, g):
    return x * lax.rsqrt(jnp.mean(x * x, axis=-1, keepdims=True) + RMS_EPS) * g


def _half_mask(shape):
    return lax.broadcasted_iota(jnp.int32, shape, len(shape) - 1) < HEAD_DIM


def _nt_dot(a, b):
    return lax.dot_general(a, b, (((1,), (1,)), ((), ())), preferred_element_type=F32)


def _store_values_with_ones(vx_ref, p, c0):
    ones = jnp.ones((p.shape[0], LANES), BF16)
    for j in range(p.shape[1] // LANES):
        pair = (c0 // LANES) + j
        vx_ref[:, 2 * pair * LANES:(2 * pair + 1) * LANES] = p[:, j * LANES:(j + 1) * LANES].astype(BF16)
        vx_ref[:, (2 * pair + 1) * LANES:(2 * pair + 2) * LANES] = ones


def _rope_chunk(p, cos, sin_signed):
    fwd = pltpu.roll(p, LANES - 32, 1)
    bwd = pltpu.roll(p, 32, 1)
    lane = lax.broadcasted_iota(jnp.int32, p.shape, 1)
    partner = jnp.where((lane % HEAD_DIM) < HEAD_DIM // 2, fwd, bwd)
    return p * cos + partner * sin_signed


def _a_proj_kernel(x_ref, g_ref, w_ref, cos_ref, sin_ref,
                   q_ref, k_ref, vx_ref, qi_ref, ki_ref, wi_ref):
    xn = _rms(x_ref[...], g_ref[...]).astype(BF16)
    cos = cos_ref[...]
    sin = sin_ref[...]
    seg = 512

    def proj(c0, width):
        return jnp.dot(xn, w_ref[:, c0:c0 + width], preferred_element_type=F32)

    def rope_store(p, out_ref, o0, scale):
        for j in range(p.shape[1] // LANES):
            r = _rope_chunk(p[:, j * LANES:(j + 1) * LANES], cos, sin)
            if scale != 1.0:
                r = r * scale
            out_ref[:, o0 + j * LANES:o0 + (j + 1) * LANES] = r.astype(out_ref.dtype)

    for s in range(HD // seg):
        rope_store(proj(s * seg, seg), q_ref, s * seg, HEAD_DIM ** -0.5 * LOG2E)
    for s in range(HD // seg):
        rope_store(proj(HD + s * seg, seg), k_ref, s * seg, 1.0)
    for s in range(HD // seg):
        _store_values_with_ones(vx_ref, proj(2 * HD + s * seg, seg), s * seg)
    rope_store(proj(3 * HD, IDX_HEADS * IDX_DIM), qi_ref, 0, 1.0)
    tail = proj(3 * HD + IDX_HEADS * IDX_DIM, 2 * LANES)
    rope_store(tail[:, :LANES], ki_ref, 0, 1.0)
    wi_ref[...] = tail[:, LANES:] * (IDX_HEADS ** -0.5)


def _a_proj(h, g, w, cos_t, sin_t, seq, tm=ROW_TILE):
    t, d = h.shape
    n_w = w.shape[1]
    tiles_per_seq = seq // tm
    row = lambda i: (i, 0)
    tab = lambda i: (i % tiles_per_seq, 0)
    const = lambda i: (0, 0)
    return pl.pallas_call(
        _a_proj_kernel,
        grid=(t // tm,),
        in_specs=[pl.BlockSpec((tm, d), row),
                  pl.BlockSpec((1, d), const),
                  pl.BlockSpec((d, n_w), const),
                  pl.BlockSpec((tm, LANES), tab),
                  pl.BlockSpec((tm, LANES), tab)],
        out_specs=[pl.BlockSpec((tm, HD), row),
                   pl.BlockSpec((tm, HD), row),
                   pl.BlockSpec((tm, 2 * HD), row),
                   pl.BlockSpec((tm, IDX_HEADS * IDX_DIM), row),
                   pl.BlockSpec((tm, LANES), row),
                   pl.BlockSpec((tm, LANES), row)],
        out_shape=[jax.ShapeDtypeStruct((t, HD), BF16),
                   jax.ShapeDtypeStruct((t, HD), BF16),
                   jax.ShapeDtypeStruct((t, 2 * HD), BF16),
                   jax.ShapeDtypeStruct((t, IDX_HEADS * IDX_DIM), BF16),
                   jax.ShapeDtypeStruct((t, LANES), BF16),
                   jax.ShapeDtypeStruct((t, LANES), F32)],
        compiler_params=_cparams(1),
        name="a_proj",
    )(h, g, w, cos_t, sin_t)


def _kv_proj_kernel(x_ref, g_ref, w_ref, bf_ref, k_ref, vx_ref, cum_ref, carry_ref, *, tiles_per_seq):
    i = pl.program_id(0)
    tm = x_ref.shape[0]

    @pl.when(i % tiles_per_seq == 0)
    def _():
        carry_ref[...] = jnp.zeros_like(carry_ref)

    xn = _rms(x_ref[...], g_ref[...]).astype(BF16)
    seg = 512
    for s in range(HD // seg):
        k_ref[:, s * seg:(s + 1) * seg] = jnp.dot(
            xn, w_ref[:, s * seg:(s + 1) * seg], preferred_element_type=F32).astype(BF16)
    for s in range(HD // seg):
        _store_values_with_ones(
            vx_ref, jnp.dot(xn, w_ref[:, HD + s * seg:HD + (s + 1) * seg], preferred_element_type=F32),
            s * seg)
    f_logit = jnp.dot(xn, w_ref[:, 2 * HD:2 * HD + LANES], preferred_element_type=F32)
    log_f = jax.nn.log_sigmoid(f_logit + bf_ref[...])
    r_i = lax.broadcasted_iota(jnp.int32, (tm, tm), 0)
    c_i = lax.broadcasted_iota(jnp.int32, (tm, tm), 1)
    tri = jnp.where(c_i <= r_i, 1.0, 0.0).astype(BF16)
    x1 = log_f.astype(BF16)
    rem = log_f - x1.astype(F32)
    x2 = rem.astype(BF16)
    x3 = (rem - x2.astype(F32)).astype(BF16)
    cum = (jnp.dot(tri, x1, preferred_element_type=F32)
           + jnp.dot(tri, x2, preferred_element_type=F32)
           + jnp.dot(tri, x3, preferred_element_type=F32)) + carry_ref[...]
    cum_ref[...] = cum * LOG2E
    carry_ref[...] = cum[tm - 1:tm, :]


def _kv_proj(h, g, w, b_f, seq, tm=ROW_TILE):
    t, d = h.shape
    row = lambda i: (i, 0)
    const = lambda i: (0, 0)
    return pl.pallas_call(
        functools.partial(_kv_proj_kernel, tiles_per_seq=seq // tm),
        grid=(t // tm,),
        in_specs=[pl.BlockSpec((tm, d), row),
                  pl.BlockSpec((1, d), const),
                  pl.BlockSpec((d, w.shape[1]), const),
                  pl.BlockSpec((1, LANES), const)],
        out_specs=[pl.BlockSpec((tm, HD), row),
                   pl.BlockSpec((tm, 2 * HD), row),
                   pl.BlockSpec((tm, LANES), row)],
        out_shape=[jax.ShapeDtypeStruct((t, HD), BF16),
                   jax.ShapeDtypeStruct((t, 2 * HD), BF16),
                   jax.ShapeDtypeStruct((t, LANES), F32)],
        scratch_shapes=[pltpu.VMEM((1, LANES), F32)],
        compiler_params=_cparams(1),
        name="kv_proj",
    )(h, g, w, b_f)


def _q_proj_kernel(x_ref, g_ref, w_ref, q_ref):
    xn = _rms(x_ref[...], g_ref[...]).astype(BF16)
    seg = 512
    for s in range(HD // seg):
        p = jnp.dot(xn, w_ref[:, s * seg:(s + 1) * seg], preferred_element_type=F32)
        q_ref[:, s * seg:(s + 1) * seg] = (p * (HEAD_DIM ** -0.5 * LOG2E)).astype(BF16)


def _q_proj(h, g, w, tm=ROW_TILE):
    t, d = h.shape
    row = lambda i: (i, 0)
    const = lambda i: (0, 0)
    return pl.pallas_call(
        _q_proj_kernel,
        grid=(t // tm,),
        in_specs=[pl.BlockSpec((tm, d), row),
                  pl.BlockSpec((1, d), const),
                  pl.BlockSpec((d, HD), const)],
        out_specs=pl.BlockSpec((tm, HD), row),
        out_shape=jax.ShapeDtypeStruct((t, HD), BF16),
        compiler_params=_cparams(1),
        name="q_proj",
    )(h, g, w)


def _out_proj_kernel(o_ref, w_ref, h_ref, out_ref):
    out_ref[...] = h_ref[...] + jnp.dot(o_ref[...], w_ref[...], preferred_element_type=F32)


def _out_proj(o, w, h, tm=ROW_TILE):
    t, d = h.shape
    row = lambda i: (i, 0)
    const = lambda i: (0, 0)
    return pl.pallas_call(
        _out_proj_kernel,
        grid=(t // tm,),
        in_specs=[pl.BlockSpec((tm, HD), row),
                  pl.BlockSpec((HD, d), const),
                  pl.BlockSpec((tm, d), row)],
        out_specs=pl.BlockSpec((tm, d), row),
        out_shape=jax.ShapeDtypeStruct((t, d), F32),
        compiler_params=_cparams(1),
        name="out_proj",
    )(o, w, h)


def _stack_pair_queries(q_ref, q2_scr):
    tq = q_ref.shape[0]
    for j in range(N_PAIRS):
        qp = q_ref[:, j * LANES:(j + 1) * LANES]
        first = _half_mask(qp.shape)
        zero = jnp.zeros_like(qp)
        q2_scr[j, 0:tq, :] = jnp.where(first, qp, zero)
        q2_scr[j, tq:2 * tq, :] = jnp.where(first, zero, qp)


def _attn_init(m_scr, l_scr, acc_scr):
    m_scr[0] = jnp.full(m_scr.shape[1:], NEG, F32)
    l_scr[0] = jnp.zeros(l_scr.shape[1:], F32)
    acc_scr[0] = jnp.zeros(acc_scr.shape[1:], F32)


def _attn_block(q2_scr, k_ref, vx_ref, m_scr, l_scr, acc_scr, s_scr, p_scr, src, dst, bias_fn,
                row_bias_fn=None, drift_fn=None, exact_max=True):
    tq = p_scr.shape[0] // 2
    l_lo = l_hi = None
    for j in range(N_PAIRS):
        kp = k_ref[:, j * LANES:(j + 1) * LANES]
        sv = _nt_dot(q2_scr[j], kp)
        alphas = []
        for half in range(2):
            h = 2 * j + half
            rows = slice(half * tq, (half + 1) * tq)
            m_prev = m_scr[src, j, rows, :]
            if exact_max:
                s_scr[rows, :] = bias_fn(sv[rows], h)
                m_blk = jnp.max(s_scr[rows, :], axis=1, keepdims=True)
                if row_bias_fn is not None:
                    m_blk = m_blk + row_bias_fn(h)
                m_new = jnp.maximum(m_prev, m_blk)
                alphas.append(jnp.exp2(m_prev - m_new))
                logits = s_scr[rows, :]
            else:
                if drift_fn is None:
                    m_new = m_prev
                    alphas.append(None)
                else:
                    m_new = m_prev + drift_fn(h)
                    alphas.append(jnp.exp2(-drift_fn(h)))
                logits = bias_fn(sv[rows], h)
            m_scr[dst, j, rows, :] = m_new
            shift = m_new if row_bias_fn is None else m_new - row_bias_fn(h)
            p_scr[rows, :] = jnp.exp2(logits - shift).astype(BF16)
        pv = jnp.dot(p_scr[...], vx_ref[:, 2 * j * LANES:2 * (j + 1) * LANES],
                     preferred_element_type=F32)
        for half in range(2):
            rows = slice(half * tq, (half + 1) * tq)
            acc_old = acc_scr[src, j, rows, :]
            l_old = l_scr[src, j, rows, :]
            if alphas[half] is not None:
                acc_old = alphas[half] * acc_old
                l_old = alphas[half] * l_old
            l_new = l_old + pv[rows, LANES:]
            acc_scr[dst, j, rows, :] = acc_old + pv[rows, :LANES]
            l_scr[dst, j, rows, :] = l_new
            if not exact_max:
                l_hi = l_new if l_hi is None else jnp.maximum(l_hi, l_new)
                l_lo = l_new if l_lo is None else jnp.minimum(l_lo, l_new)
    return l_lo, l_hi


def _sums_usable(l_lo, l_hi):
    return jnp.logical_and(jnp.min(l_lo) > 0.0, jnp.max(l_hi) < 1e37)


def _attn_finish(o_ref, l_scr, acc_scr, slot):
    tq = o_ref.shape[0]
    for j in range(N_PAIRS):
        o2 = acc_scr[slot, j] / l_scr[slot, j]
        o_ref[:, j * LANES:(j + 1) * LANES] = jnp.where(
            _half_mask((tq, LANES)), o2[0:tq], o2[tq:2 * tq]).astype(o_ref.dtype)


def _attn_scratch(tq, tk):
    return [pltpu.VMEM((N_PAIRS, 2 * tq, LANES), BF16),
            pltpu.VMEM((2, N_PAIRS, 2 * tq, 1), F32),
            pltpu.VMEM((2, N_PAIRS, 2 * tq, LANES), F32),
            pltpu.VMEM((2, N_PAIRS, 2 * tq, LANES), F32),
            pltpu.VMEM((2 * tq, tk), F32),
            pltpu.VMEM((2 * tq, tk), BF16),
            pltpu.SMEM((2,), jnp.int32)]


def _ordered_key_to_f32(key):
    return lax.bitcast_convert_type(key ^ ((key >> 31) & jnp.int32(0x7FFFFFFF)), F32)


def _dsa_kernel(b_s, qt_s, ph_s, kb_s, nkb_s,
                qi_ref, wi_ref, ki_ref, q_ref, k_ref, vx_ref, o_ref,
                score_scr, q2_scr, m_scr, l_scr, acc_scr, s_scr, p_scr, state, *, k_sel):
    step = pl.program_id(0)
    qt = qt_s[step]
    phase = ph_s[step]
    kb = kb_s[step]
    nkb = nkb_s[step]
    tq = q_ref.shape[0]
    tk = k_ref.shape[0]

    row = lax.broadcasted_iota(jnp.int32, (tq, 1), 0) + qt * tq
    limit = (row // CHUNK + 1) * CHUNK

    @pl.when(phase == 0)
    def _index():
        kk = ki_ref[...]
        w = wi_ref[...]
        acc = jnp.zeros((tq, tk), F32)
        for j in range(IDX_HEADS // 2):
            qp = qi_ref[:, j * LANES:(j + 1) * LANES]
            first = _half_mask(qp.shape)
            zero = jnp.zeros_like(qp)
            for half in range(2):
                h = 2 * j + half
                qh = jnp.where(first, qp, zero) if half == 0 else jnp.where(first, zero, qp)
                sc = _nt_dot(qh, kk)
                acc = acc + jnp.maximum(sc, 0.0) * w[:, h:h + 1]
        key = lax.broadcasted_iota(jnp.int32, (1, tk), 1) + kb * tk
        sc = jnp.where(key < limit, acc, -jnp.inf)
        score_scr[kb] = sc

    @pl.when(jnp.logical_and(phase == 0, kb == nkb - 1))
    def _select():
        band = 128
        one = jnp.ones((band, LANES), F32)
        zero = jnp.zeros((band, LANES), F32)

        def count(cand, strict):
            c_full = jnp.broadcast_to(cand, (tq, LANES))
            accs = []
            for r0 in range(0, tq, band):
                c_b = c_full[r0:r0 + band]

                def body(j, acc, r0=r0, c_b=c_b):
                    blk = score_scr[j, r0:r0 + band, :]
                    for c in range(tk // LANES):
                        part = blk[:, c * LANES:(c + 1) * LANES]
                        hit = (part > c_b) if strict else (part >= c_b)
                        acc = acc + jnp.where(hit, one, zero)
                    return acc

                accs.append(lax.fori_loop(0, nkb, body, zero))
            return jnp.sum(jnp.concatenate(accs, axis=0), axis=1, keepdims=True)

        def bit_body(i, r):
            cand_u = r | jnp.left_shift(jnp.int32(1), 31 - i)
            cnt = count(_ordered_key_to_f32(cand_u ^ jnp.int32(-2 ** 31)), False)
            return jnp.where(cnt >= k_sel, cand_u, r)

        r = lax.fori_loop(0, 32, bit_body, jnp.zeros((tq, 1), jnp.int32))
        few = limit <= k_sel
        thr = jnp.where(few, -jnp.inf, _ordered_key_to_f32(r ^ jnp.int32(-2 ** 31)))
        need = k_sel - count(thr, True)

        sub = 256
        r_i = lax.broadcasted_iota(jnp.int32, (sub, sub), 0)
        c_i = lax.broadcasted_iota(jnp.int32, (sub, sub), 1)
        tri = jnp.where(r_i <= c_i, 1.0, 0.0).astype(BF16)

        def sel_body(j, carry):
            for c in range(tk // sub):
                cols = slice(c * sub, (c + 1) * sub)
                blk = score_scr[j, :, cols]
                key = lax.broadcasted_iota(jnp.int32, (1, sub), 1) + (j * tk + c * sub)
                eq = blk == thr
                rank = carry + jnp.dot(jnp.where(eq, 1.0, 0.0).astype(BF16), tri,
                                       preferred_element_type=F32)
                bias = jnp.where(eq, jnp.where(rank <= need, 0.0, NEG),
                                 jnp.where(blk > thr, 0.0, NEG))
                score_scr[j, :, cols] = jnp.where(key < limit, bias, NEG)
                carry = rank[:, sub - 1:sub]
            return carry

        lax.fori_loop(0, nkb, sel_body, jnp.zeros((tq, 1), F32))

    attend = functools.partial(_attn_block, q2_scr, k_ref, vx_ref, m_scr, l_scr, acc_scr, s_scr, p_scr,
                               bias_fn=lambda sv, h: sv + score_scr[kb])

    state[1] = jnp.where(jnp.logical_and(phase == 1, kb == 0), 1, 0)

    @pl.when(jnp.logical_and(phase == 1, kb == 0))
    def _init():
        _stack_pair_queries(q_ref, q2_scr)
        _attn_init(m_scr, l_scr, acc_scr)
        state[0] = 0

    @pl.when(jnp.logical_and(phase == 1, kb > 0))
    def _one_pass():
        src = state[0]
        extremes = attend(src=src, dst=1 - src, exact_max=False)
        state[1] = jnp.where(_sums_usable(*extremes), 0, 1)

    @pl.when(jnp.logical_and(phase == 1, state[1] == 1))
    def _two_pass():
        src = state[0]
        attend(src=src, dst=1 - src)

    @pl.when(phase == 1)
    def _advance():
        state[0] = 1 - state[0]

    @pl.when(jnp.logical_and(phase == 1, kb == nkb - 1))
    def _finish():
        _attn_finish(o_ref, l_scr, acc_scr, state[0])


def _dsa_schedule(batch, seq, tq, tk):
    rows = []
    for b in range(batch):
        for qt in range(seq // tq):
            nkb = -(-((qt + 1) * tq) // tk)
            for phase in range(2):
                for kb in range(nkb):
                    rows.append((b, qt, phase, kb, nkb))
    return [jnp.asarray(c, jnp.int32) for c in np.asarray(rows, np.int32).T]


def _dsa_attention(qi, wi, ki, q, k, vx, batch, seq, tq=ATTN_TQ, tk=ATTN_TK):
    t = q.shape[0]
    qt_per_b = seq // tq
    kb_per_b = seq // tk
    sched = _dsa_schedule(batch, seq, tq, tk)
    k_sel = min(TOPK_MAX, seq // 4)

    def q_row(i, b, qt, ph, kb, nkb):
        return (b[i] * qt_per_b + qt[i], 0)

    def ki_row(i, b, qt, ph, kb, nkb):
        return (b[i] * kb_per_b + jnp.where(ph[i] == 0, kb[i], nkb[i] - 1), 0)

    def kv_row(i, b, qt, ph, kb, nkb):
        return (b[i] * kb_per_b + jnp.where(ph[i] == 0, 0, kb[i]), 0)

    grid_spec = pltpu.PrefetchScalarGridSpec(
        num_scalar_prefetch=5,
        grid=(int(sched[0].shape[0]),),
        in_specs=[pl.BlockSpec((tq, IDX_HEADS * IDX_DIM), q_row),
                  pl.BlockSpec((tq, LANES), q_row),
                  pl.BlockSpec((tk, LANES), ki_row),
                  pl.BlockSpec((tq, HD), q_row),
                  pl.BlockSpec((tk, HD), kv_row),
                  pl.BlockSpec((tk, 2 * HD), kv_row)],
        out_specs=pl.BlockSpec((tq, HD), q_row),
        scratch_shapes=[pltpu.VMEM((kb_per_b, tq, tk), F32)] + _attn_scratch(tq, tk),
    )
    return pl.pallas_call(
        functools.partial(_dsa_kernel, k_sel=k_sel),
        grid_spec=grid_spec,
        out_shape=jax.ShapeDtypeStruct((t, HD), BF16),
        compiler_params=_cparams(1),
        name="dsa_attention",
    )(*sched, qi, wi, ki, q, k, vx)


def _fox_kernel(b_s, qt_s, kb_s, nkb_s, q_ref, fq_ref, k_ref, vx_ref, fk_ref, o_ref,
                q2_scr, m_scr, l_scr, acc_scr, s_scr, p_scr, state, fq_scr):
    step = pl.program_id(0)
    qt = qt_s[step]
    kb = kb_s[step]
    nkb = nkb_s[step]
    tq = q_ref.shape[0]
    tk = k_ref.shape[0]

    @pl.when(kb == 0)
    def _init():
        _stack_pair_queries(q_ref, q2_scr)
        _attn_init(m_scr, l_scr, acc_scr)
        state[0] = 0
        fq = fq_ref[...]
        for h in range(N_HEADS):
            fq_scr[h] = fq[:, h:h + 1]

    fk = fk_ref[0]
    attend = functools.partial(_attn_block, q2_scr, k_ref, vx_ref, m_scr, l_scr, acc_scr, s_scr, p_scr,
                               row_bias_fn=lambda h: fq_scr[h])
    decay = lambda sv, h: sv - fk[h:h + 1, :]
    diagonal = (kb + 1) * tk > qt * tq
    state[1] = 1

    def masked_decay():
        row = lax.broadcasted_iota(jnp.int32, (tq, 1), 0) + qt * tq
        key = lax.broadcasted_iota(jnp.int32, (1, tk), 1) + kb * tk
        causal = jnp.where(key <= row, 0.0, NEG)
        return lambda sv, h: sv + (causal - fk[h:h + 1, :])

    @pl.when(jnp.logical_and(jnp.logical_not(diagonal), kb > 0))
    def _one_pass():
        src = state[0]
        extremes = attend(src=src, dst=1 - src, bias_fn=decay, exact_max=False,
                          drift_fn=lambda h: fk[h:h + 1, 0:1] - fk[h:h + 1, tk - 1:tk])
        state[1] = jnp.where(_sums_usable(*extremes), 0, 1)

    @pl.when(jnp.logical_and(diagonal, kb > 0))
    def _one_pass_masked():
        src = state[0]
        extremes = attend(src=src, dst=1 - src, bias_fn=masked_decay(), exact_max=False,
                          drift_fn=lambda h: fk[h:h + 1, 0:1] - fq_scr[h])
        state[1] = jnp.where(_sums_usable(*extremes), 0, 1)

    @pl.when(jnp.logical_and(jnp.logical_not(diagonal), state[1] == 1))
    def _two_pass():
        src = state[0]
        attend(src=src, dst=1 - src, bias_fn=decay)

    @pl.when(jnp.logical_and(diagonal, state[1] == 1))
    def _two_pass_masked():
        src = state[0]
        attend(src=src, dst=1 - src, bias_fn=masked_decay())

    state[0] = 1 - state[0]

    @pl.when(kb == nkb - 1)
    def _finish():
        _attn_finish(o_ref, l_scr, acc_scr, state[0])


def _fox_schedule(batch, seq, tq, tk):
    rows = []
    for b in range(batch):
        for qt in range(seq // tq):
            nkb = -(-((qt + 1) * tq) // tk)
            for kb in range(nkb):
                rows.append((b, qt, kb, nkb))
    return [jnp.asarray(c, jnp.int32) for c in np.asarray(rows, np.int32).T]


def _fox_attention(q, fq, fk_t, k, vx, batch, seq, tq=ATTN_TQ, tk=ATTN_TK):
    t = q.shape[0]
    qt_per_b = seq // tq
    kb_per_b = seq // tk
    sched = _fox_schedule(batch, seq, tq, tk)

    def q_row(i, b, qt, kb, nkb):
        return (b[i] * qt_per_b + qt[i], 0)

    def kv_row(i, b, qt, kb, nkb):
        return (b[i] * kb_per_b + kb[i], 0)

    def fk_row(i, b, qt, kb, nkb):
        return (b[i], 0, kb[i])

    grid_spec = pltpu.PrefetchScalarGridSpec(
        num_scalar_prefetch=4,
        grid=(int(sched[0].shape[0]),),
        in_specs=[pl.BlockSpec((tq, HD), q_row),
                  pl.BlockSpec((tq, LANES), q_row),
                  pl.BlockSpec((tk, HD), kv_row),
                  pl.BlockSpec((tk, 2 * HD), kv_row),
                  pl.BlockSpec((1, N_HEADS, tk), fk_row)],
        out_specs=pl.BlockSpec((tq, HD), q_row),
        scratch_shapes=_attn_scratch(tq, tk) + [pltpu.VMEM((N_HEADS, tq, 1), F32)],
    )
    return pl.pallas_call(
        _fox_kernel,
        grid_spec=grid_spec,
        out_shape=jax.ShapeDtypeStruct((t, HD), BF16),
        compiler_params=_cparams(1),
        name="fox_attention",
    )(*sched, q, fq, k, vx, fk_t)


N_FF_TILES = 2


def _swiglu_partial(xn, wg_ref, wu_ref, wd_ref):
    lead = (0,) * (len(wg_ref.shape) - 2)
    g = jnp.dot(xn, wg_ref[lead], preferred_element_type=F32)
    u = jnp.dot(xn, wu_ref[lead], preferred_element_type=F32)
    a = g * jax.nn.sigmoid(g) * u
    return jnp.dot(a.astype(BF16), wd_ref[lead], preferred_element_type=F32)


def _ffn_kernel(x_ref, g_ref, wg_ref, wu_ref, wd_ref, o_ref, xn_scr, acc_scr):
    f = pl.program_id(1)

    @pl.when(f == 0)
    def _():
        xn_scr[...] = _rms(x_ref[...], g_ref[...]).astype(BF16)
        acc_scr[...] = jnp.zeros_like(acc_scr)

    acc_scr[...] += _swiglu_partial(xn_scr[...], wg_ref, wu_ref, wd_ref)

    @pl.when(f == pl.num_programs(1) - 1)
    def _():
        o_ref[...] = x_ref[...] + acc_scr[...]


def _ffn(h, g, wgu, wd, layer, tm=ROW_TILE):
    t, d = h.shape
    n_f = N_FF_TILES
    tf = wd.shape[1] // n_f
    return pl.pallas_call(
        _ffn_kernel,
        grid=(t // tm, n_f),
        in_specs=[pl.BlockSpec((tm, d), lambda i, f: (i, 0)),
                  pl.BlockSpec((1, d), lambda i, f: (0, 0)),
                  pl.BlockSpec((1, d, tf), lambda i, f: (layer, 0, f)),
                  pl.BlockSpec((1, d, tf), lambda i, f: (layer, 0, n_f + f)),
                  pl.BlockSpec((1, tf, d), lambda i, f: (layer, f, 0))],
        out_specs=pl.BlockSpec((tm, d), lambda i, f: (i, 0)),
        out_shape=jax.ShapeDtypeStruct((t, d), F32),
        scratch_shapes=[pltpu.VMEM((tm, d), BF16), pltpu.VMEM((tm, d), F32)],
        compiler_params=_cparams(2),
        name="ffn_dense",
    )(h, g, wgu, wgu, wd)


META_E1, META_E2, META_POS1, META_POS2, META_G1, META_G2 = range(6)
ROW_TILES = 8


def _to_row_tiles(ref, x):
    rows = x.shape[0]
    for s in range(ROW_TILES):
        ref[pl.ds(s, rows, stride=ROW_TILES), :] = x[:, s * LANES:(s + 1) * LANES]


def _from_row_tiles(ref, rows, s):
    return ref[pl.ds(s, rows, stride=ROW_TILES), :]


def _row_tile(ref, r):
    return ref.at[pl.ds(pl.multiple_of(r * ROW_TILES, ROW_TILES), ROW_TILES)]


def _moe_route_kernel(x_ref, g_ref, wr_ref, xn_ref, meta_ref, cnt_ref, carry_ref):
    i = pl.program_id(0)
    tm = x_ref.shape[0]

    @pl.when(i == 0)
    def _():
        carry_ref[...] = jnp.zeros_like(carry_ref)

    xn = _rms(x_ref[...], g_ref[...])
    _to_row_tiles(xn_ref, xn)
    logits = jnp.dot(xn, wr_ref[...], preferred_element_type=F32, precision=lax.Precision.HIGHEST)
    lane = lax.broadcasted_iota(jnp.int32, (tm, LANES), 1)
    lg = jnp.where(lane < N_EXPERTS, logits, -jnp.inf)
    m1 = jnp.max(lg, axis=1, keepdims=True)
    i1 = jnp.min(jnp.where(lg == m1, lane, LANES), axis=1, keepdims=True)
    lg2 = jnp.where(lane == i1, -jnp.inf, lg)
    m2 = jnp.max(lg2, axis=1, keepdims=True)
    i2 = jnp.min(jnp.where(lg2 == m2, lane, LANES), axis=1, keepdims=True)
    e2 = jnp.exp(m2 - m1)
    den = 1.0 + e2
    routed = jnp.where(jnp.logical_or(lane == i1, lane == i2), 1.0, 0.0)
    r_i = lax.broadcasted_iota(jnp.int32, (tm, tm), 0)
    c_i = lax.broadcasted_iota(jnp.int32, (tm, tm), 1)
    tri = jnp.where(c_i < r_i, 1.0, 0.0).astype(BF16)
    before = jnp.dot(tri, routed.astype(BF16), preferred_element_type=F32) + carry_ref[...]
    pos1 = jnp.sum(jnp.where(lane == i1, before, 0.0), axis=1, keepdims=True)
    pos2 = jnp.sum(jnp.where(lane == i2, before, 0.0), axis=1, keepdims=True)
    total = before[tm - 1:tm, :] + routed[tm - 1:tm, :]
    carry_ref[...] = total
    cnt_ref[...] = total
    meta = jnp.zeros((tm, LANES), F32)
    for col, val in ((META_E1, i1.astype(F32)), (META_E2, i2.astype(F32)), (META_POS1, pos1),
                     (META_POS2, pos2), (META_G1, 1.0 / den), (META_G2, e2 / den)):
        meta = jnp.where(lane == col, val, meta)
    meta_ref[...] = meta


def _moe_route(h, g, w_router, tm=ROW_TILE):
    t, d = h.shape
    return pl.pallas_call(
        _moe_route_kernel,
        grid=(t // tm,),
        in_specs=[pl.BlockSpec((tm, d), lambda i: (i, 0)),
                  pl.BlockSpec((1, d), lambda i: (0, 0)),
                  pl.BlockSpec((d, LANES), lambda i: (0, 0))],
        out_specs=[pl.BlockSpec((tm * ROW_TILES, LANES), lambda i: (i, 0)),
                   pl.BlockSpec((tm, LANES), lambda i: (i, 0)),
                   pl.BlockSpec((1, LANES), lambda i: (0, 0))],
        out_shape=[jax.ShapeDtypeStruct((t * ROW_TILES, LANES), F32),
                   jax.ShapeDtypeStruct((t, LANES), F32),
                   jax.ShapeDtypeStruct((1, LANES), F32)],
        scratch_shapes=[pltpu.VMEM((1, LANES), F32)],
        compiler_params=_cparams(1),
        name="moe_route",
    )(h, g, w_router)


def _moe_expert_kernel(src_s, tile_e_s, n_used_s, xn_hbm, wg_ref, wu_ref, wd_ref, y_ref,
                       rows_scr, xs_scr, acc_scr, sem, *, n_f):
    i = pl.program_id(0)
    f = pl.program_id(1)
    tm = xs_scr.shape[0]
    used = i < n_used_s[0]
    next_used = i + 1 < n_used_s[0]
    slot = i % 2
    part = tm // n_f

    def whole_tile(s):
        return pltpu.make_async_copy(xn_hbm.at[pl.ds(0, tm * ROW_TILES)], rows_scr.at[s], sem.at[s])

    @pl.when(jnp.logical_and(i == 0, f == 0))
    def _first():
        def start(r, c):
            pltpu.make_async_copy(_row_tile(xn_hbm, src_s[r]), _row_tile(rows_scr.at[0], r),
                                  sem.at[0]).start()
            return c
        lax.fori_loop(0, tm, start, 0)

    @pl.when(jnp.logical_and(used, f == 0))
    def _unpack():
        whole_tile(slot).wait()
        for s in range(ROW_TILES):
            xs_scr[:, s * LANES:(s + 1) * LANES] = _from_row_tiles(rows_scr.at[slot], tm, s).astype(BF16)
        acc_scr[...] = jnp.zeros_like(acc_scr)

    def fetch_next():
        base = (i + 1) * tm + f * part
        for r in range(part):
            pltpu.make_async_copy(_row_tile(xn_hbm, src_s[base + r]),
                                  _row_tile(rows_scr.at[1 - slot], f * part + r),
                                  sem.at[1 - slot]).start()

    @pl.when(jnp.logical_and(used, next_used))
    def _compute_and_fetch():
        fetch_next()
        acc_scr[...] += _swiglu_partial(xs_scr[...], wg_ref, wu_ref, wd_ref)

    @pl.when(jnp.logical_and(used, jnp.logical_not(next_used)))
    def _compute():
        acc_scr[...] += _swiglu_partial(xs_scr[...], wg_ref, wu_ref, wd_ref)

    @pl.when(f == n_f - 1)
    def _store():
        @pl.when(used)
        def _():
            _to_row_tiles(y_ref, acc_scr[...])

        @pl.when(jnp.logical_not(used))
        def _():
            y_ref[...] = jnp.zeros_like(y_ref)


def _moe_experts(xn3, src, tile_e, n_used, wgu, wd, layer, tm):
    n_rows = src.shape[0]
    d = wd.shape[-1]
    n_f = N_FF_TILES
    tf = wd.shape[-2] // n_f
    grid_spec = pltpu.PrefetchScalarGridSpec(
        num_scalar_prefetch=3,
        grid=(n_rows // tm, n_f),
        in_specs=[pl.BlockSpec(memory_space=pl.ANY),
                  pl.BlockSpec((1, 1, d, tf), lambda i, f, src, te, nu: (layer, te[i], 0, f)),
                  pl.BlockSpec((1, 1, d, tf), lambda i, f, src, te, nu: (layer, te[i], 0, n_f + f)),
                  pl.BlockSpec((1, 1, tf, d), lambda i, f, src, te, nu: (layer, te[i], f, 0))],
        out_specs=pl.BlockSpec((tm * ROW_TILES, LANES), lambda i, f, src, te, nu: (i, 0)),
        scratch_shapes=[pltpu.VMEM((2, tm * ROW_TILES, LANES), F32),
                        pltpu.VMEM((tm, d), BF16),
                        pltpu.VMEM((tm, d), F32),
                        pltpu.SemaphoreType.DMA((2,))],
    )
    return pl.pallas_call(
        functools.partial(_moe_expert_kernel, n_f=n_f),
        grid_spec=grid_spec,
        out_shape=jax.ShapeDtypeStruct((n_rows * ROW_TILES, LANES), F32),
        compiler_params=_cparams(2),
        name="moe_experts",
    )(src, tile_e, n_used, xn3, wgu, wgu, wd)


def _moe_combine_kernel(d1_s, d2_s, x_ref, meta_ref, gf_ref, y_hbm, o_ref, y_scr, sem, *, final_norm):
    i = pl.program_id(0)
    tm = x_ref.shape[0]
    slot = i % 2

    def fetch(tile, s):
        def start(r, c):
            pltpu.make_async_copy(_row_tile(y_hbm, d1_s[tile * tm + r]), _row_tile(y_scr.at[s, 0], r),
                                  sem.at[s]).start()
            pltpu.make_async_copy(_row_tile(y_hbm, d2_s[tile * tm + r]), _row_tile(y_scr.at[s, 1], r),
                                  sem.at[s]).start()
            return c
        lax.fori_loop(0, tm, start, 0)

    @pl.when(i == 0)
    def _first():
        fetch(0, 0)

    @pl.when(i + 1 < pl.num_programs(0))
    def _next():
        fetch(i + 1, 1 - slot)

    for which in range(2):
        pltpu.make_async_copy(y_hbm.at[pl.ds(0, tm * ROW_TILES)], y_scr.at[slot, which],
                              sem.at[slot]).wait()
    meta = meta_ref[...]
    g1 = meta[:, META_G1:META_G1 + 1]
    g2 = meta[:, META_G2:META_G2 + 1]
    for s in range(ROW_TILES):
        cols = slice(s * LANES, (s + 1) * LANES)
        o_ref[:, cols] = (x_ref[:, cols] + g1 * _from_row_tiles(y_scr.at[slot, 0], tm, s)
                          + g2 * _from_row_tiles(y_scr.at[slot, 1], tm, s))
    if final_norm:
        o_ref[...] = _rms(o_ref[...], gf_ref[...])


def _moe_combine(h, meta, g_final, y3, dest1, dest2, final_norm, tm=ROW_TILE):
    t, d = h.shape
    grid_spec = pltpu.PrefetchScalarGridSpec(
        num_scalar_prefetch=2,
        grid=(t // tm,),
        in_specs=[pl.BlockSpec((tm, d), lambda i, d1, d2: (i, 0)),
                  pl.BlockSpec((tm, LANES), lambda i, d1, d2: (i, 0)),
                  pl.BlockSpec((1, d), lambda i, d1, d2: (0, 0)),
                  pl.BlockSpec(memory_space=pl.ANY)],
        out_specs=pl.BlockSpec((tm, d), lambda i, d1, d2: (i, 0)),
        scratch_shapes=[pltpu.VMEM((2, 2, tm * ROW_TILES, LANES), F32),
                        pltpu.SemaphoreType.DMA((2,))],
    )
    return pl.pallas_call(
        functools.partial(_moe_combine_kernel, final_norm=final_norm),
        grid_spec=grid_spec,
        out_shape=jax.ShapeDtypeStruct((t, d), F32),
        compiler_params=_cparams(1),
        name="moe_combine",
    )(dest1, dest2, h, meta, g_final, y3)


def _moe(h, g, w_router, wgu, wd, layer, g_final, final_norm, tm=ROW_TILE):
    t, d = h.shape
    xn3, meta, cnt = _moe_route(h, g, w_router)
    counts = cnt[0, :N_EXPERTS].astype(jnp.int32)
    padded = (counts + tm - 1) // tm * tm
    ends = jnp.cumsum(padded)
    starts = ends - padded
    idx = meta[:, :META_G1].astype(jnp.int32)
    experts = jnp.arange(N_EXPERTS, dtype=jnp.int32)

    def group_row(e, pos):
        return pos + jnp.sum(jnp.where(e[:, None] == experts[None, :], starts[None, :], 0), axis=1)

    dest1 = group_row(idx[:, META_E1], idx[:, META_POS1])
    dest2 = group_row(idx[:, META_E2], idx[:, META_POS2])
    n_rows = 2 * t + N_EXPERTS * tm
    tok = jnp.arange(t, dtype=jnp.int32)
    src = jnp.zeros((n_rows,), jnp.int32).at[jnp.concatenate([dest1, dest2])].set(
        jnp.concatenate([tok, tok]), unique_indices=True)
    tile_start = jnp.arange(n_rows // tm, dtype=jnp.int32) * tm
    tile_e = jnp.minimum(jnp.sum((ends[None, :] <= tile_start[:, None]).astype(jnp.int32), axis=1),
                         N_EXPERTS - 1)
    n_used = (ends[-1:] // tm).astype(jnp.int32)
    y3 = _moe_experts(xn3, src, tile_e, n_used, wgu, wd, layer, tm)
    return _moe_combine(h, meta, g_final, y3, dest1, dest2, final_norm)


def _rope_tables(seq):
    pos = jnp.arange(seq, dtype=F32)
    inv_freq = 1.0 / (ROPE_THETA ** (jnp.arange(0, HEAD_DIM, 2, dtype=F32) / HEAD_DIM))
    ang = pos[:, None] * inv_freq[None, :]
    cos, sin = jnp.cos(ang), jnp.sin(ang)
    cos_t = jnp.tile(cos, (1, LANES // (HEAD_DIM // 2)))
    sin_t = jnp.tile(jnp.concatenate([-sin, sin], axis=1), (1, LANES // HEAD_DIM))
    return cos_t, sin_t


def kernel(x, a_norm, a_w_in, a_w_out, kv_norm, w_kv, b_f, b_norm, b_w_q, b_w_out, ffn_norm,
           dense_w_gate_up, dense_w_down, moe_router, moe_w_gate_up, moe_w_down, final_norm):
    batch, seq, d = x.shape
    depth = ffn_norm.shape[0]
    n_a = a_norm.shape[0]
    cos_t, sin_t = _rope_tables(seq)
    idx_w = IDX_HEADS * IDX_DIM
    dense_gu, dense_down = dense_w_gate_up.astype(BF16), dense_w_down.astype(BF16)
    moe_gu, moe_down = moe_w_gate_up.astype(BF16), moe_w_down.astype(BF16)

    h = x.reshape(batch * seq, d)
    k_sh = vx_sh = f_sh = fk_t = None
    for i in range(depth):
        if i < n_a:
            w = a_w_in[i]
            ki_w = w[:, 3 * HD + idx_w:3 * HD + idx_w + IDX_DIM]
            wi_w = w[:, 3 * HD + idx_w + IDX_DIM:]
            w_p = jnp.concatenate(
                [w[:, :3 * HD + idx_w], ki_w, ki_w, wi_w,
                 jnp.zeros((d, LANES - IDX_HEADS), w.dtype)], axis=1).astype(BF16)
            q, k, vx, qi, ki, wi = _a_proj(h, a_norm[i][None, :], w_p, cos_t, sin_t, seq)
            o = _dsa_attention(qi, wi, ki, q, k, vx, batch, seq)
            h = _out_proj(o, a_w_out[i].astype(BF16), h)
        else:
            j = i - n_a
            if k_sh is None:
                w_p = jnp.concatenate(
                    [w_kv, jnp.zeros((d, LANES - N_HEADS), w_kv.dtype)], axis=1).astype(BF16)
                b_p = jnp.concatenate([b_f, jnp.zeros((LANES - N_HEADS,), b_f.dtype)])[None, :]
                k_sh, vx_sh, f_sh = _kv_proj(h, kv_norm[None, :], w_p, b_p, seq)
                fk_t = f_sh[:, :N_HEADS].reshape(batch, seq, N_HEADS).transpose(0, 2, 1)
            q = _q_proj(h, b_norm[j][None, :], b_w_q[j].astype(BF16))
            o = _fox_attention(q, f_sh, fk_t, k_sh, vx_sh, batch, seq)
            h = _out_proj(o, b_w_out[j].astype(BF16), h)
        if i % 2 == 0:
            h = _ffn(h, ffn_norm[i][None, :], dense_gu, dense_down, i // 2)
        else:
            e = i // 2
            wr = jnp.concatenate(
                [moe_router[e], jnp.zeros((d, LANES - N_EXPERTS), moe_router.dtype)], axis=1)
            last = i == depth - 1
            h = _moe(h, ffn_norm[i][None, :], wr, moe_gu, moe_down, e, final_norm[None, :], last)
    if depth % 2 == 1:
        raise NotImplementedError("final norm is fused into the last expert mixer")
    return h.reshape(batch, seq, d)
```

```python
import functools
import math

import numpy as np
import jax
import jax.numpy as jnp
from jax import lax
from jax.experimental import pallas as pl
from jax.experimental.pallas import tpu as pltpu

N_HEADS = 16
HEAD_DIM = 64
IDX_HEADS = 8
IDX_DIM = 64
TOPK_MAX = 256
CHUNK = 64
ROPE_THETA = 10000.0
N_EXPERTS = 8
RMS_EPS = 1e-6

LANES = 128
HD = N_HEADS * HEAD_DIM
N_PAIRS = N_HEADS // 2
NEG = -1e30
LOG2E = math.log2(math.e)
VMEM_LIMIT = 52 * 1024 * 1024

ROW_TILE = 512
ATTN_TQ = 256
ATTN_TK = 1024

F32 = jnp.float32
BF16 = jnp.bfloat16


def _cparams(n_axes):
    return pltpu.CompilerParams(dimension_semantics=("arbitrary",) * n_axes,
                                vmem_limit_bytes=VMEM_LIMIT)


def _rms(x, g):
    return x * lax.rsqrt(jnp.mean(x * x, axis=-1, keepdims=True) + RMS_EPS) * g


def _half_mask(shape):
    return lax.broadcasted_iota(jnp.int32, shape, len(shape) - 1) < HEAD_DIM


def _nt_dot(a, b):
    return lax.dot_general(a, b, (((1,), (1,)), ((), ())), preferred_element_type=F32)


def _store_values_with_ones(vx_ref, p, c0):
    ones = jnp.ones((p.shape[0], LANES), BF16)
    for j in range(p.shape[1] // LANES):
        pair = (c0 // LANES) + j
        vx_ref[:, 2 * pair * LANES:(2 * pair + 1) * LANES] = p[:, j * LANES:(j + 1) * LANES].astype(BF16)
        vx_ref[:, (2 * pair + 1) * LANES:(2 * pair + 2) * LANES] = ones


def _rope_chunk(p, cos, sin_signed):
    fwd = pltpu.roll(p, LANES - 32, 1)
    bwd = pltpu.roll(p, 32, 1)
    lane = lax.broadcasted_iota(jnp.int32, p.shape, 1)
    partner = jnp.where((lane % HEAD_DIM) < HEAD_DIM // 2, fwd, bwd)
    return p * cos + partner * sin_signed


def _a_proj_kernel(x_ref, g_ref, w_ref, cos_ref, sin_ref,
                   q_ref, k_ref, vx_ref, qi_ref, ki_ref, wi_ref):
    xn = _rms(x_ref[...], g_ref[...]).astype(BF16)
    cos = cos_ref[...]
    sin = sin_ref[...]
    seg = 512

    def proj(c0, width):
        return jnp.dot(xn, w_ref[:, c0:c0 + width], preferred_element_type=F32)

    def rope_store(p, out_ref, o0, scale):
        for j in range(p.shape[1] // LANES):
            r = _rope_chunk(p[:, j * LANES:(j + 1) * LANES], cos, sin)
            if scale != 1.0:
                r = r * scale
            out_ref[:, o0 + j * LANES:o0 + (j + 1) * LANES] = r.astype(out_ref.dtype)

    for s in range(HD // seg):
        rope_store(proj(s * seg, seg), q_ref, s * seg, HEAD_DIM ** -0.5 * LOG2E)
    for s in range(HD // seg):
        rope_store(proj(HD + s * seg, seg), k_ref, s * seg, 1.0)
    for s in range(HD // seg):
        _store_values_with_ones(vx_ref, proj(2 * HD + s * seg, seg), s * seg)
    rope_store(proj(3 * HD, IDX_HEADS * IDX_DIM), qi_ref, 0, 1.0)
    tail = proj(3 * HD + IDX_HEADS * IDX_DIM, 2 * LANES)
    rope_store(tail[:, :LANES], ki_ref, 0, 1.0)
    wi_ref[...] = tail[:, LANES:] * (IDX_HEADS ** -0.5)


def _a_proj(h, g, w, cos_t, sin_t, seq, tm=ROW_TILE):
    t, d = h.shape
    n_w = w.shape[1]
    tiles_per_seq = seq // tm
    row = lambda i: (i, 0)
    tab = lambda i: (i % tiles_per_seq, 0)
    const = lambda i: (0, 0)
    return pl.pallas_call(
        _a_proj_kernel,
        grid=(t // tm,),
        in_specs=[pl.BlockSpec((tm, d), row),
                  pl.BlockSpec((1, d), const),
                  pl.BlockSpec((d, n_w), const),
                  pl.BlockSpec((tm, LANES), tab),
                  pl.BlockSpec((tm, LANES), tab)],
        out_specs=[pl.BlockSpec((tm, HD), row),
                   pl.BlockSpec((tm, HD), row),
                   pl.BlockSpec((tm, 2 * HD), row),
                   pl.BlockSpec((tm, IDX_HEADS * IDX_DIM), row),
                   pl.BlockSpec((tm, LANES), row),
                   pl.BlockSpec((tm, LANES), row)],
        out_shape=[jax.ShapeDtypeStruct((t, HD), BF16),
                   jax.ShapeDtypeStruct((t, HD), BF16),
                   jax.ShapeDtypeStruct((t, 2 * HD), BF16),
                   jax.ShapeDtypeStruct((t, IDX_HEADS * IDX_DIM), BF16),
                   jax.ShapeDtypeStruct((t, LANES), BF16),
                   jax.ShapeDtypeStruct((t, LANES), F32)],
        compiler_params=_cparams(1),
        name="a_proj",
    )(h, g, w, cos_t, sin_t)


def _kv_proj_kernel(x_ref, g_ref, w_ref, bf_ref, k_ref, vx_ref, cum_ref, carry_ref, *, tiles_per_seq):
    i = pl.program_id(0)
    tm = x_ref.shape[0]

    @pl.when(i % tiles_per_seq == 0)
    def _():
        carry_ref[...] = jnp.zeros_like(carry_ref)

    xn = _rms(x_ref[...], g_ref[...]).astype(BF16)
    seg = 512
    for s in range(HD // seg):
        k_ref[:, s * seg:(s + 1) * seg] = jnp.dot(
            xn, w_ref[:, s * seg:(s + 1) * seg], preferred_element_type=F32).astype(BF16)
    for s in range(HD // seg):
        _store_values_with_ones(
            vx_ref, jnp.dot(xn, w_ref[:, HD + s * seg:HD + (s + 1) * seg], preferred_element_type=F32),
            s * seg)
    f_logit = jnp.dot(xn, w_ref[:, 2 * HD:2 * HD + LANES], preferred_element_type=F32)
    log_f = jax.nn.log_sigmoid(f_logit + bf_ref[...])
    r_i = lax.broadcasted_iota(jnp.int32, (tm, tm), 0)
    c_i = lax.broadcasted_iota(jnp.int32, (tm, tm), 1)
    tri = jnp.where(c_i <= r_i, 1.0, 0.0).astype(BF16)
    x1 = log_f.astype(BF16)
    rem = log_f - x1.astype(F32)
    x2 = rem.astype(BF16)
    x3 = (rem - x2.astype(F32)).astype(BF16)
    cum = (jnp.dot(tri, x1, preferred_element_type=F32)
           + jnp.dot(tri, x2, preferred_element_type=F32)
           + jnp.dot(tri, x3, preferred_element_type=F32)) + carry_ref[...]
    cum_ref[...] = cum * LOG2E
    carry_ref[...] = cum[tm - 1:tm, :]


def _kv_proj(h, g, w, b_f, seq, tm=ROW_TILE):
    t, d = h.shape
    row = lambda i: (i, 0)
    const = lambda i: (0, 0)
    return pl.pallas_call(
        functools.partial(_kv_proj_kernel, tiles_per_seq=seq // tm),
        grid=(t // tm,),
        in_specs=[pl.BlockSpec((tm, d), row),
                  pl.BlockSpec((1, d), const),
                  pl.BlockSpec((d, w.shape[1]), const),
                  pl.BlockSpec((1, LANES), const)],
        out_specs=[pl.BlockSpec((tm, HD), row),
                   pl.BlockSpec((tm, 2 * HD), row),
                   pl.BlockSpec((tm, LANES), row)],
        out_shape=[jax.ShapeDtypeStruct((t, HD), BF16),
                   jax.ShapeDtypeStruct((t, 2 * HD), BF16),
                   jax.ShapeDtypeStruct((t, LANES), F32)],
        scratch_shapes=[pltpu.VMEM((1, LANES), F32)],
        compiler_params=_cparams(1),
        name="kv_proj",
    )(h, g, w, b_f)


def _q_proj_kernel(x_ref, g_ref, w_ref, q_ref):
    xn = _rms(x_ref[...], g_ref[...]).astype(BF16)
    seg = 512
    for s in range(HD // seg):
        p = jnp.dot(xn, w_ref[:, s * seg:(s + 1) * seg], preferred_element_type=F32)
        q_ref[:, s * seg:(s + 1) * seg] = (p * (HEAD_DIM ** -0.5 * LOG2E)).astype(BF16)


def _q_proj(h, g, w, tm=ROW_TILE):
    t, d = h.shape
    row = lambda i: (i, 0)
    const = lambda i: (0, 0)
    return pl.pallas_call(
        _q_proj_kernel,
        grid=(t // tm,),
        in_specs=[pl.BlockSpec((tm, d), row),
                  pl.BlockSpec((1, d), const),
                  pl.BlockSpec((d, HD), const)],
        out_specs=pl.BlockSpec((tm, HD), row),
        out_shape=jax.ShapeDtypeStruct((t, HD), BF16),
        compiler_params=_cparams(1),
        name="q_proj",
    )(h, g, w)


def _out_proj_kernel(o_ref, w_ref, h_ref, out_ref):
    out_ref[...] = h_ref[...] + jnp.dot(o_ref[...], w_ref[...], preferred_element_type=F32)


def _out_proj(o, w, h, tm=ROW_TILE):
    t, d = h.shape
    row = lambda i: (i, 0)
    const = lambda i: (0, 0)
    return pl.pallas_call(
        _out_proj_kernel,
        grid=(t // tm,),
        in_specs=[pl.BlockSpec((tm, HD), row),
                  pl.BlockSpec((HD, d), const),
                  pl.BlockSpec((tm, d), row)],
        out_specs=pl.BlockSpec((tm, d), row),
        out_shape=jax.ShapeDtypeStruct((t, d), F32),
        compiler_params=_cparams(1),
        name="out_proj",
    )(o, w, h)


def _stack_pair_queries(q_ref, q2_scr):
    tq = q_ref.shape[0]
    for j in range(N_PAIRS):
        qp = q_ref[:, j * LANES:(j + 1) * LANES]
        first = _half_mask(qp.shape)
        zero = jnp.zeros_like(qp)
        q2_scr[j, 0:tq, :] = jnp.where(first, qp, zero)
        q2_scr[j, tq:2 * tq, :] = jnp.where(first, zero, qp)


def _attn_init(m_scr, l_scr, acc_scr):
    m_scr[0] = jnp.full(m_scr.shape[1:], NEG, F32)
    l_scr[0] = jnp.zeros(l_scr.shape[1:], F32)
    acc_scr[0] = jnp.zeros(acc_scr.shape[1:], F32)


def _attn_block(q2_scr, k_ref, vx_ref, m_scr, l_scr, acc_scr, s_scr, p_scr, src, dst, bias_fn,
                row_bias_fn=None, drift_fn=None, exact_max=True, first_base=None):
    tq = p_scr.shape[0] // 2
    l_lo = l_hi = None
    for j in range(N_PAIRS):
        kp = k_ref[:, j * LANES:(j + 1) * LANES]
        sv = _nt_dot(q2_scr[j], kp)
        alphas = []
        for half in range(2):
            h = 2 * j + half
            rows = slice(half * tq, (half + 1) * tq)
            m_prev = m_scr[src, j, rows, :]
            if exact_max:
                s_scr[rows, :] = bias_fn(sv[rows], h)
                m_blk = jnp.max(s_scr[rows, :], axis=1, keepdims=True)
                if row_bias_fn is not None:
                    m_blk = m_blk + row_bias_fn(h)
                m_new = jnp.maximum(m_prev, m_blk)
                alphas.append(jnp.exp2(m_prev - m_new))
                logits = s_scr[rows, :]
            else:
                logits = bias_fn(sv[rows], h)
                if first_base is not None:
                    base = first_base(sv[rows], logits)
                    if row_bias_fn is not None:
                        base = base + row_bias_fn(h)
                    m_new = base if drift_fn is None else base + drift_fn(h)
                    alphas.append(None)
                elif drift_fn is None:
                    m_new = m_prev
                    alphas.append(None)
                else:
                    m_new = m_prev + drift_fn(h)
                    alphas.append(jnp.exp2(-drift_fn(h)))
            m_scr[dst, j, rows, :] = m_new
            shift = m_new if row_bias_fn is None else m_new - row_bias_fn(h)
            p_scr[rows, :] = jnp.exp2(logits - shift).astype(BF16)
        pv = jnp.dot(p_scr[...], vx_ref[:, 2 * j * LANES:2 * (j + 1) * LANES],
                     preferred_element_type=F32)
        for half in range(2):
            rows = slice(half * tq, (half + 1) * tq)
            acc_old = acc_scr[src, j, rows, :]
            l_old = l_scr[src, j, rows, :]
            if alphas[half] is not None:
                acc_old = alphas[half] * acc_old
                l_old = alphas[half] * l_old
            l_new = l_old + pv[rows, LANES:]
            acc_scr[dst, j, rows, :] = acc_old + pv[rows, :LANES]
            l_scr[dst, j, rows, :] = l_new
            if not exact_max:
                l_hi = l_new if l_hi is None else jnp.maximum(l_hi, l_new)
                l_lo = l_new if l_lo is None else jnp.minimum(l_lo, l_new)
    return l_lo, l_hi


def _sums_usable(l_lo, l_hi):
    return jnp.logical_and(jnp.min(l_lo) > 0.0, jnp.max(l_hi) < 1e37)


def _attn_finish(o_ref, l_scr, acc_scr, slot):
    tq = o_ref.shape[0]
    for j in range(N_PAIRS):
        o2 = acc_scr[slot, j] / l_scr[slot, j]
        o_ref[:, j * LANES:(j + 1) * LANES] = jnp.where(
            _half_mask((tq, LANES)), o2[0:tq], o2[tq:2 * tq]).astype(o_ref.dtype)


def _attn_scratch(tq, tk):
    return [pltpu.VMEM((N_PAIRS, 2 * tq, LANES), BF16),
            pltpu.VMEM((2, N_PAIRS, 2 * tq, 1), F32),
            pltpu.VMEM((2, N_PAIRS, 2 * tq, LANES), F32),
            pltpu.VMEM((2, N_PAIRS, 2 * tq, LANES), F32),
            pltpu.VMEM((2 * tq, tk), F32),
            pltpu.VMEM((2 * tq, tk), BF16),
            pltpu.SMEM((2,), jnp.int32)]


def _ordered_key_to_f32(key):
    return lax.bitcast_convert_type(key ^ ((key >> 31) & jnp.int32(0x7FFFFFFF)), F32)


def _dsa_kernel(b_s, qt_s, ph_s, kb_s, nkb_s,
                qi_ref, wi_ref, ki_ref, q_ref, k_ref, vx_ref, o_ref,
                score_scr, q2_scr, m_scr, l_scr, acc_scr, s_scr, p_scr, state, *, k_sel):
    step = pl.program_id(0)
    qt = qt_s[step]
    phase = ph_s[step]
    kb = kb_s[step]
    nkb = nkb_s[step]
    tq = q_ref.shape[0]
    tk = k_ref.shape[0]

    row = lax.broadcasted_iota(jnp.int32, (tq, 1), 0) + qt * tq
    limit = (row // CHUNK + 1) * CHUNK

    @pl.when(phase == 0)
    def _index():
        kk = ki_ref[...]
        w = wi_ref[...]
        acc = jnp.zeros((tq, tk), F32)
        for j in range(IDX_HEADS // 2):
            qp = qi_ref[:, j * LANES:(j + 1) * LANES]
            first = _half_mask(qp.shape)
            zero = jnp.zeros_like(qp)
            for half in range(2):
                h = 2 * j + half
                qh = jnp.where(first, qp, zero) if half == 0 else jnp.where(first, zero, qp)
                sc = _nt_dot(qh, kk)
                acc = acc + jnp.maximum(sc, 0.0) * w[:, h:h + 1]
        key = lax.broadcasted_iota(jnp.int32, (1, tk), 1) + kb * tk
        sc = jnp.where(key < limit, acc, -jnp.inf)
        score_scr[kb] = sc

    @pl.when(jnp.logical_and(phase == 0, kb == nkb - 1))
    def _select():
        band = 128
        one = jnp.ones((band, LANES), F32)
        zero = jnp.zeros((band, LANES), F32)

        def count(cand, strict):
            c_full = jnp.broadcast_to(cand, (tq, LANES))
            accs = []
            for r0 in range(0, tq, band):
                c_b = c_full[r0:r0 + band]

                def body(j, acc, r0=r0, c_b=c_b):
                    blk = score_scr[j, r0:r0 + band, :]
                    for c in range(tk // LANES):
                        part = blk[:, c * LANES:(c + 1) * LANES]
                        hit = (part > c_b) if strict else (part >= c_b)
                        acc = acc + jnp.where(hit, one, zero)
                    return acc

                accs.append(lax.fori_loop(0, nkb, body, zero))
            return jnp.sum(jnp.concatenate(accs, axis=0), axis=1, keepdims=True)

        def bit_body(i, r):
            cand_u = r | jnp.left_shift(jnp.int32(1), 31 - i)
            cnt = count(_ordered_key_to_f32(cand_u ^ jnp.int32(-2 ** 31)), False)
            return jnp.where(cnt >= k_sel, cand_u, r)

        r = lax.fori_loop(0, 32, bit_body, jnp.zeros((tq, 1), jnp.int32))
        few = limit <= k_sel
        thr = jnp.where(few, -jnp.inf, _ordered_key_to_f32(r ^ jnp.int32(-2 ** 31)))
        need = k_sel - count(thr, True)

        sub = 256
        r_i = lax.broadcasted_iota(jnp.int32, (sub, sub), 0)
        c_i = lax.broadcasted_iota(jnp.int32, (sub, sub), 1)
        tri = jnp.where(r_i <= c_i, 1.0, 0.0).astype(BF16)

        def sel_body(j, carry):
            for c in range(tk // sub):
                cols = slice(c * sub, (c + 1) * sub)
                blk = score_scr[j, :, cols]
                key = lax.broadcasted_iota(jnp.int32, (1, sub), 1) + (j * tk + c * sub)
                eq = blk == thr
                rank = carry + jnp.dot(jnp.where(eq, 1.0, 0.0).astype(BF16), tri,
                                       preferred_element_type=F32)
                bias = jnp.where(eq, jnp.where(rank <= need, 0.0, NEG),
                                 jnp.where(blk > thr, 0.0, NEG))
                score_scr[j, :, cols] = jnp.where(key < limit, bias, NEG)
                carry = rank[:, sub - 1:sub]
            return carry

        lax.fori_loop(0, nkb, sel_body, jnp.zeros((tq, 1), F32))

    attend = functools.partial(_attn_block, q2_scr, k_ref, vx_ref, m_scr, l_scr, acc_scr, s_scr, p_scr,
                               bias_fn=lambda sv, h: sv + score_scr[kb])

    state[1] = 0

    @pl.when(jnp.logical_and(phase == 1, kb == 0))
    def _init():
        _stack_pair_queries(q_ref, q2_scr)
        _attn_init(m_scr, l_scr, acc_scr)
        state[0] = 0
        extremes = attend(src=0, dst=1, exact_max=False,
                          first_base=lambda raw, biased: jnp.max(raw[:, 0:LANES], axis=1, keepdims=True))
        state[1] = jnp.where(_sums_usable(*extremes), 0, 1)

    @pl.when(jnp.logical_and(phase == 1, kb > 0))
    def _one_pass():
        src = state[0]
        extremes = attend(src=src, dst=1 - src, exact_max=False)
        state[1] = jnp.where(_sums_usable(*extremes), 0, 1)

    @pl.when(jnp.logical_and(phase == 1, state[1] == 1))
    def _two_pass():
        src = state[0]
        attend(src=src, dst=1 - src)

    @pl.when(phase == 1)
    def _advance():
        state[0] = 1 - state[0]

    @pl.when(jnp.logical_and(phase == 1, kb == nkb - 1))
    def _finish():
        _attn_finish(o_ref, l_scr, acc_scr, state[0])


def _dsa_schedule(batch, seq, tq, tk):
    rows = []
    for b in range(batch):
        for qt in range(seq // tq):
            nkb = -(-((qt + 1) * tq) // tk)
            for phase in range(2):
                for kb in range(nkb):
                    rows.append((b, qt, phase, kb, nkb))
    return [jnp.asarray(c, jnp.int32) for c in np.asarray(rows, np.int32).T]


def _dsa_attention(qi, wi, ki, q, k, vx, batch, seq, tq=ATTN_TQ, tk=ATTN_TK):
    t = q.shape[0]
    qt_per_b = seq // tq
    kb_per_b = seq // tk
    sched = _dsa_schedule(batch, seq, tq, tk)
    k_sel = min(TOPK_MAX, seq // 4)

    def q_row(i, b, qt, ph, kb, nkb):
        return (b[i] * qt_per_b + qt[i], 0)

    def ki_row(i, b, qt, ph, kb, nkb):
        return (b[i] * kb_per_b + jnp.where(ph[i] == 0, kb[i], nkb[i] - 1), 0)

    def kv_row(i, b, qt, ph, kb, nkb):
        return (b[i] * kb_per_b + jnp.where(ph[i] == 0, 0, kb[i]), 0)

    grid_spec = pltpu.PrefetchScalarGridSpec(
        num_scalar_prefetch=5,
        grid=(int(sched[0].shape[0]),),
        in_specs=[pl.BlockSpec((tq, IDX_HEADS * IDX_DIM), q_row),
                  pl.BlockSpec((tq, LANES), q_row),
                  pl.BlockSpec((tk, LANES), ki_row),
                  pl.BlockSpec((tq, HD), q_row),
                  pl.BlockSpec((tk, HD), kv_row),
                  pl.BlockSpec((tk, 2 * HD), kv_row)],
        out_specs=pl.BlockSpec((tq, HD), q_row),
        scratch_shapes=[pltpu.VMEM((kb_per_b, tq, tk), F32)] + _attn_scratch(tq, tk),
    )
    return pl.pallas_call(
        functools.partial(_dsa_kernel, k_sel=k_sel),
        grid_spec=grid_spec,
        out_shape=jax.ShapeDtypeStruct((t, HD), BF16),
        compiler_params=_cparams(1),
        name="dsa_attention",
    )(*sched, qi, wi, ki, q, k, vx)


def _fox_kernel(b_s, qt_s, kb_s, nkb_s, q_ref, fq_ref, k_ref, vx_ref, fk_ref, o_ref,
                q2_scr, m_scr, l_scr, acc_scr, s_scr, p_scr, state, fq_scr):
    step = pl.program_id(0)
    qt = qt_s[step]
    kb = kb_s[step]
    nkb = nkb_s[step]
    tq = q_ref.shape[0]
    tk = k_ref.shape[0]

    @pl.when(kb == 0)
    def _init():
        _stack_pair_queries(q_ref, q2_scr)
        _attn_init(m_scr, l_scr, acc_scr)
        state[0] = 0
        fq = fq_ref[...]
        for h in range(N_HEADS):
            fq_scr[h] = fq[:, h:h + 1]

    fk = fk_ref[0]
    attend = functools.partial(_attn_block, q2_scr, k_ref, vx_ref, m_scr, l_scr, acc_scr, s_scr, p_scr,
                               row_bias_fn=lambda h: fq_scr[h])
    decay = lambda sv, h: sv - fk[h:h + 1, :]
    diagonal = (kb + 1) * tk > qt * tq
    state[1] = 1

    def masked_decay():
        row = lax.broadcasted_iota(jnp.int32, (tq, 1), 0) + qt * tq
        key = lax.broadcasted_iota(jnp.int32, (1, tk), 1) + kb * tk
        causal = jnp.where(key <= row, 0.0, NEG)
        return lambda sv, h: sv + (causal - fk[h:h + 1, :])

    @pl.when(jnp.logical_and(jnp.logical_not(diagonal), kb > 0))
    def _one_pass():
        src = state[0]
        extremes = attend(src=src, dst=1 - src, bias_fn=decay, exact_max=False,
                          drift_fn=lambda h: fk[h:h + 1, 0:1] - fk[h:h + 1, tk - 1:tk])
        state[1] = jnp.where(_sums_usable(*extremes), 0, 1)

    @pl.when(jnp.logical_and(jnp.logical_not(diagonal), kb == 0))
    def _one_pass_first():
        extremes = attend(src=0, dst=1, bias_fn=decay, exact_max=False,
                          first_base=lambda raw, biased: biased[:, 0:1],
                          drift_fn=lambda h: fk[h:h + 1, 0:1] - fk[h:h + 1, tk - 1:tk])
        state[1] = jnp.where(_sums_usable(*extremes), 0, 1)

    @pl.when(jnp.logical_and(diagonal, kb > 0))
    def _one_pass_masked():
        src = state[0]
        extremes = attend(src=src, dst=1 - src, bias_fn=masked_decay(), exact_max=False,
                          drift_fn=lambda h: fk[h:h + 1, 0:1] - fq_scr[h])
        state[1] = jnp.where(_sums_usable(*extremes), 0, 1)

    @pl.when(jnp.logical_and(jnp.logical_not(diagonal), state[1] == 1))
    def _two_pass():
        src = state[0]
        attend(src=src, dst=1 - src, bias_fn=decay)

    @pl.when(jnp.logical_and(diagonal, state[1] == 1))
    def _two_pass_masked():
        src = state[0]
        attend(src=src, dst=1 - src, bias_fn=masked_decay())

    state[0] = 1 - state[0]

    @pl.when(kb == nkb - 1)
    def _finish():
        _attn_finish(o_ref, l_scr, acc_scr, state[0])


def _fox_schedule(batch, seq, tq, tk):
    rows = []
    for b in range(batch):
        for qt in range(seq // tq):
            nkb = -(-((qt + 1) * tq) // tk)
            for kb in range(nkb):
                rows.append((b, qt, kb, nkb))
    return [jnp.asarray(c, jnp.int32) for c in np.asarray(rows, np.int32).T]


def _fox_attention(q, fq, fk_t, k, vx, batch, seq, tq=ATTN_TQ, tk=ATTN_TK):
    t = q.shape[0]
    qt_per_b = seq // tq
    kb_per_b = seq // tk
    sched = _fox_schedule(batch, seq, tq, tk)

    def q_row(i, b, qt, kb, nkb):
        return (b[i] * qt_per_b + qt[i], 0)

    def kv_row(i, b, qt, kb, nkb):
        return (b[i] * kb_per_b + kb[i], 0)

    def fk_row(i, b, qt, kb, nkb):
        return (b[i], 0, kb[i])

    grid_spec = pltpu.PrefetchScalarGridSpec(
        num_scalar_prefetch=4,
        grid=(int(sched[0].shape[0]),),
        in_specs=[pl.BlockSpec((tq, HD), q_row),
                  pl.BlockSpec((tq, LANES), q_row),
                  pl.BlockSpec((tk, HD), kv_row),
                  pl.BlockSpec((tk, 2 * HD), kv_row),
                  pl.BlockSpec((1, N_HEADS, tk), fk_row)],
        out_specs=pl.BlockSpec((tq, HD), q_row),
        scratch_shapes=_attn_scratch(tq, tk) + [pltpu.VMEM((N_HEADS, tq, 1), F32)],
    )
    return pl.pallas_call(
        _fox_kernel,
        grid_spec=grid_spec,
        out_shape=jax.ShapeDtypeStruct((t, HD), BF16),
        compiler_params=_cparams(1),
        name="fox_attention",
    )(*sched, q, fq, k, vx, fk_t)


N_FF_TILES = 2


def _swiglu_partial(xn, wg_ref, wu_ref, wd_ref):
    lead = (0,) * (len(wg_ref.shape) - 2)
    g = jnp.dot(xn, wg_ref[lead], preferred_element_type=F32)
    u = jnp.dot(xn, wu_ref[lead], preferred_element_type=F32)
    a = g * jax.nn.sigmoid(g) * u
    return jnp.dot(a.astype(BF16), wd_ref[lead], preferred_element_type=F32)


def _ffn_kernel(x_ref, g_ref, wg_ref, wu_ref, wd_ref, o_ref, xn_scr, acc_scr):
    f = pl.program_id(1)

    @pl.when(f == 0)
    def _():
        xn_scr[...] = _rms(x_ref[...], g_ref[...]).astype(BF16)
        acc_scr[...] = jnp.zeros_like(acc_scr)

    acc_scr[...] += _swiglu_partial(xn_scr[...], wg_ref, wu_ref, wd_ref)

    @pl.when(f == pl.num_programs(1) - 1)
    def _():
        o_ref[...] = x_ref[...] + acc_scr[...]


def _ffn(h, g, wgu, wd, layer, tm=ROW_TILE):
    t, d = h.shape
    n_f = N_FF_TILES
    tf = wd.shape[1] // n_f
    return pl.pallas_call(
        _ffn_kernel,
        grid=(t // tm, n_f),
        in_specs=[pl.BlockSpec((tm, d), lambda i, f: (i, 0)),
                  pl.BlockSpec((1, d), lambda i, f: (0, 0)),
                  pl.BlockSpec((1, d, tf), lambda i, f: (layer, 0, f)),
                  pl.BlockSpec((1, d, tf), lambda i, f: (layer, 0, n_f + f)),
                  pl.BlockSpec((1, tf, d), lambda i, f: (layer, f, 0))],
        out_specs=pl.BlockSpec((tm, d), lambda i, f: (i, 0)),
        out_shape=jax.ShapeDtypeStruct((t, d), F32),
        scratch_shapes=[pltpu.VMEM((tm, d), BF16), pltpu.VMEM((tm, d), F32)],
        compiler_params=_cparams(2),
        name="ffn_dense",
    )(h, g, wgu, wgu, wd)


META_E1, META_E2, META_POS1, META_POS2, META_G1, META_G2 = range(6)
ROW_TILES = 8


def _to_row_tiles(ref, x):
    rows = x.shape[0]
    for s in range(ROW_TILES):
        ref[pl.ds(s, rows, stride=ROW_TILES), :] = x[:, s * LANES:(s + 1) * LANES]


def _from_row_tiles(ref, rows, s):
    return ref[pl.ds(s, rows, stride=ROW_TILES), :]


def _row_tile(ref, r):
    return ref.at[pl.ds(pl.multiple_of(r * ROW_TILES, ROW_TILES), ROW_TILES)]


def _moe_route_kernel(x_ref, g_ref, wr_ref, xn_ref, meta_ref, cnt_ref, carry_ref):
    i = pl.program_id(0)
    tm = x_ref.shape[0]

    @pl.when(i == 0)
    def _():
        carry_ref[...] = jnp.zeros_like(carry_ref)

    xn = _rms(x_ref[...], g_ref[...])
    _to_row_tiles(xn_ref, xn)
    logits = jnp.dot(xn, wr_ref[...], preferred_element_type=F32, precision=lax.Precision.HIGHEST)
    lane = lax.broadcasted_iota(jnp.int32, (tm, LANES), 1)
    lg = jnp.where(lane < N_EXPERTS, logits, -jnp.inf)
    m1 = jnp.max(lg, axis=1, keepdims=True)
    i1 = jnp.min(jnp.where(lg == m1, lane, LANES), axis=1, keepdims=True)
    lg2 = jnp.where(lane == i1, -jnp.inf, lg)
    m2 = jnp.max(lg2, axis=1, keepdims=True)
    i2 = jnp.min(jnp.where(lg2 == m2, lane, LANES), axis=1, keepdims=True)
    e2 = jnp.exp(m2 - m1)
    den = 1.0 + e2
    routed = jnp.where(jnp.logical_or(lane == i1, lane == i2), 1.0, 0.0)
    r_i = lax.broadcasted_iota(jnp.int32, (tm, tm), 0)
    c_i = lax.broadcasted_iota(jnp.int32, (tm, tm), 1)
    tri = jnp.where(c_i < r_i, 1.0, 0.0).astype(BF16)
    before = jnp.dot(tri, routed.astype(BF16), preferred_element_type=F32) + carry_ref[...]
    pos1 = jnp.sum(jnp.where(lane == i1, before, 0.0), axis=1, keepdims=True)
    pos2 = jnp.sum(jnp.where(lane == i2, before, 0.0), axis=1, keepdims=True)
    total = before[tm - 1:tm, :] + routed[tm - 1:tm, :]
    carry_ref[...] = total
    cnt_ref[...] = total
    meta = jnp.zeros((tm, LANES), F32)
    for col, val in ((META_E1, i1.astype(F32)), (META_E2, i2.astype(F32)), (META_POS1, pos1),
                     (META_POS2, pos2), (META_G1, 1.0 / den), (META_G2, e2 / den)):
        meta = jnp.where(lane == col, val, meta)
    meta_ref[...] = meta


def _moe_route(h, g, w_router, tm=ROW_TILE):
    t, d = h.shape
    return pl.pallas_call(
        _moe_route_kernel,
        grid=(t // tm,),
        in_specs=[pl.BlockSpec((tm, d), lambda i: (i, 0)),
                  pl.BlockSpec((1, d), lambda i: (0, 0)),
                  pl.BlockSpec((d, LANES), lambda i: (0, 0))],
        out_specs=[pl.BlockSpec((tm * ROW_TILES, LANES), lambda i: (i, 0)),
                   pl.BlockSpec((tm, LANES), lambda i: (i, 0)),
                   pl.BlockSpec((1, LANES), lambda i: (0, 0))],
        out_shape=[jax.ShapeDtypeStruct((t * ROW_TILES, LANES), F32),
                   jax.ShapeDtypeStruct((t, LANES), F32),
                   jax.ShapeDtypeStruct((1, LANES), F32)],
        scratch_shapes=[pltpu.VMEM((1, LANES), F32)],
        compiler_params=_cparams(1),
        name="moe_route",
    )(h, g, w_router)


def _moe_expert_kernel(src_s, tile_e_s, n_used_s, xn_hbm, wg_ref, wu_ref, wd_ref, y_ref,
                       rows_scr, xs_scr, acc_scr, sem, *, n_f):
    i = pl.program_id(0)
    f = pl.program_id(1)
    tm = xs_scr.shape[0]
    used = i < n_used_s[0]
    next_used = i + 1 < n_used_s[0]
    slot = i % 2
    part = tm // n_f

    def whole_tile(s):
        return pltpu.make_async_copy(xn_hbm.at[pl.ds(0, tm * ROW_TILES)], rows_scr.at[s], sem.at[s])

    @pl.when(jnp.logical_and(i == 0, f == 0))
    def _first():
        def start(r, c):
            pltpu.make_async_copy(_row_tile(xn_hbm, src_s[r]), _row_tile(rows_scr.at[0], r),
                                  sem.at[0]).start()
            return c
        lax.fori_loop(0, tm, start, 0)

    @pl.when(jnp.logical_and(used, f == 0))
    def _unpack():
        whole_tile(slot).wait()
        for s in range(ROW_TILES):
            xs_scr[:, s * LANES:(s + 1) * LANES] = _from_row_tiles(rows_scr.at[slot], tm, s).astype(BF16)
        acc_scr[...] = jnp.zeros_like(acc_scr)

    def fetch_next():
        base = (i + 1) * tm + f * part
        for r in range(part):
            pltpu.make_async_copy(_row_tile(xn_hbm, src_s[base + r]),
                                  _row_tile(rows_scr.at[1 - slot], f * part + r),
                                  sem.at[1 - slot]).start()

    @pl.when(jnp.logical_and(used, next_used))
    def _compute_and_fetch():
        fetch_next()
        acc_scr[...] += _swiglu_partial(xs_scr[...], wg_ref, wu_ref, wd_ref)

    @pl.when(jnp.logical_and(used, jnp.logical_not(next_used)))
    def _compute():
        acc_scr[...] += _swiglu_partial(xs_scr[...], wg_ref, wu_ref, wd_ref)

    @pl.when(f == n_f - 1)
    def _store():
        @pl.when(used)
        def _():
            _to_row_tiles(y_ref, acc_scr[...])

        @pl.when(jnp.logical_not(used))
        def _():
            y_ref[...] = jnp.zeros_like(y_ref)


def _moe_experts(xn3, src, tile_e, n_used, wgu, wd, layer, tm):
    n_rows = src.shape[0]
    d = wd.shape[-1]
    n_f = N_FF_TILES
    tf = wd.shape[-2] // n_f
    grid_spec = pltpu.PrefetchScalarGridSpec(
        num_scalar_prefetch=3,
        grid=(n_rows // tm, n_f),
        in_specs=[pl.BlockSpec(memory_space=pl.ANY),
                  pl.BlockSpec((1, 1, d, tf), lambda i, f, src, te, nu: (layer, te[i], 0, f)),
                  pl.BlockSpec((1, 1, d, tf), lambda i, f, src, te, nu: (layer, te[i], 0, n_f + f)),
                  pl.BlockSpec((1, 1, tf, d), lambda i, f, src, te, nu: (layer, te[i], f, 0))],
        out_specs=pl.BlockSpec((tm * ROW_TILES, LANES), lambda i, f, src, te, nu: (i, 0)),
        scratch_shapes=[pltpu.VMEM((2, tm * ROW_TILES, LANES), F32),
                        pltpu.VMEM((tm, d), BF16),
                        pltpu.VMEM((tm, d), F32),
                        pltpu.SemaphoreType.DMA((2,))],
    )
    return pl.pallas_call(
        functools.partial(_moe_expert_kernel, n_f=n_f),
        grid_spec=grid_spec,
        out_shape=jax.ShapeDtypeStruct((n_rows * ROW_TILES, LANES), F32),
        compiler_params=_cparams(2),
        name="moe_experts",
    )(src, tile_e, n_used, xn3, wgu, wgu, wd)


def _moe_combine_kernel(d1_s, d2_s, x_ref, meta_ref, gf_ref, y_hbm, o_ref, y_scr, sem, *, final_norm):
    i = pl.program_id(0)
    tm = x_ref.shape[0]
    slot = i % 2

    def fetch(tile, s):
        def start(r, c):
            pltpu.make_async_copy(_row_tile(y_hbm, d1_s[tile * tm + r]), _row_tile(y_scr.at[s, 0], r),
                                  sem.at[s]).start()
            pltpu.make_async_copy(_row_tile(y_hbm, d2_s[tile * tm + r]), _row_tile(y_scr.at[s, 1], r),
                                  sem.at[s]).start()
            return c
        lax.fori_loop(0, tm, start, 0)

    @pl.when(i == 0)
    def _first():
        fetch(0, 0)

    @pl.when(i + 1 < pl.num_programs(0))
    def _next():
        fetch(i + 1, 1 - slot)

    for which in range(2):
        pltpu.make_async_copy(y_hbm.at[pl.ds(0, tm * ROW_TILES)], y_scr.at[slot, which],
                              sem.at[slot]).wait()
    meta = meta_ref[...]
    g1 = meta[:, META_G1:META_G1 + 1]
    g2 = meta[:, META_G2:META_G2 + 1]
    for s in range(ROW_TILES):
        cols = slice(s * LANES, (s + 1) * LANES)
        o_ref[:, cols] = (x_ref[:, cols] + g1 * _from_row_tiles(y_scr.at[slot, 0], tm, s)
                          + g2 * _from_row_tiles(y_scr.at[slot, 1], tm, s))
    if final_norm:
        o_ref[...] = _rms(o_ref[...], gf_ref[...])


def _moe_combine(h, meta, g_final, y3, dest1, dest2, final_norm, tm=ROW_TILE):
    t, d = h.shape
    grid_spec = pltpu.PrefetchScalarGridSpec(
        num_scalar_prefetch=2,
        grid=(t // tm,),
        in_specs=[pl.BlockSpec((tm, d), lambda i, d1, d2: (i, 0)),
                  pl.BlockSpec((tm, LANES), lambda i, d1, d2: (i, 0)),
                  pl.BlockSpec((1, d), lambda i, d1, d2: (0, 0)),
                  pl.BlockSpec(memory_space=pl.ANY)],
        out_specs=pl.BlockSpec((tm, d), lambda i, d1, d2: (i, 0)),
        scratch_shapes=[pltpu.VMEM((2, 2, tm * ROW_TILES, LANES), F32),
                        pltpu.SemaphoreType.DMA((2,))],
    )
    return pl.pallas_call(
        functools.partial(_moe_combine_kernel, final_norm=final_norm),
        grid_spec=grid_spec,
        out_shape=jax.ShapeDtypeStruct((t, d), F32),
        compiler_params=_cparams(1),
        name="moe_combine",
    )(dest1, dest2, h, meta, g_final, y3)


def _moe(h, g, w_router, wgu, wd, layer, g_final, final_norm, tm=ROW_TILE):
    t, d = h.shape
    xn3, meta, cnt = _moe_route(h, g, w_router)
    counts = cnt[0, :N_EXPERTS].astype(jnp.int32)
    padded = (counts + tm - 1) // tm * tm
    ends = jnp.cumsum(padded)
    starts = ends - padded
    idx = meta[:, :META_G1].astype(jnp.int32)
    experts = jnp.arange(N_EXPERTS, dtype=jnp.int32)

    def group_row(e, pos):
        return pos + jnp.sum(jnp.where(e[:, None] == experts[None, :], starts[None, :], 0), axis=1)

    dest1 = group_row(idx[:, META_E1], idx[:, META_POS1])
    dest2 = group_row(idx[:, META_E2], idx[:, META_POS2])
    n_rows = 2 * t + N_EXPERTS * tm
    tok = jnp.arange(t, dtype=jnp.int32)
    src = jnp.zeros((n_rows,), jnp.int32).at[jnp.concatenate([dest1, dest2])].set(
        jnp.concatenate([tok, tok]), unique_indices=True)
    tile_start = jnp.arange(n_rows // tm, dtype=jnp.int32) * tm
    tile_e = jnp.minimum(jnp.sum((ends[None, :] <= tile_start[:, None]).astype(jnp.int32), axis=1),
                         N_EXPERTS - 1)
    n_used = (ends[-1:] // tm).astype(jnp.int32)
    y3 = _moe_experts(xn3, src, tile_e, n_used, wgu, wd, layer, tm)
    return _moe_combine(h, meta, g_final, y3, dest1, dest2, final_norm)


def _rope_tables(seq):
    pos = jnp.arange(seq, dtype=F32)
    inv_freq = 1.0 / (ROPE_THETA ** (jnp.arange(0, HEAD_DIM, 2, dtype=F32) / HEAD_DIM))
    ang = pos[:, None] * inv_freq[None, :]
    cos, sin = jnp.cos(ang), jnp.sin(ang)
    cos_t = jnp.tile(cos, (1, LANES // (HEAD_DIM // 2)))
    sin_t = jnp.tile(jnp.concatenate([-sin, sin], axis=1), (1, LANES // HEAD_DIM))
    return cos_t, sin_t


def kernel(x, a_norm, a_w_in, a_w_out, kv_norm, w_kv, b_f, b_norm, b_w_q, b_w_out, ffn_norm,
           dense_w_gate_up, dense_w_down, moe_router, moe_w_gate_up, moe_w_down, final_norm):
    batch, seq, d = x.shape
    depth = ffn_norm.shape[0]
    n_a = a_norm.shape[0]
    cos_t, sin_t = _rope_tables(seq)
    idx_w = IDX_HEADS * IDX_DIM
    dense_gu, dense_down = dense_w_gate_up.astype(BF16), dense_w_down.astype(BF16)
    moe_gu, moe_down = moe_w_gate_up.astype(BF16), moe_w_down.astype(BF16)

    h = x.reshape(batch * seq, d)
    k_sh = vx_sh = f_sh = fk_t = None
    for i in range(depth):
        if i < n_a:
            w = a_w_in[i]
            ki_w = w[:, 3 * HD + idx_w:3 * HD + idx_w + IDX_DIM]
            wi_w = w[:, 3 * HD + idx_w + IDX_DIM:]
            w_p = jnp.concatenate(
                [w[:, :3 * HD + idx_w], ki_w, ki_w, wi_w,
                 jnp.zeros((d, LANES - IDX_HEADS), w.dtype)], axis=1).astype(BF16)
            q, k, vx, qi, ki, wi = _a_proj(h, a_norm[i][None, :], w_p, cos_t, sin_t, seq)
            o = _dsa_attention(qi, wi, ki, q, k, vx, batch, seq)
            h = _out_proj(o, a_w_out[i].astype(BF16), h)
        else:
            j = i - n_a
            if k_sh is None:
                w_p = jnp.concatenate(
                    [w_kv, jnp.zeros((d, LANES - N_HEADS), w_kv.dtype)], axis=1).astype(BF16)
                b_p = jnp.concatenate([b_f, jnp.zeros((LANES - N_HEADS,), b_f.dtype)])[None, :]
                k_sh, vx_sh, f_sh = _kv_proj(h, kv_norm[None, :], w_p, b_p, seq)
                fk_t = f_sh[:, :N_HEADS].reshape(batch, seq, N_HEADS).transpose(0, 2, 1)
            q = _q_proj(h, b_norm[j][None, :], b_w_q[j].astype(BF16))
            o = _fox_attention(q, f_sh, fk_t, k_sh, vx_sh, batch, seq)
            h = _out_proj(o, b_w_out[j].astype(BF16), h)
        if i % 2 == 0:
            h = _ffn(h, ffn_norm[i][None, :], dense_gu, dense_down, i // 2)
        else:
            e = i // 2
            wr = jnp.concatenate(
                [moe_router[e], jnp.zeros((d, LANES - N_EXPERTS), moe_router.dtype)], axis=1)
            last = i == depth - 1
            h = _moe(h, ffn_norm[i][None, :], wr, moe_gu, moe_down, e, final_norm[None, :], last)
    if depth % 2 == 1:
        raise NotImplementedError("final norm is fused into the last expert mixer")
    return h.reshape(batch, seq, d)
```

```python
import functools
import math

import numpy as np
import jax
import jax.numpy as jnp
from jax import lax
from jax.experimental import pallas as pl
from jax.experimental.pallas import tpu as pltpu

N_HEADS = 16
HEAD_DIM = 64
IDX_HEADS = 8
IDX_DIM = 64
TOPK_MAX = 256
CHUNK = 64
ROPE_THETA = 10000.0
N_EXPERTS = 8
RMS_EPS = 1e-6

LANES = 128
HD = N_HEADS * HEAD_DIM
N_PAIRS = N_HEADS // 2
NEG = -1e30
LOG2E = math.log2(math.e)
VMEM_LIMIT = 52 * 1024 * 1024

ROW_TILE = 512
ATTN_TQ = 256
ATTN_TK = 1024

F32 = jnp.float32
BF16 = jnp.bfloat16


def _cparams(n_axes):
    return pltpu.CompilerParams(dimension_semantics=("arbitrary",) * n_axes,
                                vmem_limit_bytes=VMEM_LIMIT)


def _rms(x, g):
    return x * lax.rsqrt(jnp.mean(x * x, axis=-1, keepdims=True) + RMS_EPS) * g


def _half_mask(shape):
    return lax.broadcasted_iota(jnp.int32, shape, len(shape) - 1) < HEAD_DIM


def _nt_dot(a, b):
    return lax.dot_general(a, b, (((1,), (1,)), ((), ())), preferred_element_type=F32)


def _store_values_with_ones(vx_ref, p, c0):
    ones = jnp.ones((p.shape[0], LANES), BF16)
    for j in range(p.shape[1] // LANES):
        pair = (c0 // LANES) + j
        vx_ref[:, 2 * pair * LANES:(2 * pair + 1) * LANES] = p[:, j * LANES:(j + 1) * LANES].astype(BF16)
        vx_ref[:, (2 * pair + 1) * LANES:(2 * pair + 2) * LANES] = ones


def _rope_chunk(p, cos, sin_signed):
    fwd = pltpu.roll(p, LANES - 32, 1)
    bwd = pltpu.roll(p, 32, 1)
    lane = lax.broadcasted_iota(jnp.int32, p.shape, 1)
    partner = jnp.where((lane % HEAD_DIM) < HEAD_DIM // 2, fwd, bwd)
    return p * cos + partner * sin_signed


def _a_proj_kernel(x_ref, g_ref, w_ref, cos_ref, sin_ref,
                   q_ref, k_ref, vx_ref, qi_ref, ki_ref, wi_ref):
    xn = _rms(x_ref[...], g_ref[...]).astype(BF16)
    cos = cos_ref[...]
    sin = sin_ref[...]
    seg = 512

    def proj(c0, width):
        return jnp.dot(xn, w_ref[:, c0:c0 + width], preferred_element_type=F32)

    def rope_store(p, out_ref, o0, scale):
        for j in range(p.shape[1] // LANES):
            r = _rope_chunk(p[:, j * LANES:(j + 1) * LANES], cos, sin)
            if scale != 1.0:
                r = r * scale
            out_ref[:, o0 + j * LANES:o0 + (j + 1) * LANES] = r.astype(out_ref.dtype)

    for s in range(HD // seg):
        rope_store(proj(s * seg, seg), q_ref, s * seg, HEAD_DIM ** -0.5 * LOG2E)
    for s in range(HD // seg):
        rope_store(proj(HD + s * seg, seg), k_ref, s * seg, 1.0)
    for s in range(HD // seg):
        _store_values_with_ones(vx_ref, proj(2 * HD + s * seg, seg), s * seg)
    rope_store(proj(3 * HD, IDX_HEADS * IDX_DIM), qi_ref, 0, 1.0)
    tail = proj(3 * HD + IDX_HEADS * IDX_DIM, 2 * LANES)
    rope_store(tail[:, :LANES], ki_ref, 0, 1.0)
    wi_ref[...] = tail[:, LANES:] * (IDX_HEADS ** -0.5)


def _a_proj(h, g, w, cos_t, sin_t, seq, tm=ROW_TILE):
    t, d = h.shape
    n_w = w.shape[1]
    tiles_per_seq = seq // tm
    row = lambda i: (i, 0)
    tab = lambda i: (i % tiles_per_seq, 0)
    const = lambda i: (0, 0)
    return pl.pallas_call(
        _a_proj_kernel,
        grid=(t // tm,),
        in_specs=[pl.BlockSpec((tm, d), row),
                  pl.BlockSpec((1, d), const),
                  pl.BlockSpec((d, n_w), const),
                  pl.BlockSpec((tm, LANES), tab),
                  pl.BlockSpec((tm, LANES), tab)],
        out_specs=[pl.BlockSpec((tm, HD), row),
                   pl.BlockSpec((tm, HD), row),
                   pl.BlockSpec((tm, 2 * HD), row),
                   pl.BlockSpec((tm, IDX_HEADS * IDX_DIM), row),
                   pl.BlockSpec((tm, LANES), row),
                   pl.BlockSpec((tm, LANES), row)],
        out_shape=[jax.ShapeDtypeStruct((t, HD), BF16),
                   jax.ShapeDtypeStruct((t, HD), BF16),
                   jax.ShapeDtypeStruct((t, 2 * HD), BF16),
                   jax.ShapeDtypeStruct((t, IDX_HEADS * IDX_DIM), BF16),
                   jax.ShapeDtypeStruct((t, LANES), BF16),
                   jax.ShapeDtypeStruct((t, LANES), F32)],
        compiler_params=_cparams(1),
        name="a_proj",
    )(h, g, w, cos_t, sin_t)


def _kv_proj_kernel(x_ref, g_ref, w_ref, bf_ref, k_ref, vx_ref, cum_ref, carry_ref, *, tiles_per_seq):
    i = pl.program_id(0)
    tm = x_ref.shape[0]

    @pl.when(i % tiles_per_seq == 0)
    def _():
        carry_ref[...] = jnp.zeros_like(carry_ref)

    xn = _rms(x_ref[...], g_ref[...]).astype(BF16)
    seg = 512
    for s in range(HD // seg):
        k_ref[:, s * seg:(s + 1) * seg] = jnp.dot(
            xn, w_ref[:, s * seg:(s + 1) * seg], preferred_element_type=F32).astype(BF16)
    for s in range(HD // seg):
        _store_values_with_ones(
            vx_ref, jnp.dot(xn, w_ref[:, HD + s * seg:HD + (s + 1) * seg], preferred_element_type=F32),
            s * seg)
    f_logit = jnp.dot(xn, w_ref[:, 2 * HD:2 * HD + LANES], preferred_element_type=F32)
    log_f = jax.nn.log_sigmoid(f_logit + bf_ref[...])
    r_i = lax.broadcasted_iota(jnp.int32, (tm, tm), 0)
    c_i = lax.broadcasted_iota(jnp.int32, (tm, tm), 1)
    tri = jnp.where(c_i <= r_i, 1.0, 0.0).astype(BF16)
    x1 = log_f.astype(BF16)
    rem = log_f - x1.astype(F32)
    x2 = rem.astype(BF16)
    x3 = (rem - x2.astype(F32)).astype(BF16)
    cum = (jnp.dot(tri, x1, preferred_element_type=F32)
           + jnp.dot(tri, x2, preferred_element_type=F32)
           + jnp.dot(tri, x3, preferred_element_type=F32)) + carry_ref[...]
    cum_ref[...] = cum * LOG2E
    carry_ref[...] = cum[tm - 1:tm, :]


def _kv_proj(h, g, w, b_f, seq, tm=ROW_TILE):
    t, d = h.shape
    row = lambda i: (i, 0)
    const = lambda i: (0, 0)
    return pl.pallas_call(
        functools.partial(_kv_proj_kernel, tiles_per_seq=seq // tm),
        grid=(t // tm,),
        in_specs=[pl.BlockSpec((tm, d), row),
                  pl.BlockSpec((1, d), const),
                  pl.BlockSpec((d, w.shape[1]), const),
                  pl.BlockSpec((1, LANES), const)],
        out_specs=[pl.BlockSpec((tm, HD), row),
                   pl.BlockSpec((tm, 2 * HD), row),
                   pl.BlockSpec((tm, LANES), row)],
        out_shape=[jax.ShapeDtypeStruct((t, HD), BF16),
                   jax.ShapeDtypeStruct((t, 2 * HD), BF16),
                   jax.ShapeDtypeStruct((t, LANES), F32)],
        scratch_shapes=[pltpu.VMEM((1, LANES), F32)],
        compiler_params=_cparams(1),
        name="kv_proj",
    )(h, g, w, b_f)


def _q_proj_kernel(x_ref, g_ref, w_ref, q_ref):
    xn = _rms(x_ref[...], g_ref[...]).astype(BF16)
    seg = 512
    for s in range(HD // seg):
        p = jnp.dot(xn, w_ref[:, s * seg:(s + 1) * seg], preferred_element_type=F32)
        q_ref[:, s * seg:(s + 1) * seg] = (p * (HEAD_DIM ** -0.5 * LOG2E)).astype(BF16)


def _q_proj(h, g, w, tm=ROW_TILE):
    t, d = h.shape
    row = lambda i: (i, 0)
    const = lambda i: (0, 0)
    return pl.pallas_call(
        _q_proj_kernel,
        grid=(t // tm,),
        in_specs=[pl.BlockSpec((tm, d), row),
                  pl.BlockSpec((1, d), const),
                  pl.BlockSpec((d, HD), const)],
        out_specs=pl.BlockSpec((tm, HD), row),
        out_shape=jax.ShapeDtypeStruct((t, HD), BF16),
        compiler_params=_cparams(1),
        name="q_proj",
    )(h, g, w)


def _out_proj_kernel(o_ref, w_ref, h_ref, out_ref):
    out_ref[...] = h_ref[...] + jnp.dot(o_ref[...], w_ref[...], preferred_element_type=F32)


def _out_proj(o, w, h, tm=ROW_TILE):
    t, d = h.shape
    row = lambda i: (i, 0)
    const = lambda i: (0, 0)
    return pl.pallas_call(
        _out_proj_kernel,
        grid=(t // tm,),
        in_specs=[pl.BlockSpec((tm, HD), row),
                  pl.BlockSpec((HD, d), const),
                  pl.BlockSpec((tm, d), row)],
        out_specs=pl.BlockSpec((tm, d), row),
        out_shape=jax.ShapeDtypeStruct((t, d), F32),
        compiler_params=_cparams(1),
        name="out_proj",
    )(o, w, h)


def _stack_pair_queries(q_ref, q2_scr):
    tq = q_ref.shape[0]
    for j in range(N_PAIRS):
        qp = q_ref[:, j * LANES:(j + 1) * LANES]
        first = _half_mask(qp.shape)
        zero = jnp.zeros_like(qp)
        q2_scr[j, 0:tq, :] = jnp.where(first, qp, zero)
        q2_scr[j, tq:2 * tq, :] = jnp.where(first, zero, qp)


def _attn_init(m_scr, l_scr, acc_scr):
    m_scr[0] = jnp.full(m_scr.shape[1:], NEG, F32)
    l_scr[0] = jnp.zeros(l_scr.shape[1:], F32)
    acc_scr[0] = jnp.zeros(acc_scr.shape[1:], F32)


def _attn_block(q2_scr, k_ref, vx_ref, m_scr, l_scr, acc_scr, s_scr, p_scr, src, dst, bias_fn,
                row_bias_fn=None, drift_fn=None, exact_max=True, first_base=None):
    tq = p_scr.shape[0] // 2
    l_lo = l_hi = None
    for j in range(N_PAIRS):
        kp = k_ref[:, j * LANES:(j + 1) * LANES]
        sv = _nt_dot(q2_scr[j], kp)
        alphas = []
        for half in range(2):
            h = 2 * j + half
            rows = slice(half * tq, (half + 1) * tq)
            m_prev = m_scr[src, j, rows, :]
            if exact_max:
                s_scr[rows, :] = bias_fn(sv[rows], h)
                m_blk = jnp.max(s_scr[rows, :], axis=1, keepdims=True)
                if row_bias_fn is not None:
                    m_blk = m_blk + row_bias_fn(h)
                m_new = jnp.maximum(m_prev, m_blk)
                alphas.append(jnp.exp2(m_prev - m_new))
                logits = s_scr[rows, :]
            else:
                logits = bias_fn(sv[rows], h)
                if first_base is not None:
                    base = first_base(sv[rows], logits)
                    if row_bias_fn is not None:
                        base = base + row_bias_fn(h)
                    m_new = base if drift_fn is None else base + drift_fn(h)
                    alphas.append(None)
                elif drift_fn is None:
                    m_new = m_prev
                    alphas.append(None)
                else:
                    m_new = m_prev + drift_fn(h)
                    alphas.append(jnp.exp2(-drift_fn(h)))
            m_scr[dst, j, rows, :] = m_new
            shift = m_new if row_bias_fn is None else m_new - row_bias_fn(h)
            p_scr[rows, :] = jnp.exp2(logits - shift).astype(BF16)
        pv = jnp.dot(p_scr[...], vx_ref[:, 2 * j * LANES:2 * (j + 1) * LANES],
                     preferred_element_type=F32)
        for half in range(2):
            rows = slice(half * tq, (half + 1) * tq)
            acc_old = acc_scr[src, j, rows, :]
            l_old = l_scr[src, j, rows, :]
            if alphas[half] is not None:
                acc_old = alphas[half] * acc_old
                l_old = alphas[half] * l_old
            l_new = l_old + pv[rows, LANES:]
            acc_scr[dst, j, rows, :] = acc_old + pv[rows, :LANES]
            l_scr[dst, j, rows, :] = l_new
            if not exact_max:
                l_hi = l_new if l_hi is None else jnp.maximum(l_hi, l_new)
                l_lo = l_new if l_lo is None else jnp.minimum(l_lo, l_new)
    return l_lo, l_hi


def _sums_usable(l_lo, l_hi):
    return jnp.logical_and(jnp.min(l_lo) > 0.0, jnp.max(l_hi) < 1e37)


def _attn_finish(o_ref, l_scr, acc_scr, slot):
    tq = o_ref.shape[0]
    for j in range(N_PAIRS):
        o2 = acc_scr[slot, j] / l_scr[slot, j]
        o_ref[:, j * LANES:(j + 1) * LANES] = jnp.where(
            _half_mask((tq, LANES)), o2[0:tq], o2[tq:2 * tq]).astype(o_ref.dtype)


def _attn_scratch(tq, tk):
    return [pltpu.VMEM((N_PAIRS, 2 * tq, LANES), BF16),
            pltpu.VMEM((2, N_PAIRS, 2 * tq, 1), F32),
            pltpu.VMEM((2, N_PAIRS, 2 * tq, LANES), F32),
            pltpu.VMEM((2, N_PAIRS, 2 * tq, LANES), F32),
            pltpu.VMEM((2 * tq, tk), F32),
            pltpu.VMEM((2 * tq, tk), BF16),
            pltpu.SMEM((2,), jnp.int32)]


def _ordered_key_to_f32(key):
    return lax.bitcast_convert_type(key ^ ((key >> 31) & jnp.int32(0x7FFFFFFF)), F32)


def _dsa_kernel(b_s, qt_s, ph_s, kb_s, nkb_s,
                qi_ref, wi_ref, ki_ref, q_ref, k_ref, vx_ref, o_ref,
                score_scr, q2_scr, m_scr, l_scr, acc_scr, s_scr, p_scr, state, *, k_sel):
    step = pl.program_id(0)
    qt = qt_s[step]
    phase = ph_s[step]
    kb = kb_s[step]
    nkb = nkb_s[step]
    tq = q_ref.shape[0]
    tk = k_ref.shape[0]

    row = lax.broadcasted_iota(jnp.int32, (tq, 1), 0) + qt * tq
    limit = (row // CHUNK + 1) * CHUNK

    @pl.when(phase == 0)
    def _index():
        kk = ki_ref[...]
        w = wi_ref[...]
        acc = jnp.zeros((tq, tk), F32)
        for j in range(IDX_HEADS // 2):
            qp = qi_ref[:, j * LANES:(j + 1) * LANES]
            first = _half_mask(qp.shape)
            zero = jnp.zeros_like(qp)
            for half in range(2):
                h = 2 * j + half
                qh = jnp.where(first, qp, zero) if half == 0 else jnp.where(first, zero, qp)
                sc = _nt_dot(qh, kk)
                acc = acc + jnp.maximum(sc, 0.0) * w[:, h:h + 1]
        key = lax.broadcasted_iota(jnp.int32, (1, tk), 1) + kb * tk
        sc = jnp.where(key < limit, acc, -jnp.inf)
        score_scr[kb] = sc

    @pl.when(jnp.logical_and(phase == 0, kb == nkb - 1))
    def _select():
        band = 128
        one = jnp.ones((band, LANES), F32)
        zero = jnp.zeros((band, LANES), F32)

        def count(cand, strict):
            c_full = jnp.broadcast_to(cand, (tq, LANES))
            accs = []
            for r0 in range(0, tq, band):
                c_b = c_full[r0:r0 + band]

                def body(j, acc, r0=r0, c_b=c_b):
                    blk = score_scr[j, r0:r0 + band, :]
                    for c in range(tk // LANES):
                        part = blk[:, c * LANES:(c + 1) * LANES]
                        hit = (part > c_b) if strict else (part >= c_b)
                        acc = acc + jnp.where(hit, one, zero)
                    return acc

                accs.append(lax.fori_loop(0, nkb, body, zero))
            return jnp.sum(jnp.concatenate(accs, axis=0), axis=1, keepdims=True)

        def bit_body(i, r):
            cand_u = r | jnp.left_shift(jnp.int32(1), 31 - i)
            cnt = count(_ordered_key_to_f32(cand_u ^ jnp.int32(-2 ** 31)), False)
            return jnp.where(cnt >= k_sel, cand_u, r)

        r = lax.fori_loop(0, 32, bit_body, jnp.zeros((tq, 1), jnp.int32))
        few = limit <= k_sel
        thr = jnp.where(few, -jnp.inf, _ordered_key_to_f32(r ^ jnp.int32(-2 ** 31)))
        need = k_sel - count(thr, True)

        sub = 256
        r_i = lax.broadcasted_iota(jnp.int32, (sub, sub), 0)
        c_i = lax.broadcasted_iota(jnp.int32, (sub, sub), 1)
        tri = jnp.where(r_i <= c_i, 1.0, 0.0).astype(BF16)

        def sel_body(j, carry):
            for c in range(tk // sub):
                cols = slice(c * sub, (c + 1) * sub)
                blk = score_scr[j, :, cols]
                key = lax.broadcasted_iota(jnp.int32, (1, sub), 1) + (j * tk + c * sub)
                eq = blk == thr
                rank = carry + jnp.dot(jnp.where(eq, 1.0, 0.0).astype(BF16), tri,
                                       preferred_element_type=F32)
                bias = jnp.where(eq, jnp.where(rank <= need, 0.0, NEG),
                                 jnp.where(blk > thr, 0.0, NEG))
                score_scr[j, :, cols] = jnp.where(key < limit, bias, NEG)
                carry = rank[:, sub - 1:sub]
            return carry

        lax.fori_loop(0, nkb, sel_body, jnp.zeros((tq, 1), F32))

    attend = functools.partial(_attn_block, q2_scr, k_ref, vx_ref, m_scr, l_scr, acc_scr, s_scr, p_scr,
                               bias_fn=lambda sv, h: sv + score_scr[kb])

    state[1] = 0

    @pl.when(jnp.logical_and(phase == 1, kb == 0))
    def _init():
        _stack_pair_queries(q_ref, q2_scr)
        _attn_init(m_scr, l_scr, acc_scr)
        state[0] = 0
        extremes = attend(src=0, dst=1, exact_max=False,
                          first_base=lambda raw, biased: jnp.max(raw[:, 0:LANES], axis=1, keepdims=True))
        state[1] = jnp.where(_sums_usable(*extremes), 0, 1)

    @pl.when(jnp.logical_and(phase == 1, kb > 0))
    def _one_pass():
        src = state[0]
        extremes = attend(src=src, dst=1 - src, exact_max=False)
        state[1] = jnp.where(_sums_usable(*extremes), 0, 1)

    @pl.when(jnp.logical_and(phase == 1, state[1] == 1))
    def _two_pass():
        src = state[0]
        attend(src=src, dst=1 - src)

    @pl.when(phase == 1)
    def _advance():
        state[0] = 1 - state[0]

    @pl.when(jnp.logical_and(phase == 1, kb == nkb - 1))
    def _finish():
        _attn_finish(o_ref, l_scr, acc_scr, state[0])


def _dsa_schedule(batch, seq, tq, tk):
    rows = []
    for b in range(batch):
        for qt in range(seq // tq):
            nkb = -(-((qt + 1) * tq) // tk)
            for phase in range(2):
                for kb in range(nkb):
                    rows.append((b, qt, phase, kb, nkb))
    return [jnp.asarray(c, jnp.int32) for c in np.asarray(rows, np.int32).T]


def _dsa_attention(qi, wi, ki, q, k, vx, batch, seq, tq=ATTN_TQ, tk=ATTN_TK):
    t = q.shape[0]
    qt_per_b = seq // tq
    kb_per_b = seq // tk
    sched = _dsa_schedule(batch, seq, tq, tk)
    k_sel = min(TOPK_MAX, seq // 4)

    def q_row(i, b, qt, ph, kb, nkb):
        return (b[i] * qt_per_b + qt[i], 0)

    def ki_row(i, b, qt, ph, kb, nkb):
        return (b[i] * kb_per_b + jnp.where(ph[i] == 0, kb[i], nkb[i] - 1), 0)

    def kv_row(i, b, qt, ph, kb, nkb):
        return (b[i] * kb_per_b + jnp.where(ph[i] == 0, 0, kb[i]), 0)

    grid_spec = pltpu.PrefetchScalarGridSpec(
        num_scalar_prefetch=5,
        grid=(int(sched[0].shape[0]),),
        in_specs=[pl.BlockSpec((tq, IDX_HEADS * IDX_DIM), q_row),
                  pl.BlockSpec((tq, LANES), q_row),
                  pl.BlockSpec((tk, LANES), ki_row),
                  pl.BlockSpec((tq, HD), q_row),
                  pl.BlockSpec((tk, HD), kv_row),
                  pl.BlockSpec((tk, 2 * HD), kv_row)],
        out_specs=pl.BlockSpec((tq, HD), q_row),
        scratch_shapes=[pltpu.VMEM((kb_per_b, tq, tk), F32)] + _attn_scratch(tq, tk),
    )
    return pl.pallas_call(
        functools.partial(_dsa_kernel, k_sel=k_sel),
        grid_spec=grid_spec,
        out_shape=jax.ShapeDtypeStruct((t, HD), BF16),
        compiler_params=_cparams(1),
        name="dsa_attention",
    )(*sched, qi, wi, ki, q, k, vx)


def _fox_kernel(b_s, qt_s, kb_s, nkb_s, q_ref, fq_ref, k_ref, vx_ref, fk_ref, o_ref,
                q2_scr, m_scr, l_scr, acc_scr, s_scr, p_scr, state, fq_scr):
    step = pl.program_id(0)
    qt = qt_s[step]
    kb = kb_s[step]
    nkb = nkb_s[step]
    tq = q_ref.shape[0]
    tk = k_ref.shape[0]

    @pl.when(kb == 0)
    def _init():
        _stack_pair_queries(q_ref, q2_scr)
        _attn_init(m_scr, l_scr, acc_scr)
        state[0] = 0
        fq = fq_ref[...]
        for h in range(N_HEADS):
            fq_scr[h] = fq[:, h:h + 1]

    fk = fk_ref[0]
    attend = functools.partial(_attn_block, q2_scr, k_ref, vx_ref, m_scr, l_scr, acc_scr, s_scr, p_scr,
                               row_bias_fn=lambda h: fq_scr[h])
    decay = lambda sv, h: sv - fk[h:h + 1, :]
    diagonal = (kb + 1) * tk > qt * tq
    state[1] = 1

    def masked_decay():
        row = lax.broadcasted_iota(jnp.int32, (tq, 1), 0) + qt * tq
        key = lax.broadcasted_iota(jnp.int32, (1, tk), 1) + kb * tk
        causal = jnp.where(key <= row, 0.0, NEG)
        return lambda sv, h: sv + (causal - fk[h:h + 1, :])

    @pl.when(jnp.logical_and(jnp.logical_not(diagonal), kb > 0))
    def _one_pass():
        src = state[0]
        extremes = attend(src=src, dst=1 - src, bias_fn=decay, exact_max=False,
                          drift_fn=lambda h: fk[h:h + 1, 0:1] - fk[h:h + 1, tk - 1:tk])
        state[1] = jnp.where(_sums_usable(*extremes), 0, 1)

    @pl.when(jnp.logical_and(jnp.logical_not(diagonal), kb == 0))
    def _one_pass_first():
        extremes = attend(src=0, dst=1, bias_fn=decay, exact_max=False,
                          first_base=lambda raw, biased: biased[:, 0:1],
                          drift_fn=lambda h: fk[h:h + 1, 0:1] - fk[h:h + 1, tk - 1:tk])
        state[1] = jnp.where(_sums_usable(*extremes), 0, 1)

    @pl.when(jnp.logical_and(diagonal, kb > 0))
    def _one_pass_masked():
        src = state[0]
        extremes = attend(src=src, dst=1 - src, bias_fn=masked_decay(), exact_max=False,
                          drift_fn=lambda h: fk[h:h + 1, 0:1] - fq_scr[h])
        state[1] = jnp.where(_sums_usable(*extremes), 0, 1)

    @pl.when(jnp.logical_and(jnp.logical_not(diagonal), state[1] == 1))
    def _two_pass():
        src = state[0]
        attend(src=src, dst=1 - src, bias_fn=decay)

    @pl.when(jnp.logical_and(diagonal, state[1] == 1))
    def _two_pass_masked():
        src = state[0]
        attend(src=src, dst=1 - src, bias_fn=masked_decay())

    state[0] = 1 - state[0]

    @pl.when(kb == nkb - 1)
    def _finish():
        _attn_finish(o_ref, l_scr, acc_scr, state[0])


def _fox_schedule(batch, seq, tq, tk):
    rows = []
    for b in range(batch):
        for qt in range(seq // tq):
            nkb = -(-((qt + 1) * tq) // tk)
            for kb in range(nkb):
                rows.append((b, qt, kb, nkb))
    return [jnp.asarray(c, jnp.int32) for c in np.asarray(rows, np.int32).T]


def _fox_attention(q, fq, fk_t, k, vx, batch, seq, tq=ATTN_TQ, tk=ATTN_TK):
    t = q.shape[0]
    qt_per_b = seq // tq
    kb_per_b = seq // tk
    sched = _fox_schedule(batch, seq, tq, tk)

    def q_row(i, b, qt, kb, nkb):
        return (b[i] * qt_per_b + qt[i], 0)

    def kv_row(i, b, qt, kb, nkb):
        return (b[i] * kb_per_b + kb[i], 0)

    def fk_row(i, b, qt, kb, nkb):
        return (b[i], 0, kb[i])

    grid_spec = pltpu.PrefetchScalarGridSpec(
        num_scalar_prefetch=4,
        grid=(int(sched[0].shape[0]),),
        in_specs=[pl.BlockSpec((tq, HD), q_row),
                  pl.BlockSpec((tq, LANES), q_row),
                  pl.BlockSpec((tk, HD), kv_row),
                  pl.BlockSpec((tk, 2 * HD), kv_row),
                  pl.BlockSpec((1, N_HEADS, tk), fk_row)],
        out_specs=pl.BlockSpec((tq, HD), q_row),
        scratch_shapes=_attn_scratch(tq, tk) + [pltpu.VMEM((N_HEADS, tq, 1), F32)],
    )
    return pl.pallas_call(
        _fox_kernel,
        grid_spec=grid_spec,
        out_shape=jax.ShapeDtypeStruct((t, HD), BF16),
        compiler_params=_cparams(1),
        name="fox_attention",
    )(*sched, q, fq, k, vx, fk_t)


N_FF_TILES = 2


def _swiglu_partial(xn, wg_ref, wu_ref, wd_ref):
    lead = (0,) * (len(wg_ref.shape) - 2)
    g = jnp.dot(xn, wg_ref[lead], preferred_element_type=F32)
    u = jnp.dot(xn, wu_ref[lead], preferred_element_type=F32)
    a = g * jax.nn.sigmoid(g) * u
    return jnp.dot(a.astype(BF16), wd_ref[lead], preferred_element_type=F32)


def _ffn_kernel(x_ref, g_ref, wg_ref, wu_ref, wd_ref, o_ref, xn_scr, acc_scr):
    f = pl.program_id(1)

    @pl.when(f == 0)
    def _():
        xn_scr[...] = _rms(x_ref[...], g_ref[...]).astype(BF16)
        acc_scr[...] = jnp.zeros_like(acc_scr)

    acc_scr[...] += _swiglu_partial(xn_scr[...], wg_ref, wu_ref, wd_ref)

    @pl.when(f == pl.num_programs(1) - 1)
    def _():
        o_ref[...] = x_ref[...] + acc_scr[...]


def _ffn(h, g, wgu, wd, layer, tm=ROW_TILE):
    t, d = h.shape
    n_f = N_FF_TILES
    tf = wd.shape[1] // n_f
    return pl.pallas_call(
        _ffn_kernel,
        grid=(t // tm, n_f),
        in_specs=[pl.BlockSpec((tm, d), lambda i, f: (i, 0)),
                  pl.BlockSpec((1, d), lambda i, f: (0, 0)),
                  pl.BlockSpec((1, d, tf), lambda i, f: (layer, 0, f)),
                  pl.BlockSpec((1, d, tf), lambda i, f: (layer, 0, n_f + f)),
                  pl.BlockSpec((1, tf, d), lambda i, f: (layer, f, 0))],
        out_specs=pl.BlockSpec((tm, d), lambda i, f: (i, 0)),
        out_shape=jax.ShapeDtypeStruct((t, d), F32),
        scratch_shapes=[pltpu.VMEM((tm, d), BF16), pltpu.VMEM((tm, d), F32)],
        compiler_params=_cparams(2),
        name="ffn_dense",
    )(h, g, wgu, wgu, wd)


META_E1, META_E2, META_POS1, META_POS2, META_G1, META_G2 = range(6)
ROW_TILES = 8


def _to_row_tiles(ref, x):
    rows = x.shape[0]
    for s in range(ROW_TILES):
        ref[pl.ds(s, rows, stride=ROW_TILES), :] = x[:, s * LANES:(s + 1) * LANES]


def _from_row_tiles(ref, rows, s):
    return ref[pl.ds(s, rows, stride=ROW_TILES), :]


def _row_tile(ref, r):
    return ref.at[pl.ds(pl.multiple_of(r * ROW_TILES, ROW_TILES), ROW_TILES)]


def _moe_route_kernel(x_ref, g_ref, wr_ref, xn_ref, meta_ref, cnt_ref, carry_ref):
    i = pl.program_id(0)
    tm = x_ref.shape[0]

    @pl.when(i == 0)
    def _():
        carry_ref[...] = jnp.zeros_like(carry_ref)

    xn = _rms(x_ref[...], g_ref[...])
    _to_row_tiles(xn_ref, xn)
    logits = jnp.dot(xn, wr_ref[...], preferred_element_type=F32, precision=lax.Precision.HIGHEST)
    lane = lax.broadcasted_iota(jnp.int32, (tm, LANES), 1)
    lg = jnp.where(lane < N_EXPERTS, logits, -jnp.inf)
    m1 = jnp.max(lg, axis=1, keepdims=True)
    i1 = jnp.min(jnp.where(lg == m1, lane, LANES), axis=1, keepdims=True)
    lg2 = jnp.where(lane == i1, -jnp.inf, lg)
    m2 = jnp.max(lg2, axis=1, keepdims=True)
    i2 = jnp.min(jnp.where(lg2 == m2, lane, LANES), axis=1, keepdims=True)
    e2 = jnp.exp(m2 - m1)
    den = 1.0 + e2
    routed = jnp.where(jnp.logical_or(lane == i1, lane == i2), 1.0, 0.0)
    r_i = lax.broadcasted_iota(jnp.int32, (tm, tm), 0)
    c_i = lax.broadcasted_iota(jnp.int32, (tm, tm), 1)
    tri = jnp.where(c_i < r_i, 1.0, 0.0).astype(BF16)
    before = jnp.dot(tri, routed.astype(BF16), preferred_element_type=F32) + carry_ref[...]
    pos1 = jnp.sum(jnp.where(lane == i1, before, 0.0), axis=1, keepdims=True)
    pos2 = jnp.sum(jnp.where(lane == i2, before, 0.0), axis=1, keepdims=True)
    total = before[tm - 1:tm, :] + routed[tm - 1:tm, :]
    carry_ref[...] = total
    cnt_ref[...] = total
    meta = jnp.zeros((tm, LANES), F32)
    for col, val in ((META_E1, i1.astype(F32)), (META_E2, i2.astype(F32)), (META_POS1, pos1),
                     (META_POS2, pos2), (META_G1, 1.0 / den), (META_G2, e2 / den)):
        meta = jnp.where(lane == col, val, meta)
    meta_ref[...] = meta


def _moe_route(h, g, w_router, tm=ROW_TILE):
    t, d = h.shape
    return pl.pallas_call(
        _moe_route_kernel,
        grid=(t // tm,),
        in_specs=[pl.BlockSpec((tm, d), lambda i: (i, 0)),
                  pl.BlockSpec((1, d), lambda i: (0, 0)),
                  pl.BlockSpec((d, LANES), lambda i: (0, 0))],
        out_specs=[pl.BlockSpec((tm * ROW_TILES, LANES), lambda i: (i, 0)),
                   pl.BlockSpec((tm, LANES), lambda i: (i, 0)),
                   pl.BlockSpec((1, LANES), lambda i: (0, 0))],
        out_shape=[jax.ShapeDtypeStruct((t * ROW_TILES, LANES), F32),
                   jax.ShapeDtypeStruct((t, LANES), F32),
                   jax.ShapeDtypeStruct((1, LANES), F32)],
        scratch_shapes=[pltpu.VMEM((1, LANES), F32)],
        compiler_params=_cparams(1),
        name="moe_route",
    )(h, g, w_router)


def _moe_expert_kernel(src_s, tile_e_s, n_used_s, xn_hbm, wg_ref, wu_ref, wd_ref, y_ref,
                       rows_scr, xs_scr, acc_scr, sem, *, n_f):
    i = pl.program_id(0)
    f = pl.program_id(1)
    tm = xs_scr.shape[0]
    used = i < n_used_s[0]
    next_used = i + 1 < n_used_s[0]
    slot = i % 2
    part = tm // n_f

    def whole_tile(s):
        return pltpu.make_async_copy(xn_hbm.at[pl.ds(0, tm * ROW_TILES)], rows_scr.at[s], sem.at[s])

    @pl.when(jnp.logical_and(i == 0, f == 0))
    def _first():
        def start(r, c):
            pltpu.make_async_copy(_row_tile(xn_hbm, src_s[r]), _row_tile(rows_scr.at[0], r),
                                  sem.at[0]).start()
            return c
        lax.fori_loop(0, tm, start, 0)

    @pl.when(jnp.logical_and(used, f == 0))
    def _unpack():
        whole_tile(slot).wait()
        for s in range(ROW_TILES):
            xs_scr[:, s * LANES:(s + 1) * LANES] = _from_row_tiles(rows_scr.at[slot], tm, s).astype(BF16)
        acc_scr[...] = jnp.zeros_like(acc_scr)

    def fetch_next():
        base = (i + 1) * tm + f * part
        for r in range(part):
            pltpu.make_async_copy(_row_tile(xn_hbm, src_s[base + r]),
                                  _row_tile(rows_scr.at[1 - slot], f * part + r),
                                  sem.at[1 - slot]).start()

    @pl.when(jnp.logical_and(used, next_used))
    def _compute_and_fetch():
        fetch_next()
        acc_scr[...] += _swiglu_partial(xs_scr[...], wg_ref, wu_ref, wd_ref)

    @pl.when(jnp.logical_and(used, jnp.logical_not(next_used)))
    def _compute():
        acc_scr[...] += _swiglu_partial(xs_scr[...], wg_ref, wu_ref, wd_ref)

    @pl.when(f == n_f - 1)
    def _store():
        @pl.when(used)
        def _():
            _to_row_tiles(y_ref, acc_scr[...])

        @pl.when(jnp.logical_not(used))
        def _():
            y_ref[...] = jnp.zeros_like(y_ref)


def _moe_experts(xn3, src, tile_e, n_used, wgu, wd, layer, tm):
    n_rows = src.shape[0]
    d = wd.shape[-1]
    n_f = N_FF_TILES
    tf = wd.shape[-2] // n_f
    grid_spec = pltpu.PrefetchScalarGridSpec(
        num_scalar_prefetch=3,
        grid=(n_rows // tm, n_f),
        in_specs=[pl.BlockSpec(memory_space=pl.ANY),
                  pl.BlockSpec((1, 1, d, tf), lambda i, f, src, te, nu: (layer, te[i], 0, f)),
                  pl.BlockSpec((1, 1, d, tf), lambda i, f, src, te, nu: (layer, te[i], 0, n_f + f)),
                  pl.BlockSpec((1, 1, tf, d), lambda i, f, src, te, nu: (layer, te[i], f, 0))],
        out_specs=pl.BlockSpec((tm * ROW_TILES, LANES), lambda i, f, src, te, nu: (i, 0)),
        scratch_shapes=[pltpu.VMEM((2, tm * ROW_TILES, LANES), F32),
                        pltpu.VMEM((tm, d), BF16),
                        pltpu.VMEM((tm, d), F32),
                        pltpu.SemaphoreType.DMA((2,))],
    )
    return pl.pallas_call(
        functools.partial(_moe_expert_kernel, n_f=n_f),
        grid_spec=grid_spec,
        out_shape=jax.ShapeDtypeStruct((n_rows * ROW_TILES, LANES), F32),
        compiler_params=_cparams(2),
        name="moe_experts",
    )(src, tile_e, n_used, xn3, wgu, wgu, wd)


def _moe_combine_kernel(d1_s, d2_s, x_ref, meta_ref, gf_ref, y_hbm, o_ref, y_scr, sem, *, final_norm):
    i = pl.program_id(0)
    tm = x_ref.shape[0]
    slot = i % 2

    def fetch_row(tile, s, r):
        pltpu.make_async_copy(_row_tile(y_hbm, d1_s[tile * tm + r]), _row_tile(y_scr.at[s, 0], r),
                              sem.at[s]).start()
        pltpu.make_async_copy(_row_tile(y_hbm, d2_s[tile * tm + r]), _row_tile(y_scr.at[s, 1], r),
                              sem.at[s]).start()

    @pl.when(i == 0)
    def _first():
        def start(r, c):
            fetch_row(0, 0, r)
            return c
        lax.fori_loop(0, tm, start, 0)

    for which in range(2):
        pltpu.make_async_copy(y_hbm.at[pl.ds(0, tm * ROW_TILES)], y_scr.at[slot, which],
                              sem.at[slot]).wait()

    def combine():
        meta = meta_ref[...]
        g1 = meta[:, META_G1:META_G1 + 1]
        g2 = meta[:, META_G2:META_G2 + 1]
        for s in range(ROW_TILES):
            cols = slice(s * LANES, (s + 1) * LANES)
            o_ref[:, cols] = (x_ref[:, cols] + g1 * _from_row_tiles(y_scr.at[slot, 0], tm, s)
                              + g2 * _from_row_tiles(y_scr.at[slot, 1], tm, s))
        if final_norm:
            o_ref[...] = _rms(o_ref[...], gf_ref[...])

    @pl.when(i + 1 < pl.num_programs(0))
    def _combine_and_fetch():
        for r in range(tm):
            fetch_row(i + 1, 1 - slot, r)
        combine()

    @pl.when(i + 1 == pl.num_programs(0))
    def _combine_last():
        combine()


def _moe_combine(h, meta, g_final, y3, dest1, dest2, final_norm, tm=ROW_TILE):
    t, d = h.shape
    grid_spec = pltpu.PrefetchScalarGridSpec(
        num_scalar_prefetch=2,
        grid=(t // tm,),
        in_specs=[pl.BlockSpec((tm, d), lambda i, d1, d2: (i, 0)),
                  pl.BlockSpec((tm, LANES), lambda i, d1, d2: (i, 0)),
                  pl.BlockSpec((1, d), lambda i, d1, d2: (0, 0)),
                  pl.BlockSpec(memory_space=pl.ANY)],
        out_specs=pl.BlockSpec((tm, d), lambda i, d1, d2: (i, 0)),
        scratch_shapes=[pltpu.VMEM((2, 2, tm * ROW_TILES, LANES), F32),
                        pltpu.SemaphoreType.DMA((2,))],
    )
    return pl.pallas_call(
        functools.partial(_moe_combine_kernel, final_norm=final_norm),
        grid_spec=grid_spec,
        out_shape=jax.ShapeDtypeStruct((t, d), F32),
        compiler_params=_cparams(1),
        name="moe_combine",
    )(dest1, dest2, h, meta, g_final, y3)


def _moe(h, g, w_router, wgu, wd, layer, g_final, final_norm, tm=ROW_TILE):
    t, d = h.shape
    xn3, meta, cnt = _moe_route(h, g, w_router)
    counts = cnt[0, :N_EXPERTS].astype(jnp.int32)
    padded = (counts + tm - 1) // tm * tm
    ends = jnp.cumsum(padded)
    starts = ends - padded
    idx = meta[:, :META_G1].astype(jnp.int32)
    experts = jnp.arange(N_EXPERTS, dtype=jnp.int32)

    def group_row(e, pos):
        return pos + jnp.sum(jnp.where(e[:, None] == experts[None, :], starts[None, :], 0), axis=1)

    dest1 = group_row(idx[:, META_E1], idx[:, META_POS1])
    dest2 = group_row(idx[:, META_E2], idx[:, META_POS2])
    n_rows = 2 * t + N_EXPERTS * tm
    tok = jnp.arange(t, dtype=jnp.int32)
    src = jnp.zeros((n_rows,), jnp.int32).at[jnp.concatenate([dest1, dest2])].set(
        jnp.concatenate([tok, tok]), unique_indices=True)
    tile_start = jnp.arange(n_rows // tm, dtype=jnp.int32) * tm
    tile_e = jnp.minimum(jnp.sum((ends[None, :] <= tile_start[:, None]).astype(jnp.int32), axis=1),
                         N_EXPERTS - 1)
    n_used = (ends[-1:] // tm).astype(jnp.int32)
    y3 = _moe_experts(xn3, src, tile_e, n_used, wgu, wd, layer, tm)
    return _moe_combine(h, meta, g_final, y3, dest1, dest2, final_norm)


def _rope_tables(seq):
    pos = jnp.arange(seq, dtype=F32)
    inv_freq = 1.0 / (ROPE_THETA ** (jnp.arange(0, HEAD_DIM, 2, dtype=F32) / HEAD_DIM))
    ang = pos[:, None] * inv_freq[None, :]
    cos, sin = jnp.cos(ang), jnp.sin(ang)
    cos_t = jnp.tile(cos, (1, LANES // (HEAD_DIM // 2)))
    sin_t = jnp.tile(jnp.concatenate([-sin, sin], axis=1), (1, LANES // HEAD_DIM))
    return cos_t, sin_t


def kernel(x, a_norm, a_w_in, a_w_out, kv_norm, w_kv, b_f, b_norm, b_w_q, b_w_out, ffn_norm,
           dense_w_gate_up, dense_w_down, moe_router, moe_w_gate_up, moe_w_down, final_norm):
    batch, seq, d = x.shape
    depth = ffn_norm.shape[0]
    n_a = a_norm.shape[0]
    cos_t, sin_t = _rope_tables(seq)
    idx_w = IDX_HEADS * IDX_DIM
    dense_gu, dense_down = dense_w_gate_up.astype(BF16), dense_w_down.astype(BF16)
    moe_gu, moe_down = moe_w_gate_up.astype(BF16), moe_w_down.astype(BF16)

    h = x.reshape(batch * seq, d)
    k_sh = vx_sh = f_sh = fk_t = None
    for i in range(depth):
        if i < n_a:
            w = a_w_in[i]
            ki_w = w[:, 3 * HD + idx_w:3 * HD + idx_w + IDX_DIM]
            wi_w = w[:, 3 * HD + idx_w + IDX_DIM:]
            w_p = jnp.concatenate(
                [w[:, :3 * HD + idx_w], ki_w, ki_w, wi_w,
                 jnp.zeros((d, LANES - IDX_HEADS), w.dtype)], axis=1).astype(BF16)
            q, k, vx, qi, ki, wi = _a_proj(h, a_norm[i][None, :], w_p, cos_t, sin_t, seq)
            o = _dsa_attention(qi, wi, ki, q, k, vx, batch, seq)
            h = _out_proj(o, a_w_out[i].astype(BF16), h)
        else:
            j = i - n_a
            if k_sh is None:
                w_p = jnp.concatenate(
                    [w_kv, jnp.zeros((d, LANES - N_HEADS), w_kv.dtype)], axis=1).astype(BF16)
                b_p = jnp.concatenate([b_f, jnp.zeros((LANES - N_HEADS,), b_f.dtype)])[None, :]
                k_sh, vx_sh, f_sh = _kv_proj(h, kv_norm[None, :], w_p, b_p, seq)
                fk_t = f_sh[:, :N_HEADS].reshape(batch, seq, N_HEADS).transpose(0, 2, 1)
            q = _q_proj(h, b_norm[j][None, :], b_w_q[j].astype(BF16))
            o = _fox_attention(q, f_sh, fk_t, k_sh, vx_sh, batch, seq)
            h = _out_proj(o, b_w_out[j].astype(BF16), h)
        if i % 2 == 0:
            h = _ffn(h, ffn_norm[i][None, :], dense_gu, dense_down, i // 2)
        else:
            e = i // 2
            wr = jnp.concatenate(
                [moe_router[e], jnp.zeros((d, LANES - N_EXPERTS), moe_router.dtype)], axis=1)
            last = i == depth - 1
            h = _moe(h, ffn_norm[i][None, :], wr, moe_gu, moe_down, e, final_norm[None, :], last)
    if depth % 2 == 1:
        raise NotImplementedError("final norm is fused into the last expert mixer")
    return h.reshape(batch, seq, d)
```

```python
import functools
import math

import numpy as np
import jax
import jax.numpy as jnp
from jax import lax
from jax.experimental import pallas as pl
from jax.experimental.pallas import tpu as pltpu

N_HEADS = 16
HEAD_DIM = 64
IDX_HEADS = 8
IDX_DIM = 64
TOPK_MAX = 256
CHUNK = 64
ROPE_THETA = 10000.0
N_EXPERTS = 8
RMS_EPS = 1e-6

LANES = 128
HD = N_HEADS * HEAD_DIM
N_PAIRS = N_HEADS // 2
NEG = -1e30
LOG2E = math.log2(math.e)
VMEM_LIMIT = 52 * 1024 * 1024

ROW_TILE = 512
ATTN_TQ = 256
ATTN_TK = 1024

F32 = jnp.float32
BF16 = jnp.bfloat16


def _cparams(n_axes):
    return pltpu.CompilerParams(dimension_semantics=("arbitrary",) * n_axes,
                                vmem_limit_bytes=VMEM_LIMIT)


def _rms(x, g):
    return x * lax.rsqrt(jnp.mean(x * x, axis=-1, keepdims=True) + RMS_EPS) * g


def _half_mask(shape):
    return lax.broadcasted_iota(jnp.int32, shape, len(shape) - 1) < HEAD_DIM


def _nt_dot(a, b):
    return lax.dot_general(a, b, (((1,), (1,)), ((), ())), preferred_element_type=F32)


def _store_values_with_ones(vx_ref, p, c0):
    ones = jnp.ones((p.shape[0], LANES), BF16)
    for j in range(p.shape[1] // LANES):
        pair = (c0 // LANES) + j
        vx_ref[:, 2 * pair * LANES:(2 * pair + 1) * LANES] = p[:, j * LANES:(j + 1) * LANES].astype(BF16)
        vx_ref[:, (2 * pair + 1) * LANES:(2 * pair + 2) * LANES] = ones


def _rope_chunk(p, cos, sin_signed):
    fwd = pltpu.roll(p, LANES - 32, 1)
    bwd = pltpu.roll(p, 32, 1)
    lane = lax.broadcasted_iota(jnp.int32, p.shape, 1)
    partner = jnp.where((lane % HEAD_DIM) < HEAD_DIM // 2, fwd, bwd)
    return p * cos + partner * sin_signed


def _a_proj_kernel(x_ref, g_ref, w_ref, cos_ref, sin_ref,
                   q_ref, k_ref, vx_ref, qi_ref, ki_ref, wi_ref):
    xn = _rms(x_ref[...], g_ref[...]).astype(BF16)
    cos = cos_ref[...]
    sin = sin_ref[...]
    seg = 512

    def proj(c0, width):
        return jnp.dot(xn, w_ref[:, c0:c0 + width], preferred_element_type=F32)

    def rope_store(p, out_ref, o0, scale):
        for j in range(p.shape[1] // LANES):
            r = _rope_chunk(p[:, j * LANES:(j + 1) * LANES], cos, sin)
            if scale != 1.0:
                r = r * scale
            out_ref[:, o0 + j * LANES:o0 + (j + 1) * LANES] = r.astype(out_ref.dtype)

    for s in range(HD // seg):
        rope_store(proj(s * seg, seg), q_ref, s * seg, HEAD_DIM ** -0.5 * LOG2E)
    for s in range(HD // seg):
        rope_store(proj(HD + s * seg, seg), k_ref, s * seg, 1.0)
    for s in range(HD // seg):
        _store_values_with_ones(vx_ref, proj(2 * HD + s * seg, seg), s * seg)
    rope_store(proj(3 * HD, IDX_HEADS * IDX_DIM), qi_ref, 0, 1.0)
    tail = proj(3 * HD + IDX_HEADS * IDX_DIM, 2 * LANES)
    rope_store(tail[:, :LANES], ki_ref, 0, 1.0)
    wi_ref[...] = tail[:, LANES:] * (IDX_HEADS ** -0.5)


def _a_proj(h, g, w, cos_t, sin_t, seq, tm=ROW_TILE):
    t, d = h.shape
    n_w = w.shape[1]
    tiles_per_seq = seq // tm
    row = lambda i: (i, 0)
    tab = lambda i: (i % tiles_per_seq, 0)
    const = lambda i: (0, 0)
    return pl.pallas_call(
        _a_proj_kernel,
        grid=(t // tm,),
        in_specs=[pl.BlockSpec((tm, d), row),
                  pl.BlockSpec((1, d), const),
                  pl.BlockSpec((d, n_w), const),
                  pl.BlockSpec((tm, LANES), tab),
                  pl.BlockSpec((tm, LANES), tab)],
        out_specs=[pl.BlockSpec((tm, HD), row),
                   pl.BlockSpec((tm, HD), row),
                   pl.BlockSpec((tm, 2 * HD), row),
                   pl.BlockSpec((tm, IDX_HEADS * IDX_DIM), row),
                   pl.BlockSpec((tm, LANES), row),
                   pl.BlockSpec((tm, LANES), row)],
        out_shape=[jax.ShapeDtypeStruct((t, HD), BF16),
                   jax.ShapeDtypeStruct((t, HD), BF16),
                   jax.ShapeDtypeStruct((t, 2 * HD), BF16),
                   jax.ShapeDtypeStruct((t, IDX_HEADS * IDX_DIM), BF16),
                   jax.ShapeDtypeStruct((t, LANES), BF16),
                   jax.ShapeDtypeStruct((t, LANES), F32)],
        compiler_params=_cparams(1),
        name="a_proj",
    )(h, g, w, cos_t, sin_t)


def _kv_proj_kernel(x_ref, g_ref, w_ref, bf_ref, k_ref, vx_ref, cum_ref, carry_ref, *, tiles_per_seq):
    i = pl.program_id(0)
    tm = x_ref.shape[0]

    @pl.when(i % tiles_per_seq == 0)
    def _():
        carry_ref[...] = jnp.zeros_like(carry_ref)

    xn = _rms(x_ref[...], g_ref[...]).astype(BF16)
    seg = 512
    for s in range(HD // seg):
        k_ref[:, s * seg:(s + 1) * seg] = jnp.dot(
            xn, w_ref[:, s * seg:(s + 1) * seg], preferred_element_type=F32).astype(BF16)
    for s in range(HD // seg):
        _store_values_with_ones(
            vx_ref, jnp.dot(xn, w_ref[:, HD + s * seg:HD + (s + 1) * seg], preferred_element_type=F32),
            s * seg)
    f_logit = jnp.dot(xn, w_ref[:, 2 * HD:2 * HD + LANES], preferred_element_type=F32)
    log_f = jax.nn.log_sigmoid(f_logit + bf_ref[...])
    r_i = lax.broadcasted_iota(jnp.int32, (tm, tm), 0)
    c_i = lax.broadcasted_iota(jnp.int32, (tm, tm), 1)
    tri = jnp.where(c_i <= r_i, 1.0, 0.0).astype(BF16)
    x1 = log_f.astype(BF16)
    rem = log_f - x1.astype(F32)
    x2 = rem.astype(BF16)
    x3 = (rem - x2.astype(F32)).astype(BF16)
    cum = (jnp.dot(tri, x1, preferred_element_type=F32)
           + jnp.dot(tri, x2, preferred_element_type=F32)
           + jnp.dot(tri, x3, preferred_element_type=F32)) + carry_ref[...]
    cum_ref[...] = cum * LOG2E
    carry_ref[...] = cum[tm - 1:tm, :]


def _kv_proj(h, g, w, b_f, seq, tm=ROW_TILE):
    t, d = h.shape
    row = lambda i: (i, 0)
    const = lambda i: (0, 0)
    return pl.pallas_call(
        functools.partial(_kv_proj_kernel, tiles_per_seq=seq // tm),
        grid=(t // tm,),
        in_specs=[pl.BlockSpec((tm, d), row),
                  pl.BlockSpec((1, d), const),
                  pl.BlockSpec((d, w.shape[1]), const),
                  pl.BlockSpec((1, LANES), const)],
        out_specs=[pl.BlockSpec((tm, HD), row),
                   pl.BlockSpec((tm, 2 * HD), row),
                   pl.BlockSpec((tm, LANES), row)],
        out_shape=[jax.ShapeDtypeStruct((t, HD), BF16),
                   jax.ShapeDtypeStruct((t, 2 * HD), BF16),
                   jax.ShapeDtypeStruct((t, LANES), F32)],
        scratch_shapes=[pltpu.VMEM((1, LANES), F32)],
        compiler_params=_cparams(1),
        name="kv_proj",
    )(h, g, w, b_f)


def _q_proj_kernel(x_ref, g_ref, w_ref, q_ref):
    xn = _rms(x_ref[...], g_ref[...]).astype(BF16)
    seg = 512
    for s in range(HD // seg):
        p = jnp.dot(xn, w_ref[:, s * seg:(s + 1) * seg], preferred_element_type=F32)
        q_ref[:, s * seg:(s + 1) * seg] = (p * (HEAD_DIM ** -0.5 * LOG2E)).astype(BF16)


def _q_proj(h, g, w, tm=ROW_TILE):
    t, d = h.shape
    row = lambda i: (i, 0)
    const = lambda i: (0, 0)
    return pl.pallas_call(
        _q_proj_kernel,
        grid=(t // tm,),
        in_specs=[pl.BlockSpec((tm, d), row),
                  pl.BlockSpec((1, d), const),
                  pl.BlockSpec((d, HD), const)],
        out_specs=pl.BlockSpec((tm, HD), row),
        out_shape=jax.ShapeDtypeStruct((t, HD), BF16),
        compiler_params=_cparams(1),
        name="q_proj",
    )(h, g, w)


def _out_proj_kernel(o_ref, w_ref, h_ref, out_ref):
    out_ref[...] = h_ref[...] + jnp.dot(o_ref[...], w_ref[...], preferred_element_type=F32)


def _out_proj(o, w, h, tm=ROW_TILE):
    t, d = h.shape
    row = lambda i: (i, 0)
    const = lambda i: (0, 0)
    return pl.pallas_call(
        _out_proj_kernel,
        grid=(t // tm,),
        in_specs=[pl.BlockSpec((tm, HD), row),
                  pl.BlockSpec((HD, d), const),
                  pl.BlockSpec((tm, d), row)],
        out_specs=pl.BlockSpec((tm, d), row),
        out_shape=jax.ShapeDtypeStruct((t, d), F32),
        compiler_params=_cparams(1),
        name="out_proj",
    )(o, w, h)


def _stack_pair_queries(q_ref, q2_scr):
    tq = q_ref.shape[0]
    for j in range(N_PAIRS):
        qp = q_ref[:, j * LANES:(j + 1) * LANES]
        first = _half_mask(qp.shape)
        zero = jnp.zeros_like(qp)
        q2_scr[j, 0:tq, :] = jnp.where(first, qp, zero)
        q2_scr[j, tq:2 * tq, :] = jnp.where(first, zero, qp)


def _attn_init(m_scr, l_scr, acc_scr):
    m_scr[0] = jnp.full(m_scr.shape[1:], NEG, F32)
    l_scr[0] = jnp.zeros(l_scr.shape[1:], F32)
    acc_scr[0] = jnp.zeros(acc_scr.shape[1:], F32)


def _attn_block(q2_scr, k_ref, vx_ref, m_scr, l_scr, acc_scr, s_scr, p_scr, src, dst, bias_fn,
                row_bias_fn=None, drift_fn=None, exact_max=True, first_base=None):
    tq = p_scr.shape[0] // 2
    l_lo = l_hi = None
    for j in range(N_PAIRS):
        kp = k_ref[:, j * LANES:(j + 1) * LANES]
        sv = _nt_dot(q2_scr[j], kp)
        alphas = []
        for half in range(2):
            h = 2 * j + half
            rows = slice(half * tq, (half + 1) * tq)
            m_prev = m_scr[src, j, rows, :]
            if exact_max:
                s_scr[rows, :] = bias_fn(sv[rows], h)
                m_blk = jnp.max(s_scr[rows, :], axis=1, keepdims=True)
                if row_bias_fn is not None:
                    m_blk = m_blk + row_bias_fn(h)
                m_new = jnp.maximum(m_prev, m_blk)
                alphas.append(jnp.exp2(m_prev - m_new))
                logits = s_scr[rows, :]
            else:
                logits = bias_fn(sv[rows], h)
                if first_base is not None:
                    base = first_base(sv[rows], logits)
                    if row_bias_fn is not None:
                        base = base + row_bias_fn(h)
                    m_new = base if drift_fn is None else base + drift_fn(h)
                    alphas.append(None)
                elif drift_fn is None:
                    m_new = m_prev
                    alphas.append(None)
                else:
                    m_new = m_prev + drift_fn(h)
                    alphas.append(jnp.exp2(-drift_fn(h)))
            m_scr[dst, j, rows, :] = m_new
            shift = m_new if row_bias_fn is None else m_new - row_bias_fn(h)
            p_scr[rows, :] = jnp.exp2(logits - shift).astype(BF16)
        pv = jnp.dot(p_scr[...], vx_ref[:, 2 * j * LANES:2 * (j + 1) * LANES],
                     preferred_element_type=F32)
        for half in range(2):
            rows = slice(half * tq, (half + 1) * tq)
            acc_old = acc_scr[src, j, rows, :]
            l_old = l_scr[src, j, rows, :]
            if alphas[half] is not None:
                acc_old = alphas[half] * acc_old
                l_old = alphas[half] * l_old
            l_new = l_old + pv[rows, LANES:]
            acc_scr[dst, j, rows, :] = acc_old + pv[rows, :LANES]
            l_scr[dst, j, rows, :] = l_new
            if not exact_max:
                l_hi = l_new if l_hi is None else jnp.maximum(l_hi, l_new)
                l_lo = l_new if l_lo is None else jnp.minimum(l_lo, l_new)
    return l_lo, l_hi


def _sums_usable(l_lo, l_hi):
    return jnp.logical_and(jnp.min(l_lo) > 0.0, jnp.max(l_hi) < 1e37)


def _attn_finish(o_ref, l_scr, acc_scr, slot):
    tq = o_ref.shape[0]
    for j in range(N_PAIRS):
        o2 = acc_scr[slot, j] / l_scr[slot, j]
        o_ref[:, j * LANES:(j + 1) * LANES] = jnp.where(
            _half_mask((tq, LANES)), o2[0:tq], o2[tq:2 * tq]).astype(o_ref.dtype)


def _attn_scratch(tq, tk):
    return [pltpu.VMEM((N_PAIRS, 2 * tq, LANES), BF16),
            pltpu.VMEM((2, N_PAIRS, 2 * tq, 1), F32),
            pltpu.VMEM((2, N_PAIRS, 2 * tq, LANES), F32),
            pltpu.VMEM((2, N_PAIRS, 2 * tq, LANES), F32),
            pltpu.VMEM((2 * tq, tk), F32),
            pltpu.VMEM((2 * tq, tk), BF16),
            pltpu.SMEM((2,), jnp.int32)]


def _ordered_key_to_f32(key):
    return lax.bitcast_convert_type(key ^ ((key >> 31) & jnp.int32(0x7FFFFFFF)), F32)


def _dsa_kernel(b_s, qt_s, ph_s, kb_s, nkb_s,
                qi_ref, wi_ref, ki_ref, q_ref, k_ref, vx_ref, o_ref,
                score_scr, q2_scr, m_scr, l_scr, acc_scr, s_scr, p_scr, state, *, k_sel):
    step = pl.program_id(0)
    qt = qt_s[step]
    phase = ph_s[step]
    kb = kb_s[step]
    nkb = nkb_s[step]
    tq = q_ref.shape[0]
    tk = k_ref.shape[0]

    row = lax.broadcasted_iota(jnp.int32, (tq, 1), 0) + qt * tq
    limit = (row // CHUNK + 1) * CHUNK

    @pl.when(phase == 0)
    def _index():
        kk = ki_ref[...]
        w = wi_ref[...]
        acc = jnp.zeros((tq, tk), F32)
        for j in range(IDX_HEADS // 2):
            qp = qi_ref[:, j * LANES:(j + 1) * LANES]
            first = _half_mask(qp.shape)
            zero = jnp.zeros_like(qp)
            for half in range(2):
                h = 2 * j + half
                qh = jnp.where(first, qp, zero) if half == 0 else jnp.where(first, zero, qp)
                sc = _nt_dot(qh, kk)
                acc = acc + jnp.maximum(sc, 0.0) * w[:, h:h + 1]
        key = lax.broadcasted_iota(jnp.int32, (1, tk), 1) + kb * tk
        sc = jnp.where(key < limit, acc, -jnp.inf)
        score_scr[kb] = sc

    @pl.when(jnp.logical_and(phase == 0, kb == nkb - 1))
    def _select():
        band = 128
        one = jnp.ones((band, LANES), F32)
        zero = jnp.zeros((band, LANES), F32)

        def count(cand, strict):
            c_full = jnp.broadcast_to(cand, (tq, LANES))
            accs = []
            for r0 in range(0, tq, band):
                c_b = c_full[r0:r0 + band]

                def body(j, acc, r0=r0, c_b=c_b):
                    blk = score_scr[j, r0:r0 + band, :]
                    for c in range(tk // LANES):
                        part = blk[:, c * LANES:(c + 1) * LANES]
                        hit = (part > c_b) if strict else (part >= c_b)
                        acc = acc + jnp.where(hit, one, zero)
                    return acc

                accs.append(lax.fori_loop(0, nkb, body, zero))
            return jnp.sum(jnp.concatenate(accs, axis=0), axis=1, keepdims=True)

        def bit_body(i, r):
            cand_u = r | jnp.left_shift(jnp.int32(1), 31 - i)
            cnt = count(_ordered_key_to_f32(cand_u ^ jnp.int32(-2 ** 31)), False)
            return jnp.where(cnt >= k_sel, cand_u, r)

        r = lax.fori_loop(0, 32, bit_body, jnp.zeros((tq, 1), jnp.int32))
        few = limit <= k_sel
        thr = jnp.where(few, -jnp.inf, _ordered_key_to_f32(r ^ jnp.int32(-2 ** 31)))
        need = k_sel - count(thr, True)

        sub = 256
        r_i = lax.broadcasted_iota(jnp.int32, (sub, sub), 0)
        c_i = lax.broadcasted_iota(jnp.int32, (sub, sub), 1)
        tri = jnp.where(r_i <= c_i, 1.0, 0.0).astype(BF16)

        def sel_body(j, carry):
            for c in range(tk // sub):
                cols = slice(c * sub, (c + 1) * sub)
                blk = score_scr[j, :, cols]
                key = lax.broadcasted_iota(jnp.int32, (1, sub), 1) + (j * tk + c * sub)
                eq = blk == thr
                rank = carry + jnp.dot(jnp.where(eq, 1.0, 0.0).astype(BF16), tri,
                                       preferred_element_type=F32)
                bias = jnp.where(eq, jnp.where(rank <= need, 0.0, NEG),
                                 jnp.where(blk > thr, 0.0, NEG))
                score_scr[j, :, cols] = jnp.where(key < limit, bias, NEG)
                carry = rank[:, sub - 1:sub]
            return carry

        lax.fori_loop(0, nkb, sel_body, jnp.zeros((tq, 1), F32))

    attend = functools.partial(_attn_block, q2_scr, k_ref, vx_ref, m_scr, l_scr, acc_scr, s_scr, p_scr,
                               bias_fn=lambda sv, h: sv + score_scr[kb])

    state[1] = 0

    @pl.when(jnp.logical_and(phase == 1, kb == 0))
    def _init():
        _stack_pair_queries(q_ref, q2_scr)
        _attn_init(m_scr, l_scr, acc_scr)
        state[0] = 0
        extremes = attend(src=0, dst=1, exact_max=False,
                          first_base=lambda raw, biased: jnp.max(raw[:, 0:LANES], axis=1, keepdims=True))
        state[1] = jnp.where(_sums_usable(*extremes), 0, 1)

    @pl.when(jnp.logical_and(phase == 1, kb > 0))
    def _one_pass():
        src = state[0]
        extremes = attend(src=src, dst=1 - src, exact_max=False)
        state[1] = jnp.where(_sums_usable(*extremes), 0, 1)

    @pl.when(jnp.logical_and(phase == 1, state[1] == 1))
    def _two_pass():
        src = state[0]
        attend(src=src, dst=1 - src)

    @pl.when(phase == 1)
    def _advance():
        state[0] = 1 - state[0]

    @pl.when(jnp.logical_and(phase == 1, kb == nkb - 1))
    def _finish():
        _attn_finish(o_ref, l_scr, acc_scr, state[0])


def _dsa_schedule(batch, seq, tq, tk):
    rows = []
    for b in range(batch):
        for qt in range(seq // tq):
            nkb = -(-((qt + 1) * tq) // tk)
            for phase in range(2):
                for kb in range(nkb):
                    rows.append((b, qt, phase, kb, nkb))
    return [jnp.asarray(c, jnp.int32) for c in np.asarray(rows, np.int32).T]


def _dsa_attention(qi, wi, ki, q, k, vx, batch, seq, tq=ATTN_TQ, tk=ATTN_TK):
    t = q.shape[0]
    qt_per_b = seq // tq
    kb_per_b = seq // tk
    sched = _dsa_schedule(batch, seq, tq, tk)
    k_sel = min(TOPK_MAX, seq // 4)

    def q_row(i, b, qt, ph, kb, nkb):
        return (b[i] * qt_per_b + qt[i], 0)

    def ki_row(i, b, qt, ph, kb, nkb):
        return (b[i] * kb_per_b + jnp.where(ph[i] == 0, kb[i], nkb[i] - 1), 0)

    def kv_row(i, b, qt, ph, kb, nkb):
        return (b[i] * kb_per_b + jnp.where(ph[i] == 0, 0, kb[i]), 0)

    grid_spec = pltpu.PrefetchScalarGridSpec(
        num_scalar_prefetch=5,
        grid=(int(sched[0].shape[0]),),
        in_specs=[pl.BlockSpec((tq, IDX_HEADS * IDX_DIM), q_row),
                  pl.BlockSpec((tq, LANES), q_row),
                  pl.BlockSpec((tk, LANES), ki_row),
                  pl.BlockSpec((tq, HD), q_row),
                  pl.BlockSpec((tk, HD), kv_row),
                  pl.BlockSpec((tk, 2 * HD), kv_row)],
        out_specs=pl.BlockSpec((tq, HD), q_row),
        scratch_shapes=[pltpu.VMEM((kb_per_b, tq, tk), F32)] + _attn_scratch(tq, tk),
    )
    return pl.pallas_call(
        functools.partial(_dsa_kernel, k_sel=k_sel),
        grid_spec=grid_spec,
        out_shape=jax.ShapeDtypeStruct((t, HD), BF16),
        compiler_params=_cparams(1),
        name="dsa_attention",
    )(*sched, qi, wi, ki, q, k, vx)


def _fox_kernel(b_s, qt_s, kb_s, nkb_s, q_ref, fq_ref, k_ref, vx_ref, fk_ref, o_ref,
                q2_scr, m_scr, l_scr, acc_scr, s_scr, p_scr, state, fq_scr):
    step = pl.program_id(0)
    qt = qt_s[step]
    kb = kb_s[step]
    nkb = nkb_s[step]
    tq = q_ref.shape[0]
    tk = k_ref.shape[0]

    @pl.when(kb == 0)
    def _init():
        _stack_pair_queries(q_ref, q2_scr)
        _attn_init(m_scr, l_scr, acc_scr)
        state[0] = 0
        fq = fq_ref[...]
        for h in range(N_HEADS):
            fq_scr[h] = fq[:, h:h + 1]

    fk = fk_ref[0]
    attend = functools.partial(_attn_block, q2_scr, k_ref, vx_ref, m_scr, l_scr, acc_scr, s_scr, p_scr,
                               row_bias_fn=lambda h: fq_scr[h])
    decay = lambda sv, h: sv - fk[h:h + 1, :]
    diagonal = (kb + 1) * tk > qt * tq
    state[1] = 1

    def masked_decay():
        row = lax.broadcasted_iota(jnp.int32, (tq, 1), 0) + qt * tq
        key = lax.broadcasted_iota(jnp.int32, (1, tk), 1) + kb * tk
        causal = jnp.where(key <= row, 0.0, NEG)
        return lambda sv, h: sv + (causal - fk[h:h + 1, :])

    @pl.when(jnp.logical_and(jnp.logical_not(diagonal), kb > 0))
    def _one_pass():
        src = state[0]
        extremes = attend(src=src, dst=1 - src, bias_fn=decay, exact_max=False,
                          drift_fn=lambda h: fk[h:h + 1, 0:1] - fk[h:h + 1, tk - 1:tk])
        state[1] = jnp.where(_sums_usable(*extremes), 0, 1)

    @pl.when(jnp.logical_and(jnp.logical_not(diagonal), kb == 0))
    def _one_pass_first():
        extremes = attend(src=0, dst=1, bias_fn=decay, exact_max=False,
                          first_base=lambda raw, biased: biased[:, 0:1],
                          drift_fn=lambda h: fk[h:h + 1, 0:1] - fk[h:h + 1, tk - 1:tk])
        state[1] = jnp.where(_sums_usable(*extremes), 0, 1)

    @pl.when(jnp.logical_and(diagonal, kb > 0))
    def _one_pass_masked():
        src = state[0]
        extremes = attend(src=src, dst=1 - src, bias_fn=masked_decay(), exact_max=False,
                          drift_fn=lambda h: fk[h:h + 1, 0:1] - fq_scr[h])
        state[1] = jnp.where(_sums_usable(*extremes), 0, 1)

    @pl.when(jnp.logical_and(jnp.logical_not(diagonal), state[1] == 1))
    def _two_pass():
        src = state[0]
        attend(src=src, dst=1 - src, bias_fn=decay)

    @pl.when(jnp.logical_and(diagonal, state[1] == 1))
    def _two_pass_masked():
        src = state[0]
        attend(src=src, dst=1 - src, bias_fn=masked_decay())

    state[0] = 1 - state[0]

    @pl.when(kb == nkb - 1)
    def _finish():
        _attn_finish(o_ref, l_scr, acc_scr, state[0])


def _fox_schedule(batch, seq, tq, tk):
    rows = []
    for b in range(batch):
        for qt in range(seq // tq):
            nkb = -(-((qt + 1) * tq) // tk)
            for kb in range(nkb):
                rows.append((b, qt, kb, nkb))
    return [jnp.asarray(c, jnp.int32) for c in np.asarray(rows, np.int32).T]


def _fox_attention(q, fq, fk_t, k, vx, batch, seq, tq=ATTN_TQ, tk=ATTN_TK):
    t = q.shape[0]
    qt_per_b = seq // tq
    kb_per_b = seq // tk
    sched = _fox_schedule(batch, seq, tq, tk)

    def q_row(i, b, qt, kb, nkb):
        return (b[i] * qt_per_b + qt[i], 0)

    def kv_row(i, b, qt, kb, nkb):
        return (b[i] * kb_per_b + kb[i], 0)

    def fk_row(i, b, qt, kb, nkb):
        return (b[i], 0, kb[i])

    grid_spec = pltpu.PrefetchScalarGridSpec(
        num_scalar_prefetch=4,
        grid=(int(sched[0].shape[0]),),
        in_specs=[pl.BlockSpec((tq, HD), q_row),
                  pl.BlockSpec((tq, LANES), q_row),
                  pl.BlockSpec((tk, HD), kv_row),
                  pl.BlockSpec((tk, 2 * HD), kv_row),
                  pl.BlockSpec((1, N_HEADS, tk), fk_row)],
        out_specs=pl.BlockSpec((tq, HD), q_row),
        scratch_shapes=_attn_scratch(tq, tk) + [pltpu.VMEM((N_HEADS, tq, 1), F32)],
    )
    return pl.pallas_call(
        _fox_kernel,
        grid_spec=grid_spec,
        out_shape=jax.ShapeDtypeStruct((t, HD), BF16),
        compiler_params=_cparams(1),
        name="fox_attention",
    )(*sched, q, fq, k, vx, fk_t)


N_FF_TILES = 2


def _swiglu_partial(xn, wg_ref, wu_ref, wd_ref):
    lead = (0,) * (len(wg_ref.shape) - 2)
    g = jnp.dot(xn, wg_ref[lead], preferred_element_type=F32)
    u = jnp.dot(xn, wu_ref[lead], preferred_element_type=F32)
    a = g * jax.nn.sigmoid(g) * u
    return jnp.dot(a.astype(BF16), wd_ref[lead], preferred_element_type=F32)


def _ffn_kernel(x_ref, g_ref, wgu_ref, wd_ref, o_ref):
    x = x_ref[...]
    xn = _rms(x, g_ref[...]).astype(BF16)
    n_f = N_FF_TILES
    f_all = wd_ref.shape[1]
    tf = f_all // n_f
    y = x
    for f in range(n_f):
        gcol = slice(f * tf, (f + 1) * tf)
        ucol = slice(f_all + f * tf, f_all + (f + 1) * tf)
        gate = jnp.dot(xn, wgu_ref[0, :, gcol], preferred_element_type=F32)
        up = jnp.dot(xn, wgu_ref[0, :, ucol], preferred_element_type=F32)
        a = (gate * jax.nn.sigmoid(gate) * up).astype(BF16)
        y = y + jnp.dot(a, wd_ref[0, gcol, :], preferred_element_type=F32)
    o_ref[...] = y


def _ffn(h, g, wgu, wd, layer, tm=ROW_TILE):
    t, d = h.shape
    resident = pl.Buffered(1)
    return pl.pallas_call(
        _ffn_kernel,
        grid=(t // tm,),
        in_specs=[pl.BlockSpec((tm, d), lambda i: (i, 0)),
                  pl.BlockSpec((1, d), lambda i: (0, 0)),
                  pl.BlockSpec((1,) + wgu.shape[1:], lambda i: (layer, 0, 0), pipeline_mode=resident),
                  pl.BlockSpec((1,) + wd.shape[1:], lambda i: (layer, 0, 0), pipeline_mode=resident)],
        out_specs=pl.BlockSpec((tm, d), lambda i: (i, 0)),
        out_shape=jax.ShapeDtypeStruct((t, d), F32),
        compiler_params=_cparams(1),
        name="ffn_dense",
    )(h, g, wgu, wd)


META_E1, META_E2, META_POS1, META_POS2, META_G1, META_G2 = range(6)
ROW_TILES = 8


def _to_row_tiles(ref, x):
    rows = x.shape[0]
    for s in range(ROW_TILES):
        ref[pl.ds(s, rows, stride=ROW_TILES), :] = x[:, s * LANES:(s + 1) * LANES]


def _from_row_tiles(ref, rows, s):
    return ref[pl.ds(s, rows, stride=ROW_TILES), :]


def _row_tile(ref, r):
    return ref.at[pl.ds(pl.multiple_of(r * ROW_TILES, ROW_TILES), ROW_TILES)]


def _moe_route_kernel(x_ref, g_ref, wr_ref, xn_ref, meta_ref, cnt_ref, carry_ref):
    i = pl.program_id(0)
    tm = x_ref.shape[0]

    @pl.when(i == 0)
    def _():
        carry_ref[...] = jnp.zeros_like(carry_ref)

    xn = _rms(x_ref[...], g_ref[...])
    _to_row_tiles(xn_ref, xn)
    logits = jnp.dot(xn, wr_ref[...], preferred_element_type=F32, precision=lax.Precision.HIGHEST)
    lane = lax.broadcasted_iota(jnp.int32, (tm, LANES), 1)
    lg = jnp.where(lane < N_EXPERTS, logits, -jnp.inf)
    m1 = jnp.max(lg, axis=1, keepdims=True)
    i1 = jnp.min(jnp.where(lg == m1, lane, LANES), axis=1, keepdims=True)
    lg2 = jnp.where(lane == i1, -jnp.inf, lg)
    m2 = jnp.max(lg2, axis=1, keepdims=True)
    i2 = jnp.min(jnp.where(lg2 == m2, lane, LANES), axis=1, keepdims=True)
    e2 = jnp.exp(m2 - m1)
    den = 1.0 + e2
    routed = jnp.where(jnp.logical_or(lane == i1, lane == i2), 1.0, 0.0)
    r_i = lax.broadcasted_iota(jnp.int32, (tm, tm), 0)
    c_i = lax.broadcasted_iota(jnp.int32, (tm, tm), 1)
    tri = jnp.where(c_i < r_i, 1.0, 0.0).astype(BF16)
    before = jnp.dot(tri, routed.astype(BF16), preferred_element_type=F32) + carry_ref[...]
    pos1 = jnp.sum(jnp.where(lane == i1, before, 0.0), axis=1, keepdims=True)
    pos2 = jnp.sum(jnp.where(lane == i2, before, 0.0), axis=1, keepdims=True)
    total = before[tm - 1:tm, :] + routed[tm - 1:tm, :]
    carry_ref[...] = total
    cnt_ref[...] = total
    meta = jnp.zeros((tm, LANES), F32)
    for col, val in ((META_E1, i1.astype(F32)), (META_E2, i2.astype(F32)), (META_POS1, pos1),
                     (META_POS2, pos2), (META_G1, 1.0 / den), (META_G2, e2 / den)):
        meta = jnp.where(lane == col, val, meta)
    meta_ref[...] = meta


def _moe_route(h, g, w_router, tm=ROW_TILE):
    t, d = h.shape
    return pl.pallas_call(
        _moe_route_kernel,
        grid=(t // tm,),
        in_specs=[pl.BlockSpec((tm, d), lambda i: (i, 0)),
                  pl.BlockSpec((1, d), lambda i: (0, 0)),
                  pl.BlockSpec((d, LANES), lambda i: (0, 0))],
        out_specs=[pl.BlockSpec((tm * ROW_TILES, LANES), lambda i: (i, 0)),
                   pl.BlockSpec((tm, LANES), lambda i: (i, 0)),
                   pl.BlockSpec((1, LANES), lambda i: (0, 0))],
        out_shape=[jax.ShapeDtypeStruct((t * ROW_TILES, LANES), F32),
                   jax.ShapeDtypeStruct((t, LANES), F32),
                   jax.ShapeDtypeStruct((1, LANES), F32)],
        scratch_shapes=[pltpu.VMEM((1, LANES), F32)],
        compiler_params=_cparams(1),
        name="moe_route",
    )(h, g, w_router)


def _moe_expert_kernel(src_s, tile_e_s, n_used_s, xn_hbm, wg_ref, wu_ref, wd_ref, y_ref,
                       rows_scr, xs_scr, acc_scr, sem, *, n_f):
    i = pl.program_id(0)
    f = pl.program_id(1)
    tm = xs_scr.shape[0]
    used = i < n_used_s[0]
    next_used = i + 1 < n_used_s[0]
    slot = i % 2
    part = tm // n_f

    def whole_tile(s):
        return pltpu.make_async_copy(xn_hbm.at[pl.ds(0, tm * ROW_TILES)], rows_scr.at[s], sem.at[s])

    @pl.when(jnp.logical_and(i == 0, f == 0))
    def _first():
        def start(r, c):
            pltpu.make_async_copy(_row_tile(xn_hbm, src_s[r]), _row_tile(rows_scr.at[0], r),
                                  sem.at[0]).start()
            return c
        lax.fori_loop(0, tm, start, 0)

    @pl.when(jnp.logical_and(used, f == 0))
    def _unpack():
        whole_tile(slot).wait()
        for s in range(ROW_TILES):
            xs_scr[:, s * LANES:(s + 1) * LANES] = _from_row_tiles(rows_scr.at[slot], tm, s).astype(BF16)
        acc_scr[...] = jnp.zeros_like(acc_scr)

    def fetch_next():
        base = (i + 1) * tm + f * part
        for r in range(part):
            pltpu.make_async_copy(_row_tile(xn_hbm, src_s[base + r]),
                                  _row_tile(rows_scr.at[1 - slot], f * part + r),
                                  sem.at[1 - slot]).start()

    @pl.when(jnp.logical_and(used, next_used))
    def _compute_and_fetch():
        fetch_next()
        acc_scr[...] += _swiglu_partial(xs_scr[...], wg_ref, wu_ref, wd_ref)

    @pl.when(jnp.logical_and(used, jnp.logical_not(next_used)))
    def _compute():
        acc_scr[...] += _swiglu_partial(xs_scr[...], wg_ref, wu_ref, wd_ref)

    @pl.when(f == n_f - 1)
    def _store():
        @pl.when(used)
        def _():
            _to_row_tiles(y_ref, acc_scr[...])

        @pl.when(jnp.logical_not(used))
        def _():
            y_ref[...] = jnp.zeros_like(y_ref)


def _moe_experts(xn3, src, tile_e, n_used, wgu, wd, layer, tm):
    n_rows = src.shape[0]
    d = wd.shape[-1]
    n_f = N_FF_TILES
    tf = wd.shape[-2] // n_f
    grid_spec = pltpu.PrefetchScalarGridSpec(
        num_scalar_prefetch=3,
        grid=(n_rows // tm, n_f),
        in_specs=[pl.BlockSpec(memory_space=pl.ANY),
                  pl.BlockSpec((1, 1, d, tf), lambda i, f, src, te, nu: (layer, te[i], 0, f)),
                  pl.BlockSpec((1, 1, d, tf), lambda i, f, src, te, nu: (layer, te[i], 0, n_f + f)),
                  pl.BlockSpec((1, 1, tf, d), lambda i, f, src, te, nu: (layer, te[i], f, 0))],
        out_specs=pl.BlockSpec((tm * ROW_TILES, LANES), lambda i, f, src, te, nu: (i, 0)),
        scratch_shapes=[pltpu.VMEM((2, tm * ROW_TILES, LANES), F32),
                        pltpu.VMEM((tm, d), BF16),
                        pltpu.VMEM((tm, d), F32),
                        pltpu.SemaphoreType.DMA((2,))],
    )
    return pl.pallas_call(
        functools.partial(_moe_expert_kernel, n_f=n_f),
        grid_spec=grid_spec,
        out_shape=jax.ShapeDtypeStruct((n_rows * ROW_TILES, LANES), F32),
        compiler_params=_cparams(2),
        name="moe_experts",
    )(src, tile_e, n_used, xn3, wgu, wgu, wd)


def _moe_combine_kernel(d1_s, d2_s, x_ref, meta_ref, gf_ref, y_hbm, o_ref, y_scr, sem, *, final_norm):
    i = pl.program_id(0)
    tm = x_ref.shape[0]
    slot = i % 2

    def fetch(tile, s):
        def start(r, c):
            pltpu.make_async_copy(_row_tile(y_hbm, d1_s[tile * tm + r]), _row_tile(y_scr.at[s, 0], r),
                                  sem.at[s]).start()
            pltpu.make_async_copy(_row_tile(y_hbm, d2_s[tile * tm + r]), _row_tile(y_scr.at[s, 1], r),
                                  sem.at[s]).start()
            return c
        lax.fori_loop(0, tm, start, 0)

    @pl.when(i == 0)
    def _first():
        fetch(0, 0)

    @pl.when(i + 1 < pl.num_programs(0))
    def _next():
        fetch(i + 1, 1 - slot)

    for which in range(2):
        pltpu.make_async_copy(y_hbm.at[pl.ds(0, tm * ROW_TILES)], y_scr.at[slot, which],
                              sem.at[slot]).wait()
    meta = meta_ref[...]
    g1 = meta[:, META_G1:META_G1 + 1]
    g2 = meta[:, META_G2:META_G2 + 1]
    for s in range(ROW_TILES):
        cols = slice(s * LANES, (s + 1) * LANES)
        o_ref[:, cols] = (x_ref[:, cols] + g1 * _from_row_tiles(y_scr.at[slot, 0], tm, s)
                          + g2 * _from_row_tiles(y_scr.at[slot, 1], tm, s))
    if final_norm:
        o_ref[...] = _rms(o_ref[...], gf_ref[...])


def _moe_combine(h, meta, g_final, y3, dest1, dest2, final_norm, tm=ROW_TILE):
    t, d = h.shape
    grid_spec = pltpu.PrefetchScalarGridSpec(
        num_scalar_prefetch=2,
        grid=(t // tm,),
        in_specs=[pl.BlockSpec((tm, d), lambda i, d1, d2: (i, 0)),
                  pl.BlockSpec((tm, LANES), lambda i, d1, d2: (i, 0)),
                  pl.BlockSpec((1, d), lambda i, d1, d2: (0, 0)),
                  pl.BlockSpec(memory_space=pl.ANY)],
        out_specs=pl.BlockSpec((tm, d), lambda i, d1, d2: (i, 0)),
        scratch_shapes=[pltpu.VMEM((2, 2, tm * ROW_TILES, LANES), F32),
                        pltpu.SemaphoreType.DMA((2,))],
    )
    return pl.pallas_call(
        functools.partial(_moe_combine_kernel, final_norm=final_norm),
        grid_spec=grid_spec,
        out_shape=jax.ShapeDtypeStruct((t, d), F32),
        compiler_params=_cparams(1),
        name="moe_combine",
    )(dest1, dest2, h, meta, g_final, y3)


def _moe(h, g, w_router, wgu, wd, layer, g_final, final_norm, tm=ROW_TILE):
    t, d = h.shape
    xn3, meta, cnt = _moe_route(h, g, w_router)
    counts = cnt[0, :N_EXPERTS].astype(jnp.int32)
    padded = (counts + tm - 1) // tm * tm
    ends = jnp.cumsum(padded)
    starts = ends - padded
    idx = meta[:, :META_G1].astype(jnp.int32)
    experts = jnp.arange(N_EXPERTS, dtype=jnp.int32)

    def group_row(e, pos):
        return pos + jnp.sum(jnp.where(e[:, None] == experts[None, :], starts[None, :], 0), axis=1)

    dest1 = group_row(idx[:, META_E1], idx[:, META_POS1])
    dest2 = group_row(idx[:, META_E2], idx[:, META_POS2])
    n_rows = 2 * t + N_EXPERTS * tm
    tok = jnp.arange(t, dtype=jnp.int32)
    src = jnp.zeros((n_rows,), jnp.int32).at[jnp.concatenate([dest1, dest2])].set(
        jnp.concatenate([tok, tok]), unique_indices=True)
    tile_start = jnp.arange(n_rows // tm, dtype=jnp.int32) * tm
    tile_e = jnp.minimum(jnp.sum((ends[None, :] <= tile_start[:, None]).astype(jnp.int32), axis=1),
                         N_EXPERTS - 1)
    n_used = (ends[-1:] // tm).astype(jnp.int32)
    y3 = _moe_experts(xn3, src, tile_e, n_used, wgu, wd, layer, tm)
    return _moe_combine(h, meta, g_final, y3, dest1, dest2, final_norm)


def _rope_tables(seq):
    pos = jnp.arange(seq, dtype=F32)
    inv_freq = 1.0 / (ROPE_THETA ** (jnp.arange(0, HEAD_DIM, 2, dtype=F32) / HEAD_DIM))
    ang = pos[:, None] * inv_freq[None, :]
    cos, sin = jnp.cos(ang), jnp.sin(ang)
    cos_t = jnp.tile(cos, (1, LANES // (HEAD_DIM // 2)))
    sin_t = jnp.tile(jnp.concatenate([-sin, sin], axis=1), (1, LANES // HEAD_DIM))
    return cos_t, sin_t


def kernel(x, a_norm, a_w_in, a_w_out, kv_norm, w_kv, b_f, b_norm, b_w_q, b_w_out, ffn_norm,
           dense_w_gate_up, dense_w_down, moe_router, moe_w_gate_up, moe_w_down, final_norm):
    batch, seq, d = x.shape
    depth = ffn_norm.shape[0]
    n_a = a_norm.shape[0]
    cos_t, sin_t = _rope_tables(seq)
    idx_w = IDX_HEADS * IDX_DIM
    dense_gu, dense_down = dense_w_gate_up.astype(BF16), dense_w_down.astype(BF16)
    moe_gu, moe_down = moe_w_gate_up.astype(BF16), moe_w_down.astype(BF16)

    h = x.reshape(batch * seq, d)
    k_sh = vx_sh = f_sh = fk_t = None
    for i in range(depth):
        if i < n_a:
            w = a_w_in[i]
            ki_w = w[:, 3 * HD + idx_w:3 * HD + idx_w + IDX_DIM]
            wi_w = w[:, 3 * HD + idx_w + IDX_DIM:]
            w_p = jnp.concatenate(
                [w[:, :3 * HD + idx_w], ki_w, ki_w, wi_w,
                 jnp.zeros((d, LANES - IDX_HEADS), w.dtype)], axis=1).astype(BF16)
            q, k, vx, qi, ki, wi = _a_proj(h, a_norm[i][None, :], w_p, cos_t, sin_t, seq)
            o = _dsa_attention(qi, wi, ki, q, k, vx, batch, seq)
            h = _out_proj(o, a_w_out[i].astype(BF16), h)
        else:
            j = i - n_a
            if k_sh is None:
                w_p = jnp.concatenate(
                    [w_kv, jnp.zeros((d, LANES - N_HEADS), w_kv.dtype)], axis=1).astype(BF16)
                b_p = jnp.concatenate([b_f, jnp.zeros((LANES - N_HEADS,), b_f.dtype)])[None, :]
                k_sh, vx_sh, f_sh = _kv_proj(h, kv_norm[None, :], w_p, b_p, seq)
                fk_t = f_sh[:, :N_HEADS].reshape(batch, seq, N_HEADS).transpose(0, 2, 1)
            q = _q_proj(h, b_norm[j][None, :], b_w_q[j].astype(BF16))
            o = _fox_attention(q, f_sh, fk_t, k_sh, vx_sh, batch, seq)
            h = _out_proj(o, b_w_out[j].astype(BF16), h)
        if i % 2 == 0:
            h = _ffn(h, ffn_norm[i][None, :], dense_gu, dense_down, i // 2)
        else:
            e = i // 2
            wr = jnp.concatenate(
                [moe_router[e], jnp.zeros((d, LANES - N_EXPERTS), moe_router.dtype)], axis=1)
            last = i == depth - 1
            h = _moe(h, ffn_norm[i][None, :], wr, moe_gu, moe_down, e, final_norm[None, :], last)
    if depth % 2 == 1:
        raise NotImplementedError("final norm is fused into the last expert mixer")
    return h.reshape(batch, seq, d)
```
